```python
import jax, jax.numpy as jnp
from jax import lax
import numpy as np

D_MODEL = 1024
BATCH = 4
SEQ = 8192
DEPTH = 2

GRID_W = 64
CTX_LEN = 256
ROPE_BASE = 10000.0
ALPHA = (2 * DEPTH) ** 0.25
BETA = (8 * DEPTH) ** -0.25
N_EVEN = (DEPTH + 1) // 2
N_ODD = DEPTH // 2
Q_BLOCK = 128
EPS = 1e-5
NEG = -1e30

H_MLA = 8
MLA_NOPE = 64
MLA_ROPE = 32
MLA_V = 64
MLA_Q_LORA = 384
MLA_KV_LORA = 256

H_RET = 4
RET_DK = 128
RET_DV = 128
RET_CHUNK = 128

AB_IN = MLA_Q_LORA + MLA_KV_LORA + MLA_ROPE + 2 * H_RET * RET_DK + 2 * H_RET * RET_DV
AB_MIX = H_MLA * MLA_V + H_RET * RET_DV

H_M = 8
M_DK = 64
M_DV = 128
M_CHUNK = 64
CONV_W = 3
M_IN = 2 * H_M * M_DK + 2 * H_M * M_DV + 4 * H_M
M_MIX = H_M * M_DV

D_FF = -(-8 * D_MODEL // (3 * 256)) * 256

kernel_name = 'hybrid_mla_retention_mlstm_dit'


def _layer_norm(x, g, b):
    xf = x.astype(jnp.float32)
    mu = jnp.mean(xf, axis=-1, keepdims=True)
    var = jnp.mean(jnp.square(xf - mu), axis=-1, keepdims=True)
    return ((xf - mu) * lax.rsqrt(var + EPS) * g + b).astype(x.dtype)


def _head_norm(x):
    xf = x.astype(jnp.float32)
    mu = jnp.mean(xf, axis=-1, keepdims=True)
    var = jnp.mean(jnp.square(xf - mu), axis=-1, keepdims=True)
    return (xf - mu) * lax.rsqrt(var + EPS)


def _rms_norm(x, g):
    xf = x.astype(jnp.float32)
    return (xf * lax.rsqrt(jnp.mean(jnp.square(xf), axis=-1, keepdims=True) + EPS) * g).astype(x.dtype)


def _modulate(x, shift, scale):
    return x * (1.0 + scale) + shift


def _swiglu(h, w_in, w_out):
    up, gate = jnp.split(h @ w_in, 2, axis=-1)
    return (jax.nn.silu(gate) * up) @ w_out


def _axial_tables(n, d):
    rows = n // GRID_W
    row = jnp.repeat(jnp.arange(rows, dtype=jnp.float32), GRID_W)
    col = jnp.tile(jnp.arange(GRID_W, dtype=jnp.float32), rows)
    da = d // 2
    inv = ROPE_BASE ** (-jnp.arange(0, da, 2, dtype=jnp.float32) / da)
    ar = row[:, None] * inv
    ac = col[:, None] * inv
    return (jnp.cos(ar), jnp.sin(ar), jnp.cos(ac), jnp.sin(ac))


def _rot_half(x, cos, sin):
    x1, x2 = jnp.split(x, 2, axis=-1)
    cos = cos[:, None, :]
    sin = sin[:, None, :]
    return jnp.concatenate([x1 * cos - x2 * sin, x1 * sin + x2 * cos], axis=-1)


def _rope_2d(x, tabs):
    cos_r, sin_r, cos_c, sin_c = tabs
    xr, xc = jnp.split(x, 2, axis=-1)
    return jnp.concatenate([_rot_half(xr, cos_r, sin_r), _rot_half(xc, cos_c, sin_c)], axis=-1).astype(x.dtype)


def _to_heads(a, n_heads):
    b, s, _ = a.shape
    return a.reshape(b, s, n_heads, -1).transpose(0, 2, 1, 3).astype(jnp.float32)


def _seq_flip(a):
    return jnp.flip(a, axis=2)


def _block_attention(q, k, v):
    b, s, h, d = q.shape
    nb = s // Q_BLOCK
    scale = d ** -0.5
    qb = q.reshape(b, nb, Q_BLOCK, h, d).transpose(1, 0, 2, 3, 4)

    def one_block(qblk):
        logits = jnp.einsum('bqhd,bkhd->bhqk', qblk, k).astype(jnp.float32) * scale
        p = jax.nn.softmax(logits, axis=-1).astype(v.dtype)
        return jnp.einsum('bhqk,bkhd->bqhd', p, v)

    o = lax.map(one_block, qb)
    return o.transpose(1, 0, 2, 3, 4).reshape(b, s, h, v.shape[-1])


def _ret_update(r, k, v, lg):
    n = k.shape[2]
    pos = jnp.arange(n, dtype=jnp.float32)
    wk = jnp.exp(lg[:, None] * (n - 1 - pos))
    return r * jnp.exp(lg * n)[None, :, None, None] + jnp.einsum('bhjd,hj,bhjv->bhdv', k, wk, v)


def _ret_scan(q, k, v, lg, r0):
    b, h, s, _ = q.shape
    nc = s // RET_CHUNK

    def chunks(a):
        return jnp.moveaxis(a.reshape((b, h, nc, RET_CHUNK) + a.shape[3:]), 2, 0)

    pos = jnp.arange(RET_CHUNK, dtype=jnp.float32)
    rel = pos[:, None] - pos[None, :]
    d_intra = jnp.where(rel >= 0, jnp.exp(lg[:, None, None] * jnp.maximum(rel, 0.0)), 0.0)
    d_q = jnp.exp(lg[:, None] * (pos + 1.0))

    def step(r, inp):
        qc, kc, vc = inp
        sc = jnp.einsum('bhid,bhjd->bhij', qc, kc) * d_intra
        o = jnp.einsum('bhij,bhjv->bhiv', sc, vc) + jnp.einsum('bhid,hi,bhdv->bhiv', qc, d_q, r)
        return _ret_update(r, kc, vc, lg), o

    r_fin, o = lax.scan(step, r0, (chunks(q), chunks(k), chunks(v)))
    return jnp.moveaxis(o, 0, 2).reshape(b, h, s, -1), r_fin


def _mlstm_update(state, k, v, li, lf):
    c_mat, n_vec, m = state
    b_cum = jnp.cumsum(lf, axis=-1)
    b_end = b_cum[..., -1]
    lw = b_end[..., None] - b_cum + li
    m_new = jnp.maximum(b_end + m, jnp.max(lw, axis=-1))
    w = jnp.exp(lw - m_new[..., None])
    carry = jnp.exp(b_end + m - m_new)
    c_new = carry[..., None, None] * c_mat + jnp.einsum('bhs,bhsd,bhsv->bhdv', w, k, v)
    n_new = carry[..., None] * n_vec + jnp.einsum('bhs,bhsd->bhd', w, k)
    return (c_new, n_new, m_new)


def _mlstm_scan(q, k, v, li, lf, state0):
    b, h, s, _ = q.shape
    nc = s // M_CHUNK

    def chunks(a):
        return jnp.moveaxis(a.reshape((b, h, nc, M_CHUNK) + a.shape[3:]), 2, 0)

    tril = jnp.tril(jnp.ones((M_CHUNK, M_CHUNK), dtype=bool))

    def step(state, inp):
        c_mat, n_vec, m = state
        qc, kc, vc, ic, fc = inp
        b_cum = jnp.cumsum(fc, axis=-1)
        logd = jnp.where(tril, b_cum[..., :, None] - b_cum[..., None, :] + ic[..., None, :], NEG)
        inter = b_cum + m[..., None]
        m_t = jnp.maximum(inter, jnp.max(logd, axis=-1))
        sc = jnp.einsum('bhtd,bhsd->bhts', qc, kc) * jnp.exp(logd - m_t[..., None])
        w_inter = jnp.exp(inter - m_t)
        num = jnp.einsum('bhts,bhsv->bhtv', sc, vc) + w_inter[..., None] * jnp.einsum('bhtd,bhdv->bhtv', qc, c_mat)
        den = jnp.sum(sc, axis=-1) + w_inter * jnp.einsum('bhtd,bhd->bht', qc, n_vec)
        h_t = num / jnp.maximum(jnp.abs(den), jnp.exp(-m_t))[..., None]
        return _mlstm_update(state, kc, vc, ic, fc), h_t

    st, hs = lax.scan(step, state0, (chunks(q), chunks(k), chunks(v), chunks(li), chunks(lf)))
    return jnp.moveaxis(hs, 0, 2).reshape(b, h, s, -1), st


def _centred_conv(x, w, bias):
    ch = x.shape[-1]
    y = lax.conv_general_dilated(x, w[:, None, :], window_strides=(1,), padding='SAME',
                                 dimension_numbers=('NWC', 'WIO', 'NWC'), feature_group_count=ch)
    return y + bias


def _ab_project(h, w_in, q_norm, w_uq, kv_norm, w_ukv, tabs_mla, tabs_ret):
    b, s, _ = h.shape
    o1 = MLA_Q_LORA
    o2 = o1 + MLA_KV_LORA
    o3 = o2 + MLA_ROPE
    o4 = o3 + H_RET * RET_DK
    o5 = o4 + H_RET * RET_DK
    o6 = o5 + H_RET * RET_DV
    cq, ckv, kr, rq, rk, rv, rg = jnp.split(h @ w_in, [o1, o2, o3, o4, o5, o6], axis=-1)
    q = (_rms_norm(cq, q_norm) @ w_uq).reshape(b, s, H_MLA, MLA_NOPE + MLA_ROPE)
    kv = (_rms_norm(ckv, kv_norm) @ w_ukv).reshape(b, s, H_MLA, MLA_NOPE + MLA_V)
    q_nope, q_rope = jnp.split(q, [MLA_NOPE], axis=-1)
    k_nope, v = jnp.split(kv, [MLA_NOPE], axis=-1)
    k_rope = kr[:, :, None, :]
    rq = rq.reshape(b, s, H_RET, RET_DK)
    rk = rk.reshape(b, s, H_RET, RET_DK)
    if tabs_mla is not None:
        q_rope = _rope_2d(q_rope, tabs_mla)
        k_rope = _rope_2d(k_rope, tabs_mla)
        rq = _rope_2d(rq, tabs_ret)
        rk = _rope_2d(rk, tabs_ret)
    q = jnp.concatenate([q_nope, q_rope], axis=-1)
    k = jnp.concatenate([k_nope, jnp.broadcast_to(k_rope, (b, s, H_MLA, MLA_ROPE))], axis=-1)
    rq = rq.transpose(0, 2, 1, 3).astype(jnp.float32)
    rk = rk.transpose(0, 2, 1, 3).astype(jnp.float32) * RET_DK ** -0.5
    rv = _to_heads(rv, H_RET)
    return (q, k, v), (rq, rk, rv), rg


def _ab_merge(att, ret, rg, w_out):
    b, s = att.shape[:2]
    ret = _head_norm(ret.transpose(0, 2, 1, 3)).reshape(b, s, -1)
    ret = (jax.nn.silu(rg.astype(jnp.float32)) * ret).astype(rg.dtype)
    return jnp.concatenate([att.reshape(b, s, -1), ret], axis=-1) @ w_out


def _attn_ret_mixer(h_lat, h_ctx, w_in, q_norm, w_uq, kv_norm, w_ukv, log_decay, w_out,
                    tabs_mla, tabs_ret, need_ctx_out):
    (ql, kl, vl), (rql, rkl, rvl), rgl = _ab_project(h_lat, w_in, q_norm, w_uq, kv_norm, w_ukv, tabs_mla, tabs_ret)
    (qc, kc, vc), (rqc, rkc, rvc), rgc = _ab_project(h_ctx, w_in, q_norm, w_uq, kv_norm, w_ukv, None, None)
    lg_f = log_decay[0].astype(jnp.float32)
    lg_b = log_decay[1].astype(jnp.float32)
    r0 = jnp.zeros((h_lat.shape[0], H_RET, RET_DK, RET_DV), jnp.float32)
    fl = _seq_flip
    if need_ctx_out:
        ret_cf, r_f = _ret_scan(rqc, rkc, rvc, lg_f, r0)
        ret_cb, r_b = _ret_scan(fl(rqc), fl(rkc), fl(rvc), lg_b, r0)
        att_c = _block_attention(qc, kc, vc)
        y_ctx = _ab_merge(att_c, ret_cf + fl(ret_cb), rgc, w_out)
    else:
        r_f = _ret_update(r0, rkc, rvc, lg_f)
        r_b = _ret_update(r0, fl(rkc), fl(rvc), lg_b)
        y_ctx = None
    att_l = _block_attention(ql, jnp.concatenate([kl, kc], axis=1), jnp.concatenate([vl, vc], axis=1))
    ret_lf, _ = _ret_scan(rql, rkl, rvl, lg_f, r_f)
    ret_lb, _ = _ret_scan(fl(rql), fl(rkl), fl(rvl), lg_b, r_b)
    y_lat = _ab_merge(att_l, ret_lf + fl(ret_lb), rgl, w_out)
    return y_lat, y_ctx


def _mlstm_project(h, w_in, conv_w, conv_b, gate_b):
    b, s, _ = h.shape
    o1 = 2 * H_M * M_DK
    o2 = o1 + H_M * M_DV
    o3 = o2 + H_M * M_DV
    qk, v, og, g = jnp.split(h @ w_in, [o1, o2, o3], axis=-1)
    qk = jax.nn.silu(_centred_conv(qk, conv_w, conv_b))
    q, k = jnp.split(qk, 2, axis=-1)
    q = _to_heads(q, H_M)
    k = _to_heads(k, H_M) * M_DK ** -0.5
    v = _to_heads(v, H_M)
    g = (g.reshape(b, s, 4, H_M) + gate_b).astype(jnp.float32).transpose(2, 0, 3, 1)
    fwd = (g[0], jax.nn.log_sigmoid(g[1]))
    bwd = (g[2], jax.nn.log_sigmoid(g[3]))
    return q, k, v, og, fwd, bwd


def _mlstm_merge(h_sum, og, norm_g, w_out):
    b, s = og.shape[:2]
    hn = _head_norm(h_sum.transpose(0, 2, 1, 3)).reshape(b, s, -1) * norm_g
    return (jax.nn.sigmoid(og.astype(jnp.float32)) * hn).astype(og.dtype) @ w_out


def _mlstm_mixer(h_lat, h_ctx, w_in, conv_w, conv_b, gate_b, norm_g, w_out, need_ctx_out):
    ql, kl, vl, ogl, gfl, gbl = _mlstm_project(h_lat, w_in, conv_w, conv_b, gate_b)
    qc, kc, vc, ogc, gfc, gbc = _mlstm_project(h_ctx, w_in, conv_w, conv_b, gate_b)
    b = h_lat.shape[0]
    zero = (jnp.zeros((b, H_M, M_DK, M_DV), jnp.float32), jnp.zeros((b, H_M, M_DK), jnp.float32),
            jnp.zeros((b, H_M), jnp.float32))
    fl = _seq_flip
    if need_ctx_out:
        hcf, st_f = _mlstm_scan(qc, kc, vc, gfc[0], gfc[1], zero)
        hcb, st_b = _mlstm_scan(fl(qc), fl(kc), fl(vc), fl(gbc[0]), fl(gbc[1]), zero)
        y_ctx = _mlstm_merge(hcf + fl(hcb), ogc, norm_g, w_out)
    else:
        st_f = _mlstm_update(zero, kc, vc, gfc[0], gfc[1])
        st_b = _mlstm_update(zero, fl(kc), fl(vc), fl(gbc[0]), fl(gbc[1]))
        y_ctx = None
    hlf, _ = _mlstm_scan(ql, kl, vl, gfl[0], gfl[1], st_f)
    hlb, _ = _mlstm_scan(fl(ql), fl(kl), fl(vl), fl(gbl[0]), fl(gbl[1]), st_b)
    y_lat = _mlstm_merge(hlf + fl(hlb), ogl, norm_g, w_out)
    return y_lat, y_ctx


def setup_inputs(seed: int = 0) -> dict:
    key = jax.random.key(seed)
    ks = jax.random.split(key, 24)
    f32 = jnp.float32
    D = D_MODEL

    def nrm(k, shape, s):
        return s * jax.random.normal(k, shape, f32)

    ret_base = jnp.log1p(-(2.0 ** (-5.0 - jnp.arange(H_RET, dtype=f32))))
    f_lin = jnp.linspace(3.0, 6.0, H_M, dtype=f32)
    zeros_h = jnp.zeros((H_M,), f32)
    gate_base = jnp.stack([zeros_h, f_lin, zeros_h, f_lin])
    return {
        'x': nrm(ks[0], (BATCH, SEQ, D), 1.0),
        'c': nrm(ks[1], (BATCH, D), 1.0),
        'ctx': nrm(ks[2], (BATCH, CTX_LEN, D), 1.0),
        'c_ctx': nrm(ks[3], (D,), 1.0),
        'mod_w': nrm(ks[4], (DEPTH, D, 6 * D), 0.5 * D ** -0.5),
        'mod_b': nrm(ks[5], (DEPTH, 6 * D), 0.02),
        'ln_g': 1.0 + nrm(ks[6], (DEPTH, 2, D), 0.02),
        'ln_b': nrm(ks[7], (DEPTH, 2, D), 0.02),
        'ffn_w_in': nrm(ks[8], (DEPTH, D, 2 * D_FF), D ** -0.5),
        'ffn_w_out': nrm(ks[9], (DEPTH, D_FF, D), BETA * D_FF ** -0.5),
        'ab_w_in': nrm(ks[10], (N_EVEN, D, AB_IN), D ** -0.5),
        'mla_q_norm': 1.0 + nrm(ks[11], (N_EVEN, MLA_Q_LORA), 0.02),
        'mla_w_uq': nrm(ks[12], (N_EVEN, MLA_Q_LORA, H_MLA * (MLA_NOPE + MLA_ROPE)), MLA_Q_LORA ** -0.5),
        'mla_kv_norm': 1.0 + nrm(ks[13], (N_EVEN, MLA_KV_LORA), 0.02),
        'mla_w_ukv': nrm(ks[14], (N_EVEN, MLA_KV_LORA, H_MLA * (MLA_NOPE + MLA_V)), MLA_KV_LORA ** -0.5),
        'ret_log_decay': ret_base * (1.0 + nrm(ks[15], (N_EVEN, 2, H_RET), 0.05)),
        'ab_w_out': nrm(ks[16], (N_EVEN, AB_MIX, D), BETA * AB_MIX ** -0.5),
        'm_w_in': nrm(ks[17], (N_ODD, D, M_IN), D ** -0.5),
        'm_conv_w': nrm(ks[18], (N_ODD, CONV_W, 2 * H_M * M_DK), CONV_W ** -0.5),
        'm_conv_b': nrm(ks[19], (N_ODD, 2 * H_M * M_DK), 0.02),
        'm_gate_b': gate_base[None] + nrm(ks[20], (N_ODD, 4, H_M), 0.1),
        'm_norm_g': 1.0 + nrm(ks[21], (N_ODD, M_MIX), 0.02),
        'm_w_out': nrm(ks[22], (N_ODD, M_MIX, D), BETA * M_MIX ** -0.5),
    }


def reference(x, c, ctx, c_ctx, mod_w, mod_b, ln_g, ln_b, ffn_w_in, ffn_w_out,
              ab_w_in, mla_q_norm, mla_w_uq, mla_kv_norm, mla_w_ukv, ret_log_decay, ab_w_out,
              m_w_in, m_conv_w, m_conv_b, m_gate_b, m_norm_g, m_w_out):
    n_lat = x.shape[1]
    tabs_mla = _axial_tables(n_lat, MLA_ROPE)
    tabs_ret = _axial_tables(n_lat, RET_DK)
    sc = jax.nn.silu(c)
    sc_ctx = jax.nn.silu(c_ctx)
    for l in range(DEPTH):
        last = l == DEPTH - 1
        ml = [m[:, None, :] for m in jnp.split(sc @ mod_w[l] + mod_b[l], 6, axis=-1)]
        mc = jnp.split(sc_ctx @ mod_w[l] + mod_b[l], 6, axis=-1)
        h_lat = _modulate(x, ml[0], ml[1])
        h_ctx = _modulate(ctx, mc[0], mc[1])
        j = l // 2
        if l % 2 == 0:
            y_lat, y_ctx = _attn_ret_mixer(h_lat, h_ctx, ab_w_in[j], mla_q_norm[j], mla_w_uq[j], mla_kv_norm[j],
                                           mla_w_ukv[j], ret_log_decay[j], ab_w_out[j], tabs_mla, tabs_ret,
                                           not last)
        else:
            y_lat, y_ctx = _mlstm_mixer(h_lat, h_ctx, m_w_in[j], m_conv_w[j], m_conv_b[j], m_gate_b[j],
                                        m_norm_g[j], m_w_out[j], not last)
        x = _layer_norm(ALPHA * x + ml[2] * y_lat, ln_g[l, 0], ln_b[l, 0])
        x = _layer_norm(ALPHA * x + ml[5] * _swiglu(_modulate(x, ml[3], ml[4]), ffn_w_in[l], ffn_w_out[l]),
                        ln_g[l, 1], ln_b[l, 1])
        if not last:
            ctx = _layer_norm(ALPHA * ctx + mc[2] * y_ctx, ln_g[l, 0], ln_b[l, 0])
            ctx = _layer_norm(ALPHA * ctx + mc[5] * _swiglu(_modulate(ctx, mc[3], mc[4]), ffn_w_in[l], ffn_w_out[l]),
                              ln_g[l, 1], ln_b[l, 1])
    return x
```

```python
import functools

import jax
import jax.numpy as jnp
import numpy as np
from jax import lax
from jax.experimental import pallas as pl
from jax.experimental.pallas import tpu as pltpu

F32 = jnp.float32
BF16 = jnp.bfloat16
HIGHEST = lax.Precision.HIGHEST

GRID_W = 64
ROPE_BASE = 10000.0
EPS = 1e-5
NEG = -1e30
H_MLA, MLA_NOPE, MLA_ROPE, MLA_V = 8, 64, 32, 64
MLA_Q_LORA, MLA_KV_LORA = 384, 256
H_RET, RET_DK, RET_DV = 4, 128, 128
H_M, M_DK, M_DV, M_CHUNK = 8, 64, 128, 64

LANES = 128
ROW_TILE = 256
SCAN_BLOCK = 128
VMEM_LIMIT = 56 * 1024 * 1024


def _cparams(sem):
    return pltpu.CompilerParams(dimension_semantics=sem, vmem_limit_bytes=VMEM_LIMIT)


def _const_spec(shape):
    nd = len(shape)
    return pl.BlockSpec(shape, lambda *_: (0,) * nd, pipeline_mode=pl.Buffered(1))


def _silu(v):
    return v * jax.nn.sigmoid(v)


def _layer_norm_rows(v, g, b):
    mu = jnp.mean(v, axis=-1, keepdims=True)
    d = v - mu
    var = jnp.mean(d * d, axis=-1, keepdims=True)
    return d * lax.rsqrt(var + EPS) * g + b


def _head_norm_lanes(v):
    outs = []
    for h in range(v.shape[-1] // LANES):
        blk = v[:, h * LANES:(h + 1) * LANES]
        mu = jnp.mean(blk, axis=-1, keepdims=True)
        d = blk - mu
        var = jnp.mean(d * d, axis=-1, keepdims=True)
        outs.append(d * lax.rsqrt(var + EPS))
    return outs


def _mod_kernel(c_ref, w_ref, b_ref, o_ref):
    sc = _silu(c_ref[...])
    o_ref[0] = jnp.dot(sc, w_ref[0], precision=HIGHEST, preferred_element_type=F32) + b_ref[0]


def _modulation(c_rows, mod_w, mod_b):
    depth, d, n = mod_w.shape
    tn = 1536
    return pl.pallas_call(
        _mod_kernel,
        grid=(depth, n // tn),
        in_specs=[pl.BlockSpec((8, d), lambda l, j: (0, 0)),
                  pl.BlockSpec((1, d, tn), lambda l, j: (l, 0, j)),
                  pl.BlockSpec((1, 1, tn), lambda l, j: (l, 0, j))],
        out_specs=pl.BlockSpec((1, 8, tn), lambda l, j: (l, 0, j)),
        out_shape=jax.ShapeDtypeStruct((depth, 8, n), F32),
        compiler_params=_cparams(("arbitrary", "arbitrary")),
        name="modulation",
    )(c_rows, mod_w, mod_b.reshape(depth, 1, n))


_A_CQ, _A_CKV, _A_RQ, _A_RK, _A_RV, _A_RG, _A_RQR, _A_RKR, _A_KR, _A_KRR, _A_END = (
    0, 384, 640, 1152, 1664, 2176, 2688, 3200, 3712, 3840, 3968)


def _ab_in_kernel(x_ref, mod_ref, w_ref, qn_ref, kvn_ref, wuq_ref, wukv_ref, tm_ref, tr_ref,
                  q_ref, k_ref, v_ref, rq_ref, rk_ref, rv_ref, rg_ref):
    x = x_ref[0]
    shift = mod_ref[0, 0, 0:1, :]
    scale = mod_ref[0, 0, 1:2, :]
    h = (x * (1.0 + scale) + shift).astype(BF16)
    y = jnp.dot(h, w_ref[...], preferred_element_type=F32)

    cq = y[:, _A_CQ:_A_CKV]
    ckv = y[:, _A_CKV:_A_RQ]
    ncq = (cq * lax.rsqrt(jnp.mean(cq * cq, axis=-1, keepdims=True) + EPS) * qn_ref[...]).astype(BF16)
    nckv = (ckv * lax.rsqrt(jnp.mean(ckv * ckv, axis=-1, keepdims=True) + EPS) * kvn_ref[...]).astype(BF16)
    q2 = jnp.dot(ncq, wuq_ref[...], preferred_element_type=F32)
    kv = jnp.dot(nckv, wukv_ref[...], preferred_element_type=F32)

    cos_m = tm_ref[0]
    sin_m = tm_ref[1]
    q_scale = (MLA_NOPE + MLA_ROPE) ** -0.5
    k_rope = y[:, _A_KR:_A_KRR] * cos_m + y[:, _A_KRR:_A_END] * sin_m
    nq = H_MLA * LANES
    for hd in range(H_MLA):
        sl = slice(hd * LANES, (hd + 1) * LANES)
        qh = q2[:, sl] * cos_m + q2[:, nq + hd * LANES:nq + (hd + 1) * LANES] * sin_m
        q_ref[0, :, sl] = (qh * q_scale).astype(BF16)
        k_ref[0, :, sl] = (kv[:, sl] + k_rope).astype(BF16)
    v_ref[0] = kv[:, nq:].astype(BF16)

    cos_r = tr_ref[0]
    sin_r = tr_ref[1]
    k_scale = RET_DK ** -0.5
    for hd in range(H_RET):
        sl = slice(hd * LANES, (hd + 1) * LANES)
        rq = y[:, _A_RQ + hd * LANES:_A_RQ + (hd + 1) * LANES] * cos_r \
            + y[:, _A_RQR + hd * LANES:_A_RQR + (hd + 1) * LANES] * sin_r
        rk = y[:, _A_RK + hd * LANES:_A_RK + (hd + 1) * LANES] * cos_r \
            + y[:, _A_RKR + hd * LANES:_A_RKR + (hd + 1) * LANES] * sin_r
        rq_ref[0, :, sl] = rq.astype(BF16)
        rk_ref[0, :, sl] = (rk * k_scale).astype(BF16)
    rv_ref[0] = y[:, _A_RV:_A_RG].astype(BF16)
    rg_ref[0] = y[:, _A_RG:_A_RQR].astype(BF16)


def _ab_in_proj(xa, modsel, w_ext, q_norm, kv_norm, wuq_ext, wukv_ext, tab_mla, tab_ret, nct):
    b, t, d = xa.shape
    tm = ROW_TILE
    row = lambda bi, i: (bi, i, 0)
    n_ret = H_RET * RET_DK
    out_shapes = [jax.ShapeDtypeStruct((b, t, H_MLA * LANES), BF16),
                  jax.ShapeDtypeStruct((b, t, H_MLA * LANES), BF16),
                  jax.ShapeDtypeStruct((b, t, H_MLA * MLA_V), BF16)] + \
                 [jax.ShapeDtypeStruct((b, t, n_ret), BF16)] * 4
    out_specs = [pl.BlockSpec((1, tm, s.shape[-1]), row) for s in out_shapes]
    return pl.pallas_call(
        _ab_in_kernel,
        grid=(b, t // tm),
        in_specs=[pl.BlockSpec((1, tm, d), row),
                  pl.BlockSpec((1, 1, 6, d), lambda bi, i: (bi, (i >= nct).astype(jnp.int32), 0, 0)),
                  _const_spec(w_ext.shape), _const_spec(q_norm.shape), _const_spec(kv_norm.shape),
                  _const_spec(wuq_ext.shape), _const_spec(wukv_ext.shape),
                  pl.BlockSpec((2, tm, LANES), lambda bi, i: (0, i, 0)),
                  pl.BlockSpec((2, tm, LANES), lambda bi, i: (0, i, 0))],
        out_specs=out_specs,
        out_shape=out_shapes,
        compiler_params=_cparams(("parallel", "arbitrary")),
        name="ab_in_proj",
    )(xa, modsel, w_ext, q_norm, kv_norm, wuq_ext, wukv_ext, tab_mla, tab_ret)


def _attn_kernel(q_ref, k_ref, v_ref, o_ref, *, tk, nk_ctx, nk_all, nct):
    i = pl.program_id(2)
    nk = jnp.where(i < nct, nk_ctx, nk_all)
    tq = q_ref.shape[1]
    outs = []
    for hh in range(2):
        q = q_ref[0, :, hh * LANES:(hh + 1) * LANES]

        def body(j, carry, q=q, hh=hh):
            m, l, acc = carry
            r0 = pl.multiple_of(j * tk, tk)
            ks = k_ref[0, pl.ds(r0, tk), hh * LANES:(hh + 1) * LANES]
            vs = v_ref[0, pl.ds(r0, tk), :]
            s = lax.dot_general(q, ks, (((1,), (1,)), ((), ())), preferred_element_type=F32)
            m_new = jnp.maximum(m, jnp.max(s, axis=-1, keepdims=True))
            alpha = jnp.exp(m - m_new)
            p = jnp.exp(s - m_new)
            l = alpha * l + jnp.sum(p, axis=-1, keepdims=True)
            acc = alpha * acc + jnp.dot(p.astype(BF16), vs, preferred_element_type=F32)
            return m_new, l, acc

        init = (jnp.full((tq, 1), NEG, F32), jnp.zeros((tq, 1), F32), jnp.zeros((tq, LANES), F32))
        _, l, acc = lax.fori_loop(0, nk, body, init)
        outs.append(acc / l)
    lane = lax.broadcasted_iota(jnp.int32, (tq, LANES), 1)
    o_ref[0] = jnp.where(lane < MLA_V, outs[0], outs[1]).astype(BF16)


def _attention(q, k, v, ctx_len):
    b, t, _ = q.shape
    tq = ROW_TILE
    tk = 256
    kern = functools.partial(_attn_kernel, tk=tk, nk_ctx=ctx_len // tk, nk_all=t // tk, nct=ctx_len // tq)
    return pl.pallas_call(
        kern,
        grid=(b, H_MLA // 2, t // tq),
        in_specs=[pl.BlockSpec((1, tq, 2 * LANES), lambda bi, hp, i: (bi, i, hp)),
                  pl.BlockSpec((1, t, 2 * LANES), lambda bi, hp, i: (bi, 0, hp)),
                  pl.BlockSpec((1, t, LANES), lambda bi, hp, i: (bi, 0, hp))],
        out_specs=pl.BlockSpec((1, tq, LANES), lambda bi, hp, i: (bi, i, hp)),
        out_shape=jax.ShapeDtypeStruct((b, t, H_MLA * MLA_V), BF16),
        compiler_params=_cparams(("parallel", "parallel", "arbitrary")),
        name="mla_attention",
    )(q, k, v)


def _ret_kernel(ld_ref, qf_ref, ktf_ref, vf_ref, qb_ref, ktb_ref, vb_ref, of_ref, ob_ref,
                state, dmat, dq, wk, gl):
    c = pl.program_id(1)
    L = SCAN_BLOCK

    @pl.when(c == 0)
    def _():
        state[...] = jnp.zeros_like(state)
        row = lax.broadcasted_iota(jnp.int32, (L, L), 0).astype(F32)
        col = lax.broadcasted_iota(jnp.int32, (L, L), 1).astype(F32)
        for d in range(2):
            for hd in range(H_RET):
                lg = ld_ref[d, hd]
                if d == 0:
                    rel = row - col
                    dq[d, hd] = jnp.exp(lg * (row + 1.0))
                    wk[d, hd] = jnp.exp(lg * (L - 1.0 - col))
                else:
                    rel = col - row
                    dq[d, hd] = jnp.exp(lg * (L - row))
                    wk[d, hd] = jnp.exp(lg * col)
                dmat[d, hd] = jnp.where(rel >= 0, jnp.exp(lg * jnp.maximum(rel, 0.0)), 0.0)
                gl[d, hd] = jnp.exp(jnp.zeros((L, L), F32) + lg * L)

    streams = ((qf_ref, ktf_ref, vf_ref, of_ref), (qb_ref, ktb_ref, vb_ref, ob_ref))
    for d, (q_ref, kt_ref, v_ref, o_ref) in enumerate(streams):
        for hd in range(H_RET):
            sl = slice(hd * LANES, (hd + 1) * LANES)
            q = q_ref[0, :, sl]
            kt = kt_ref[0, sl, :]
            v = v_ref[0, :, sl]
            r = state[d, hd]
            s = jnp.dot(q, kt, preferred_element_type=F32) * dmat[d, hd]
            qd = (q.astype(F32) * dq[d, hd]).astype(BF16)
            o = jnp.dot(s.astype(BF16), v, preferred_element_type=F32) \
                + jnp.dot(qd, r.astype(BF16), preferred_element_type=F32)
            o_ref[0, :, sl] = o
            kw = (kt.astype(F32) * wk[d, hd]).astype(BF16)
            state[d, hd] = r * gl[d, hd] + jnp.dot(kw, v, preferred_element_type=F32)


def _rev_block(j, n_ctx_blocks, n_blocks):
    return jnp.where(j < n_ctx_blocks, n_ctx_blocks - 1 - j, n_blocks - 1 + n_ctx_blocks - j)


def _retention(rq, rkt, rv, log_decay, ctx_len):
    b, t, n = rq.shape
    L = SCAN_BLOCK
    nb, ncb = t // L, ctx_len // L
    fwd = lambda bi, c: (bi, c, 0)
    bwd = lambda bi, c: (bi, _rev_block(c, ncb, nb), 0)
    fwd_t = lambda bi, c: (bi, 0, c)
    bwd_t = lambda bi, c: (bi, 0, _rev_block(c, ncb, nb))
    tile = pl.BlockSpec((1, L, n), fwd)
    tile_b = pl.BlockSpec((1, L, n), bwd)
    out = jax.ShapeDtypeStruct((b, t, n), F32)
    sq = pltpu.VMEM((2, H_RET, L, L), F32)
    return pl.pallas_call(
        _ret_kernel,
        grid=(b, nb),
        in_specs=[pl.BlockSpec(memory_space=pltpu.SMEM),
                  tile, pl.BlockSpec((1, n, L), fwd_t), tile,
                  tile_b, pl.BlockSpec((1, n, L), bwd_t), tile_b],
        out_specs=[tile, tile_b],
        out_shape=[out, out],
        scratch_shapes=[sq, sq, sq, sq, sq],
        compiler_params=_cparams(("parallel", "arbitrary")),
        name="retention_scan",
    )(log_decay, rq, rkt, rv, rq, rkt, rv)


def _post_body(x, mod_ref, mix_parts, wo_ref, ln_ref, w1_ref, w2_ref, o_ref, alpha):
    y = None
    off = 0
    for part in mix_parts:
        n = part.shape[-1]
        contrib = jnp.dot(part, wo_ref[off:off + n, :], preferred_element_type=F32)
        y = contrib if y is None else y + contrib
        off += n
    g1 = mod_ref[0, 0, 2:3, :]
    sh2 = mod_ref[0, 0, 3:4, :]
    sc2 = mod_ref[0, 0, 4:5, :]
    g2 = mod_ref[0, 0, 5:6, :]
    x1 = _layer_norm_rows(alpha * x + g1 * y, ln_ref[0:1, :], ln_ref[1:2, :])
    h2 = (x1 * (1.0 + sc2) + sh2).astype(BF16)
    u = jnp.dot(h2, w1_ref[...], preferred_element_type=F32)
    dff = u.shape[-1] // 2
    a = (_silu(u[:, dff:]) * u[:, :dff]).astype(BF16)
    y2 = jnp.dot(a, w2_ref[...], preferred_element_type=F32)
    o_ref[0] = _layer_norm_rows(alpha * x1 + g2 * y2, ln_ref[2:3, :], ln_ref[3:4, :])


def _ab_post_kernel(x_ref, mod_ref, att_ref, rf_ref, rb_ref, rg_ref, wo_ref, ln_ref, w1_ref, w2_ref, o_ref, *, alpha):
    ret = rf_ref[0] + rb_ref[0]
    rg = rg_ref[0].astype(F32)
    gate = _silu(rg)
    normed = _head_norm_lanes(ret)
    parts = [att_ref[0]]
    for hd, nh in enumerate(normed):
        parts.append((gate[:, hd * LANES:(hd + 1) * LANES] * nh).astype(BF16))
    _post_body(x_ref[0], mod_ref, parts, wo_ref, ln_ref, w1_ref, w2_ref, o_ref, alpha)


def _m_post_kernel(x_ref, mod_ref, hf_ref, hb_ref, og_ref, ng_ref, wo_ref, ln_ref, w1_ref, w2_ref, o_ref, *, alpha):
    hs = hf_ref[0] + hb_ref[0]
    og = jax.nn.sigmoid(og_ref[0].astype(F32))
    normed = _head_norm_lanes(hs)
    parts = []
    for hd, nh in enumerate(normed):
        sl = slice(hd * LANES, (hd + 1) * LANES)
        parts.append((og[:, sl] * (nh * ng_ref[:, sl])).astype(BF16))
    _post_body(x_ref[0], mod_ref, parts, wo_ref, ln_ref, w1_ref, w2_ref, o_ref, alpha)


def _post(kernel_fn, xa, modsel, acts, consts, w_out, ln, w1, w2, nct, row0_tiles, alpha):
    b, t, d = xa.shape
    tm = ROW_TILE
    n_tiles = t // tm - row0_tiles
    row = lambda bi, i: (bi, i + row0_tiles, 0)
    in_specs = [pl.BlockSpec((1, tm, d), row),
                pl.BlockSpec((1, 1, 6, d), lambda bi, i: (bi, (i + row0_tiles >= nct).astype(jnp.int32), 0, 0))]
    in_specs += [pl.BlockSpec((1, tm, a.shape[-1]), row) for a in acts]
    in_specs += [_const_spec(cst.shape) for cst in consts]
    in_specs += [_const_spec(w_out.shape), _const_spec(ln.shape), _const_spec(w1.shape), _const_spec(w2.shape)]
    return pl.pallas_call(
        functools.partial(kernel_fn, alpha=alpha),
        grid=(b, n_tiles),
        in_specs=in_specs,
        out_specs=pl.BlockSpec((1, tm, d), lambda bi, i: (bi, i, 0)),
        out_shape=jax.ShapeDtypeStruct((b, n_tiles * tm, d), F32),
        compiler_params=_cparams(("parallel", "arbitrary")),
        name="post_" + kernel_fn.__name__,
    )(xa, modsel, *acts, *consts, w_out, ln, w1, w2)


_M_QK, _M_V, _M_OG, _M_G, _M_END = 0, 2048, 3072, 4096, 4224


def _m_in_kernel(x_ref, xp_ref, xn_ref, mod_ref, w_ref, cw_ref, cb_ref, gb_ref,
                 q_ref, k_ref, v_ref, og_ref, g_ref, *, seg_starts, seg_ends):
    i = pl.program_id(1)
    tm = x_ref.shape[1]
    shift = mod_ref[0, 0, 0:1, :]
    scale = mod_ref[0, 0, 1:2, :]
    h = (x_ref[0] * (1.0 + scale) + shift).astype(BF16)
    y = jnp.dot(h, w_ref[...], preferred_element_type=F32)
    hp = (xp_ref[0] * (1.0 + scale) + shift).astype(BF16)
    hn = (xn_ref[0] * (1.0 + scale) + shift).astype(BF16)
    wqk = w_ref[:, _M_QK:_M_V]
    up = jnp.dot(hp, wqk, preferred_element_type=F32)[7:8, :]
    un = jnp.dot(hn, wqk, preferred_element_type=F32)[0:1, :]
    is_start = functools.reduce(jnp.logical_or, [i == s for s in seg_starts])
    is_end = functools.reduce(jnp.logical_or, [i == e for e in seg_ends])
    up = jnp.where(is_start, 0.0, up)
    un = jnp.where(is_end, 0.0, un)

    u = y[:, _M_QK:_M_V]
    rows = lax.broadcasted_iota(jnp.int32, u.shape, 0)
    u_prev = jnp.where(rows == 0, up, pltpu.roll(u, 1, axis=0))
    u_next = jnp.where(rows == tm - 1, un, pltpu.roll(u, tm - 1, axis=0))
    qk = _silu(cw_ref[0:1, :] * u_prev + cw_ref[1:2, :] * u + cw_ref[2:3, :] * u_next + cb_ref[...])
    nq = H_M * LANES
    q_ref[0] = qk[:, :nq].astype(BF16)
    k_ref[0] = (qk[:, nq:] * (M_DK ** -0.5)).astype(BF16)
    v_ref[0] = y[:, _M_V:_M_OG].astype(BF16)
    og_ref[0] = y[:, _M_OG:_M_G].astype(BF16)

    g = y[:, _M_G:_M_END] + gb_ref[...]
    lsg = jnp.minimum(g, 0.0) - jnp.log1p(jnp.exp(-jnp.abs(g)))
    r = lax.broadcasted_iota(jnp.int32, (tm, tm), 0)
    cidx = lax.broadcasted_iota(jnp.int32, (tm, tm), 1)
    same = (r // M_CHUNK) == (cidx // M_CHUNK)
    tril = jnp.where(jnp.logical_and(same, cidx <= r), 1.0, 0.0)
    triu = jnp.where(jnp.logical_and(same, cidx >= r), 1.0, 0.0)
    pre = jnp.dot(tril, lsg, precision=HIGHEST, preferred_element_type=F32)
    suf = jnp.dot(triu, lsg, precision=HIGHEST, preferred_element_type=F32)
    lane = lax.broadcasted_iota(jnp.int32, g.shape, 1)
    grp = lane // H_M
    g_ref[0] = jnp.where(grp == 1, pre, jnp.where(grp == 3, suf, g))


def _m_in_proj(xa, modsel, w_ext, conv_w, conv_b, gate_b, nct):
    b, t, d = xa.shape
    tm = ROW_TILE
    nt = t // tm
    r8 = tm // 8
    row = lambda bi, i: (bi, i, 0)
    prev = lambda bi, i: (bi, jnp.maximum(i * r8 - 1, 0), 0)
    nxt = lambda bi, i: (bi, jnp.minimum((i + 1) * r8, t // 8 - 1), 0)
    widths = (H_M * LANES, H_M * LANES, H_M * M_DV, H_M * M_DV)
    out_shapes = [jax.ShapeDtypeStruct((b, t, w), BF16) for w in widths] + \
                 [jax.ShapeDtypeStruct((b, t, LANES), F32)]
    out_specs = [pl.BlockSpec((1, tm, s.shape[-1]), row) for s in out_shapes]
    kern = functools.partial(_m_in_kernel, seg_starts=(0, nct), seg_ends=(nct - 1, nt - 1))
    return pl.pallas_call(
        kern,
        grid=(b, nt),
        in_specs=[pl.BlockSpec((1, tm, d), row),
                  pl.BlockSpec((1, 8, d), prev),
                  pl.BlockSpec((1, 8, d), nxt),
                  pl.BlockSpec((1, 1, 6, d), lambda bi, i: (bi, (i >= nct).astype(jnp.int32), 0, 0)),
                  _const_spec(w_ext.shape), _const_spec(conv_w.shape), _const_spec(conv_b.shape),
                  _const_spec(gate_b.shape)],
        out_specs=out_specs,
        out_shape=out_shapes,
        compiler_params=_cparams(("parallel", "arbitrary")),
        name="m_in_proj",
    )(xa, xa, xa, modsel, w_ext, conv_w, conv_b, gate_b)


def _mlstm_chunk(q, kt, v, gc, gt, hd, backward, cext, m_prev):
    L = M_CHUNK
    base = 2 * H_M if backward else 0
    i_row = gt[base + hd:base + hd + 1, :]
    b_row = gt[base + H_M + hd:base + H_M + hd + 1, :]
    b_col = gc[:, base + H_M + hd:base + H_M + hd + 1]
    t_idx = lax.broadcasted_iota(jnp.int32, (L, L), 0)
    s_idx = lax.broadcasted_iota(jnp.int32, (L, L), 1)
    visible = (s_idx >= t_idx) if backward else (s_idx <= t_idx)
    logd = jnp.where(visible, b_col - b_row + i_row, NEG)
    inter = b_col + m_prev
    m_t = jnp.maximum(inter, jnp.max(logd, axis=-1, keepdims=True))
    sc = jnp.dot(q, kt, preferred_element_type=F32) * jnp.exp(logd - m_t)
    w_inter = jnp.exp(inter - m_t)
    ones = jnp.ones((L, LANES), BF16)
    vext = jnp.concatenate([v, ones], axis=-1)
    nd = jnp.dot(sc.astype(BF16), vext, preferred_element_type=F32) \
        + w_inter * jnp.dot(q, cext.astype(BF16), preferred_element_type=F32)
    h = nd[:, :LANES] / jnp.maximum(jnp.abs(nd[:, LANES:]), jnp.exp(-m_t))
    b_end = b_row[:, 0:1] if backward else b_row[:, L - 1:L]
    lw = b_end - b_row + i_row
    m_new = jnp.maximum(b_end + m_prev, jnp.max(lw, axis=-1, keepdims=True))
    w = jnp.exp(lw - m_new)
    carry = jnp.exp(b_end + m_prev - m_new)
    kw = (kt.astype(F32) * w).astype(BF16)
    cext_new = carry * cext + jnp.dot(kw, vext, preferred_element_type=F32)
    return h, cext_new, m_new


def _mlstm_kernel(qf_ref, ktf_ref, vf_ref, gf_ref, gtf_ref, qb_ref, ktb_ref, vb_ref, gb_ref, gtb_ref,
                  hf_ref, hb_ref, cstate, mstate):
    c = pl.program_id(1)

    @pl.when(c == 0)
    def _():
        cstate[...] = jnp.zeros_like(cstate)
        mstate[...] = jnp.zeros_like(mstate)

    L = M_CHUNK
    n_sub = SCAN_BLOCK // L
    streams = ((qf_ref, ktf_ref, vf_ref, gf_ref, gtf_ref, hf_ref),
               (qb_ref, ktb_ref, vb_ref, gb_ref, gtb_ref, hb_ref))
    for d, (q_ref, kt_ref, v_ref, g_ref, gt_ref, h_ref) in enumerate(streams):
        order = range(n_sub) if d == 0 else range(n_sub - 1, -1, -1)
        for hd in range(H_M):
            sl = slice(hd * LANES, (hd + 1) * LANES)
            ch = d * H_M + hd
            cext = cstate[ch]
            m_prev = mstate[ch][0:1, 0:1]
            for sub in order:
                rows = slice(sub * L, (sub + 1) * L)
                h, cext, m_prev = _mlstm_chunk(
                    q_ref[0, rows, sl], kt_ref[0, sl, rows], v_ref[0, rows, sl],
                    g_ref[0, rows, :], gt_ref[0, :, rows], hd, d == 1, cext, m_prev)
                h_ref[0, rows, sl] = h
            cstate[ch] = cext
            mstate[ch] = jnp.broadcast_to(m_prev, mstate.shape[1:])


def _mlstm(q, kt, v, g, gt, ctx_len):
    b, t, n = v.shape
    L = SCAN_BLOCK
    nb, ncb = t // L, ctx_len // L
    fwd = lambda bi, c: (bi, c, 0)
    bwd = lambda bi, c: (bi, _rev_block(c, ncb, nb), 0)
    fwd_t = lambda bi, c: (bi, 0, c)
    bwd_t = lambda bi, c: (bi, 0, _rev_block(c, ncb, nb))

    def specs(rmap, tmap):
        return [pl.BlockSpec((1, L, q.shape[-1]), rmap), pl.BlockSpec((1, kt.shape[1], L), tmap),
                pl.BlockSpec((1, L, n), rmap), pl.BlockSpec((1, L, LANES), rmap),
                pl.BlockSpec((1, gt.shape[1], L), tmap)]

    out = jax.ShapeDtypeStruct((b, t, n), F32)
    return pl.pallas_call(
        _mlstm_kernel,
        grid=(b, nb),
        in_specs=specs(fwd, fwd_t) + specs(bwd, bwd_t),
        out_specs=[pl.BlockSpec((1, L, n), fwd), pl.BlockSpec((1, L, n), bwd)],
        out_shape=[out, out],
        scratch_shapes=[pltpu.VMEM((2 * H_M, LANES, 2 * LANES), F32), pltpu.VMEM((2 * H_M, 8, LANES), F32)],
        compiler_params=_cparams(("parallel", "arbitrary")),
        name="mlstm_scan",
    )(q, kt, v, g, gt, q, kt, v, g, gt)


def _pair_swap_index(d):
    q = d // 4
    j = np.arange(d)
    return np.where((j // q) % 2 == 0, j + q, j - q)


def _rope_tables(n_lat, ctx_len, d, lane0):
    rows = n_lat // GRID_W
    row = jnp.repeat(jnp.arange(rows, dtype=F32), GRID_W)
    col = jnp.tile(jnp.arange(GRID_W, dtype=F32), rows)
    da = d // 2
    inv = ROPE_BASE ** (-jnp.arange(0, da, 2, dtype=F32) / da)
    ar = row[:, None] * inv
    ac = col[:, None] * inv
    cos = jnp.concatenate([jnp.cos(ar), jnp.cos(ar), jnp.cos(ac), jnp.cos(ac)], axis=-1)
    sin = jnp.concatenate([-jnp.sin(ar), jnp.sin(ar), -jnp.sin(ac), jnp.sin(ac)], axis=-1)
    cos_full = jnp.ones((ctx_len + n_lat, LANES), F32).at[ctx_len:, lane0:lane0 + d].set(cos)
    sin_full = jnp.zeros((ctx_len + n_lat, LANES), F32).at[ctx_len:, lane0:lane0 + d].set(sin)
    return jnp.stack([cos_full, sin_full])


def _pad_heads(w, n_heads, width):
    k = w.shape[0]
    w = w.reshape(k, n_heads, width)
    return jnp.pad(w, ((0, 0), (0, 0), (0, LANES - width))).reshape(k, n_heads * LANES)


def _prep_ab_weights(w_in, w_uq, w_ukv):
    o1 = MLA_Q_LORA
    o2 = o1 + MLA_KV_LORA
    o3 = o2 + MLA_ROPE
    o4 = o3 + H_RET * RET_DK
    o5 = o4 + H_RET * RET_DK
    o6 = o5 + H_RET * RET_DV
    d = w_in.shape[0]
    cq, ckv, kr, rq, rk, rv, rg = (w_in[:, :o1], w_in[:, o1:o2], w_in[:, o2:o3], w_in[:, o3:o4],
                                   w_in[:, o4:o5], w_in[:, o5:o6], w_in[:, o6:])
    ret_swap = np.concatenate([hd * RET_DK + _pair_swap_index(RET_DK) for hd in range(H_RET)])
    kr_blk = jnp.zeros((d, LANES), F32).at[:, MLA_NOPE:MLA_NOPE + MLA_ROPE].set(kr)
    krr_blk = jnp.zeros((d, LANES), F32).at[:, MLA_NOPE:MLA_NOPE + MLA_ROPE].set(kr[:, _pair_swap_index(MLA_ROPE)])
    w_ext = jnp.concatenate([cq, ckv, rq, rk, rv, rg, rq[:, ret_swap], rk[:, ret_swap], kr_blk, krr_blk], axis=1)

    dqh = MLA_NOPE + MLA_ROPE
    uq = w_uq.reshape(-1, H_MLA, dqh)
    uq_rot = jnp.concatenate([jnp.zeros_like(uq[..., :MLA_NOPE]),
                              uq[..., MLA_NOPE:][..., _pair_swap_index(MLA_ROPE)]], axis=-1)
    wuq_ext = jnp.concatenate([_pad_heads(uq.reshape(-1, H_MLA * dqh), H_MLA, dqh),
                               _pad_heads(uq_rot.reshape(-1, H_MLA * dqh), H_MLA, dqh)], axis=1)
    ukv = w_ukv.reshape(-1, H_MLA, MLA_NOPE + MLA_V)
    k_nope = _pad_heads(ukv[..., :MLA_NOPE].reshape(-1, H_MLA * MLA_NOPE), H_MLA, MLA_NOPE)
    v_cols = ukv[..., MLA_NOPE:].reshape(-1, H_MLA * MLA_V)
    wukv_ext = jnp.concatenate([k_nope, v_cols], axis=1)
    return w_ext.astype(BF16), wuq_ext.astype(BF16), wukv_ext.astype(BF16)


def _prep_m_weights(w_in, conv_w, conv_b, gate_b):
    o1 = 2 * H_M * M_DK
    o2 = o1 + H_M * M_DV
    o3 = o2 + H_M * M_DV
    qk_pad = _pad_heads(w_in[:, :o1], 2 * H_M, M_DK)
    g_pad = jnp.pad(w_in[:, o3:], ((0, 0), (0, LANES - 4 * H_M)))
    w_ext = jnp.concatenate([qk_pad, w_in[:, o1:o2], w_in[:, o2:o3], g_pad], axis=1).astype(BF16)
    cw = _pad_heads(conv_w, 2 * H_M, M_DK)
    cb = _pad_heads(conv_b[None, :], 2 * H_M, M_DK)
    gb = jnp.pad(gate_b.reshape(1, 4 * H_M), ((0, 0), (0, LANES - 4 * H_M)))
    return w_ext, cw, cb, gb


def kernel(x, c, ctx, c_ctx, mod_w, mod_b, ln_g, ln_b, ffn_w_in, ffn_w_out, ab_w_in, mla_q_norm, mla_w_uq,
           mla_kv_norm, mla_w_ukv, ret_log_decay, ab_w_out, m_w_in, m_conv_w, m_conv_b, m_gate_b, m_norm_g, m_w_out):
    b, s, d = x.shape
    ctx_len = ctx.shape[1]
    depth = mod_w.shape[0]
    assert ctx_len % ROW_TILE == 0 and s % ROW_TILE == 0 and s % GRID_W == 0 and b + 1 <= 8
    nct = ctx_len // ROW_TILE
    alpha = (2 * depth) ** 0.25

    c_rows = jnp.zeros((8, d), F32).at[:b].set(c).at[b].set(c_ctx)
    mods = _modulation(c_rows, mod_w, mod_b).reshape(depth, 8, 6, d)
    tab_mla = _rope_tables(s, ctx_len, MLA_ROPE, MLA_NOPE)
    tab_ret = _rope_tables(s, ctx_len, RET_DK, 0)

    xa = jnp.concatenate([ctx, x], axis=1)
    for l in range(depth):
        last = l == depth - 1
        j = l // 2
        modsel = jnp.stack([jnp.broadcast_to(mods[l, b], (b, 6, d)), mods[l, :b]], axis=1)
        ln = jnp.stack([ln_g[l, 0], ln_b[l, 0], ln_g[l, 1], ln_b[l, 1]])
        w1 = ffn_w_in[l].astype(BF16)
        w2 = ffn_w_out[l].astype(BF16)
        row0 = nct if last else 0
        if l % 2 == 0:
            w_ext, wuq_ext, wukv_ext = _prep_ab_weights(ab_w_in[j], mla_w_uq[j], mla_w_ukv[j])
            q, k, v, rq, rk, rv, rg = _ab_in_proj(xa, modsel, w_ext, mla_q_norm[j][None, :], mla_kv_norm[j][None, :],
                                                  wuq_ext, wukv_ext, tab_mla, tab_ret, nct)
            att = _attention(q, k, v, ctx_len)
            ret_f, ret_b = _retention(rq, jnp.swapaxes(rk, 1, 2), rv, ret_log_decay[j], ctx_len)
            xa_new = _post(_ab_post_kernel, xa, modsel, [att, ret_f, ret_b, rg], [],
                           ab_w_out[j].astype(BF16), ln, w1, w2, nct, row0, alpha)
        else:
            w_ext, cw, cb, gb = _prep_m_weights(m_w_in[j], m_conv_w[j], m_conv_b[j], m_gate_b[j])
            q, k, v, og, g = _m_in_proj(xa, modsel, w_ext, cw, cb, gb, nct)
            gt = jnp.swapaxes(g[:, :, :4 * H_M], 1, 2)
            hf, hb = _mlstm(q, jnp.swapaxes(k, 1, 2), v, g, gt, ctx_len)
            xa_new = _post(_m_post_kernel, xa, modsel, [hf, hb, og], [m_norm_g[j][None, :]],
                           m_w_out[j].astype(BF16), ln, w1, w2, nct, row0, alpha)
        xa = xa_new if last else xa_new
        if last:
            return xa
    return xa[:, ctx_len:]
```

```python
import functools

import jax
import jax.numpy as jnp
import numpy as np
from jax import lax
from jax.experimental import pallas as pl
from jax.experimental.pallas import tpu as pltpu

F32 = jnp.float32
BF16 = jnp.bfloat16
HIGHEST = lax.Precision.HIGHEST

GRID_W = 64
ROPE_BASE = 10000.0
EPS = 1e-5
NEG = -1e30
LOG2E = 1.4426950408889634
H_MLA, MLA_NOPE, MLA_ROPE, MLA_V = 8, 64, 32, 64
MLA_Q_LORA, MLA_KV_LORA = 384, 256
H_RET, RET_DK, RET_DV = 4, 128, 128
H_M, M_DK, M_DV = 8, 64, 128

LANES = 128
ROW_TILE = 256
SCAN_BLOCK = 128
ATT_KEY_CHUNK = 512
M_SCAN_CHUNK = SCAN_BLOCK
VMEM_LIMIT = 56 * 1024 * 1024


def _cparams(sem):
    return pltpu.CompilerParams(dimension_semantics=sem, vmem_limit_bytes=VMEM_LIMIT)


def _const_spec(shape):
    nd = len(shape)
    return pl.BlockSpec(shape, lambda *_: (0,) * nd, pipeline_mode=pl.Buffered(1))


def _silu(v):
    return v * jax.nn.sigmoid(v)


def _layer_norm_rows(v, g, b):
    mu = jnp.mean(v, axis=-1, keepdims=True)
    d = v - mu
    var = jnp.mean(d * d, axis=-1, keepdims=True)
    return d * lax.rsqrt(var + EPS) * g + b


def _head_norm_lanes(v):
    outs = []
    for h in range(v.shape[-1] // LANES):
        blk = v[:, h * LANES:(h + 1) * LANES]
        mu = jnp.mean(blk, axis=-1, keepdims=True)
        d = blk - mu
        var = jnp.mean(d * d, axis=-1, keepdims=True)
        outs.append(d * lax.rsqrt(var + EPS))
    return outs


def _mod_kernel(c_ref, w_ref, b_ref, o_ref):
    sc = _silu(c_ref[...])
    o_ref[0] = jnp.dot(sc, w_ref[0], precision=HIGHEST, preferred_element_type=F32) + b_ref[0]


def _modulation(c_rows, mod_w, mod_b):
    depth, d, n = mod_w.shape
    tn = 1536
    return pl.pallas_call(
        _mod_kernel,
        grid=(depth, n // tn),
        in_specs=[pl.BlockSpec((8, d), lambda l, j: (0, 0)),
                  pl.BlockSpec((1, d, tn), lambda l, j: (l, 0, j)),
                  pl.BlockSpec((1, 1, tn), lambda l, j: (l, 0, j))],
        out_specs=pl.BlockSpec((1, 8, tn), lambda l, j: (l, 0, j)),
        out_shape=jax.ShapeDtypeStruct((depth, 8, n), F32),
        compiler_params=_cparams(("arbitrary", "arbitrary")),
        name="modulation",
    )(c_rows, mod_w, mod_b.reshape(depth, 1, n))


_A_CQ, _A_CKV, _A_RQ, _A_RK, _A_RV, _A_RG, _A_RQR, _A_RKR, _A_KR, _A_KRR, _A_END = (
    0, 384, 640, 1152, 1664, 2176, 2688, 3200, 3712, 3840, 3968)


def _ab_in_kernel(x_ref, mod_ref, w_ref, qn_ref, kvn_ref, wuq_ref, wukv_ref, tm_ref, tr_ref,
                  q_ref, k_ref, v_ref, rq_ref, rk_ref, rv_ref, rg_ref):
    x = x_ref[0]
    shift = mod_ref[0, 0, 0:1, :]
    scale = mod_ref[0, 0, 1:2, :]
    h = (x * (1.0 + scale) + shift).astype(BF16)
    y = jnp.dot(h, w_ref[...], preferred_element_type=F32)

    cq = y[:, _A_CQ:_A_CKV]
    ckv = y[:, _A_CKV:_A_RQ]
    ncq = (cq * lax.rsqrt(jnp.mean(cq * cq, axis=-1, keepdims=True) + EPS) * qn_ref[...]).astype(BF16)
    nckv = (ckv * lax.rsqrt(jnp.mean(ckv * ckv, axis=-1, keepdims=True) + EPS) * kvn_ref[...]).astype(BF16)
    q2 = jnp.dot(ncq, wuq_ref[...], preferred_element_type=F32)
    kv = jnp.dot(nckv, wukv_ref[...], preferred_element_type=F32)

    cos_m = tm_ref[0]
    sin_m = tm_ref[1]
    q_scale = (MLA_NOPE + MLA_ROPE) ** -0.5 * LOG2E
    k_rope = y[:, _A_KR:_A_KRR] * cos_m + y[:, _A_KRR:_A_END] * sin_m
    nq = H_MLA * LANES
    for hd in range(H_MLA):
        sl = slice(hd * LANES, (hd + 1) * LANES)
        qh = q2[:, sl] * cos_m + q2[:, nq + hd * LANES:nq + (hd + 1) * LANES] * sin_m
        q_ref[0, :, sl] = (qh * q_scale).astype(BF16)
        k_ref[0, :, sl] = (kv[:, sl] + k_rope).astype(BF16)
    lane = lax.broadcasted_iota(jnp.int32, (1, nq), 1)
    ones_half = jnp.where(((lane // MLA_V) % 2) != ((lane // LANES) % 2), 1.0, 0.0)
    v_ref[0] = (kv[:, nq:] + ones_half).astype(BF16)

    cos_r = tr_ref[0]
    sin_r = tr_ref[1]
    k_scale = RET_DK ** -0.5
    for hd in range(H_RET):
        sl = slice(hd * LANES, (hd + 1) * LANES)
        rq = y[:, _A_RQ + hd * LANES:_A_RQ + (hd + 1) * LANES] * cos_r \
            + y[:, _A_RQR + hd * LANES:_A_RQR + (hd + 1) * LANES] * sin_r
        rk = y[:, _A_RK + hd * LANES:_A_RK + (hd + 1) * LANES] * cos_r \
            + y[:, _A_RKR + hd * LANES:_A_RKR + (hd + 1) * LANES] * sin_r
        rq_ref[0, :, sl] = rq.astype(BF16)
        rk_ref[0, :, sl] = (rk * k_scale).astype(BF16)
    rv_ref[0] = y[:, _A_RV:_A_RG].astype(BF16)
    rg_ref[0] = y[:, _A_RG:_A_RQR].astype(BF16)


def _ab_in_proj(xa, modsel, w_ext, q_norm, kv_norm, wuq_ext, wukv_ext, tab_mla, tab_ret, nct):
    b, t, d = xa.shape
    tm = ROW_TILE
    row = lambda bi, i: (bi, i, 0)
    n_ret = H_RET * RET_DK
    out_shapes = [jax.ShapeDtypeStruct((b, t, H_MLA * LANES), BF16),
                  jax.ShapeDtypeStruct((b, t, H_MLA * LANES), BF16),
                  jax.ShapeDtypeStruct((b, t, H_MLA * LANES), BF16)] + \
                 [jax.ShapeDtypeStruct((b, t, n_ret), BF16)] * 4
    out_specs = [pl.BlockSpec((1, tm, s.shape[-1]), row) for s in out_shapes]
    return pl.pallas_call(
        _ab_in_kernel,
        grid=(b, t // tm),
        in_specs=[pl.BlockSpec((1, tm, d), row),
                  pl.BlockSpec((1, 1, 6, d), lambda bi, i: (bi, (i >= nct).astype(jnp.int32), 0, 0)),
                  _const_spec(w_ext.shape), _const_spec(q_norm.shape), _const_spec(kv_norm.shape),
                  _const_spec(wuq_ext.shape), _const_spec(wukv_ext.shape),
                  pl.BlockSpec((2, tm, LANES), lambda bi, i: (0, i, 0)),
                  pl.BlockSpec((2, tm, LANES), lambda bi, i: (0, i, 0))],
        out_specs=out_specs,
        out_shape=out_shapes,
        compiler_params=_cparams(("parallel", "arbitrary")),
        name="ab_in_proj",
    )(xa, modsel, w_ext, q_norm, kv_norm, wuq_ext, wukv_ext, tab_mla, tab_ret)


def _attn_kernel(q_ref, k_ref, v_ref, o_ref, s_scr, m_scr, acc_scr, *, tk, ctx_len, n_lat_chunks, nct):
    i = pl.program_id(2)
    tq = q_ref.shape[1]
    m_scr[...] = jnp.full(m_scr.shape, NEG, F32)
    acc_scr[...] = jnp.zeros_like(acc_scr)

    def scores(slot, r0, size):
        for hh in range(2):
            sl = slice(hh * LANES, (hh + 1) * LANES)
            s_scr[slot, hh, :, 0:size] = lax.dot_general(
                q_ref[0, :, sl], k_ref[0, pl.ds(r0, size), sl], (((1,), (1,)), ((), ())),
                preferred_element_type=F32)

    def accumulate(slot, r0, size):
        for hh in range(2):
            sl = slice(hh * LANES, (hh + 1) * LANES)
            s = s_scr[slot, hh, :, 0:size]
            m = m_scr[hh]
            m_new = jnp.maximum(m, jnp.max(s, axis=-1, keepdims=True))
            m_scr[hh] = m_new
            p = jnp.exp2(s - m_new[:, 0:1]).astype(BF16)
            acc_scr[hh] = jnp.exp2(m - m_new) * acc_scr[hh] \
                + jnp.dot(p, v_ref[0, pl.ds(r0, size), sl], preferred_element_type=F32)

    def lat_row(c):
        return pl.multiple_of(ctx_len + jnp.minimum(c, n_lat_chunks - 1) * tk, tk)

    scores(0, 0, ctx_len)
    accumulate(0, 0, ctx_len)
    n = jnp.where(i < nct, 0, n_lat_chunks // 2)
    scores(0, lat_row(0), tk)

    def body(j, carry):
        scores(1, lat_row(2 * j + 1), tk)
        accumulate(0, lat_row(2 * j), tk)
        scores(0, lat_row(2 * j + 2), tk)
        accumulate(1, lat_row(2 * j + 1), tk)
        return carry

    lax.fori_loop(0, n, body, 0)
    outs = [acc_scr[hh] / pltpu.roll(acc_scr[hh], MLA_V, axis=1) for hh in range(2)]
    lane = lax.broadcasted_iota(jnp.int32, (tq, LANES), 1)
    o_ref[0] = jnp.where(lane < MLA_V, outs[0], outs[1]).astype(BF16)


def _attention(q, k, v, ctx_len):
    b, t, _ = q.shape
    tq = ROW_TILE
    tk = ATT_KEY_CHUNK
    assert (t - ctx_len) % (2 * tk) == 0
    kern = functools.partial(_attn_kernel, tk=tk, ctx_len=ctx_len, n_lat_chunks=(t - ctx_len) // tk,
                             nct=ctx_len // tq)
    pair = 2 * LANES
    return pl.pallas_call(
        kern,
        grid=(b, H_MLA // 2, t // tq),
        in_specs=[pl.BlockSpec((1, tq, pair), lambda bi, hp, i: (bi, i, hp)),
                  pl.BlockSpec((1, t, pair), lambda bi, hp, i: (bi, 0, hp)),
                  pl.BlockSpec((1, t, pair), lambda bi, hp, i: (bi, 0, hp))],
        out_specs=pl.BlockSpec((1, tq, LANES), lambda bi, hp, i: (bi, i, hp)),
        out_shape=jax.ShapeDtypeStruct((b, t, H_MLA * MLA_V), BF16),
        scratch_shapes=[pltpu.VMEM((2, 2, tq, max(tk, ctx_len)), F32), pltpu.VMEM((2, tq, LANES), F32),
                        pltpu.VMEM((2, tq, LANES), F32)],
        compiler_params=_cparams(("parallel", "parallel", "arbitrary")),
        name="mla_attention",
    )(q, k, v)


def _ret_kernel(ld_ref, qf_ref, ktf_ref, vf_ref, qb_ref, ktb_ref, vb_ref, of_ref, ob_ref,
                state, dmat, dq, wk, gl):
    c = pl.program_id(1)
    L = SCAN_BLOCK

    @pl.when(c == 0)
    def _():
        state[...] = jnp.zeros_like(state)
        row = lax.broadcasted_iota(jnp.int32, (L, L), 0).astype(F32)
        col = lax.broadcasted_iota(jnp.int32, (L, L), 1).astype(F32)
        for d in range(2):
            for hd in range(H_RET):
                lg = ld_ref[d, hd]
                if d == 0:
                    rel = row - col
                    dq[d, hd] = jnp.exp(lg * (row + 1.0))
                    wk[d, hd] = jnp.exp(lg * (L - 1.0 - col))
                else:
                    rel = col - row
                    dq[d, hd] = jnp.exp(lg * (L - row))
                    wk[d, hd] = jnp.exp(lg * col)
                dmat[d, hd] = jnp.where(rel >= 0, jnp.exp(lg * jnp.maximum(rel, 0.0)), 0.0)
                gl[d, hd] = jnp.exp(jnp.zeros((L, L), F32) + lg * L)

    streams = ((qf_ref, ktf_ref, vf_ref, of_ref), (qb_ref, ktb_ref, vb_ref, ob_ref))
    for d, (q_ref, kt_ref, v_ref, o_ref) in enumerate(streams):
        for hd in range(H_RET):
            sl = slice(hd * LANES, (hd + 1) * LANES)
            q = q_ref[0, :, sl]
            kt = kt_ref[0, sl, :]
            v = v_ref[0, :, sl]
            r = state[d, hd]
            s = jnp.dot(q, kt, preferred_element_type=F32) * dmat[d, hd]
            qd = (q.astype(F32) * dq[d, hd]).astype(BF16)
            o = jnp.dot(s.astype(BF16), v, preferred_element_type=F32) \
                + jnp.dot(qd, r.astype(BF16), preferred_element_type=F32)
            o_ref[0, :, sl] = o
            kw = (kt.astype(F32) * wk[d, hd]).astype(BF16)
            state[d, hd] = r * gl[d, hd] + jnp.dot(kw, v, preferred_element_type=F32)


def _rev_block(j, n_ctx_blocks, n_blocks):
    return jnp.where(j < n_ctx_blocks, n_ctx_blocks - 1 - j, n_blocks - 1 + n_ctx_blocks - j)


def _retention(rq, rkt, rv, log_decay, ctx_len):
    b, t, n = rq.shape
    L = SCAN_BLOCK
    nb, ncb = t // L, ctx_len // L
    fwd = lambda bi, c: (bi, c, 0)
    bwd = lambda bi, c: (bi, _rev_block(c, ncb, nb), 0)
    fwd_t = lambda bi, c: (bi, 0, c)
    bwd_t = lambda bi, c: (bi, 0, _rev_block(c, ncb, nb))
    tile = pl.BlockSpec((1, L, n), fwd)
    tile_b = pl.BlockSpec((1, L, n), bwd)
    out = jax.ShapeDtypeStruct((b, t, n), F32)
    sq = pltpu.VMEM((2, H_RET, L, L), F32)
    return pl.pallas_call(
        _ret_kernel,
        grid=(b, nb),
        in_specs=[pl.BlockSpec(memory_space=pltpu.SMEM),
                  tile, pl.BlockSpec((1, n, L), fwd_t), tile,
                  tile_b, pl.BlockSpec((1, n, L), bwd_t), tile_b],
        out_specs=[tile, tile_b],
        out_shape=[out, out],
        scratch_shapes=[sq, sq, sq, sq, sq],
        compiler_params=_cparams(("parallel", "arbitrary")),
        name="retention_scan",
    )(log_decay, rq, rkt, rv, rq, rkt, rv)


def _post_body(x, mod_ref, mix_parts, wo_ref, ln_ref, w1_ref, w2_ref, o_ref, alpha):
    y = None
    off = 0
    for part in mix_parts:
        n = part.shape[-1]
        contrib = jnp.dot(part, wo_ref[off:off + n, :], preferred_element_type=F32)
        y = contrib if y is None else y + contrib
        off += n
    g1 = mod_ref[0, 0, 2:3, :]
    sh2 = mod_ref[0, 0, 3:4, :]
    sc2 = mod_ref[0, 0, 4:5, :]
    g2 = mod_ref[0, 0, 5:6, :]
    x1 = _layer_norm_rows(alpha * x + g1 * y, ln_ref[0:1, :], ln_ref[1:2, :])
    h2 = (x1 * (1.0 + sc2) + sh2).astype(BF16)
    u = jnp.dot(h2, w1_ref[...], preferred_element_type=F32)
    dff = u.shape[-1] // 2
    a = (_silu(u[:, dff:]) * u[:, :dff]).astype(BF16)
    y2 = jnp.dot(a, w2_ref[...], preferred_element_type=F32)
    o_ref[0] = _layer_norm_rows(alpha * x1 + g2 * y2, ln_ref[2:3, :], ln_ref[3:4, :])


def _ab_post_kernel(x_ref, mod_ref, att_ref, rf_ref, rb_ref, rg_ref, wo_ref, ln_ref, w1_ref, w2_ref, o_ref, *, alpha):
    ret = rf_ref[0] + rb_ref[0]
    rg = rg_ref[0].astype(F32)
    gate = _silu(rg)
    normed = _head_norm_lanes(ret)
    parts = [att_ref[0]]
    for hd, nh in enumerate(normed):
        parts.append((gate[:, hd * LANES:(hd + 1) * LANES] * nh).astype(BF16))
    _post_body(x_ref[0], mod_ref, parts, wo_ref, ln_ref, w1_ref, w2_ref, o_ref, alpha)


def _m_post_kernel(x_ref, mod_ref, hf_ref, hb_ref, og_ref, ng_ref, wo_ref, ln_ref, w1_ref, w2_ref, o_ref, *, alpha):
    hs = hf_ref[0] + hb_ref[0]
    og = jax.nn.sigmoid(og_ref[0].astype(F32))
    normed = _head_norm_lanes(hs)
    parts = []
    for hd, nh in enumerate(normed):
        sl = slice(hd * LANES, (hd + 1) * LANES)
        parts.append((og[:, sl] * (nh * ng_ref[:, sl])).astype(BF16))
    _post_body(x_ref[0], mod_ref, parts, wo_ref, ln_ref, w1_ref, w2_ref, o_ref, alpha)


def _post(kernel_fn, xa, modsel, acts, consts, w_out, ln, w1, w2, nct, row0_tiles, alpha):
    b, t, d = xa.shape
    tm = ROW_TILE
    n_tiles = t // tm - row0_tiles
    row = lambda bi, i: (bi, i + row0_tiles, 0)
    in_specs = [pl.BlockSpec((1, tm, d), row),
                pl.BlockSpec((1, 1, 6, d), lambda bi, i: (bi, (i + row0_tiles >= nct).astype(jnp.int32), 0, 0))]
    in_specs += [pl.BlockSpec((1, tm, a.shape[-1]), row) for a in acts]
    in_specs += [_const_spec(cst.shape) for cst in consts]
    in_specs += [_const_spec(w_out.shape), _const_spec(ln.shape), _const_spec(w1.shape), _const_spec(w2.shape)]
    return pl.pallas_call(
        functools.partial(kernel_fn, alpha=alpha),
        grid=(b, n_tiles),
        in_specs=in_specs,
        out_specs=pl.BlockSpec((1, tm, d), lambda bi, i: (bi, i, 0)),
        out_shape=jax.ShapeDtypeStruct((b, n_tiles * tm, d), F32),
        compiler_params=_cparams(("parallel", "arbitrary")),
        name="post_" + kernel_fn.__name__,
    )(xa, modsel, *acts, *consts, w_out, ln, w1, w2)


_M_QK, _M_V, _M_OG, _M_G, _M_END = 0, 2048, 3072, 4096, 4224


def _m_in_kernel(x_ref, xp_ref, xn_ref, mod_ref, w_ref, cw_ref, cb_ref, gb_ref,
                 q_ref, k_ref, v_ref, og_ref, g_ref, *, seg_starts, seg_ends):
    i = pl.program_id(1)
    tm = x_ref.shape[1]
    shift = mod_ref[0, 0, 0:1, :]
    scale = mod_ref[0, 0, 1:2, :]
    h = (x_ref[0] * (1.0 + scale) + shift).astype(BF16)
    y = jnp.dot(h, w_ref[...], preferred_element_type=F32)
    hp = (xp_ref[0] * (1.0 + scale) + shift).astype(BF16)
    hn = (xn_ref[0] * (1.0 + scale) + shift).astype(BF16)
    wqk = w_ref[:, _M_QK:_M_V]
    up = jnp.dot(hp, wqk, preferred_element_type=F32)[7:8, :]
    un = jnp.dot(hn, wqk, preferred_element_type=F32)[0:1, :]
    is_start = functools.reduce(jnp.logical_or, [i == s for s in seg_starts])
    is_end = functools.reduce(jnp.logical_or, [i == e for e in seg_ends])
    up = jnp.where(is_start, 0.0, up)
    un = jnp.where(is_end, 0.0, un)

    u = y[:, _M_QK:_M_V]
    rows = lax.broadcasted_iota(jnp.int32, u.shape, 0)
    u_prev = jnp.where(rows == 0, up, pltpu.roll(u, 1, axis=0))
    u_next = jnp.where(rows == tm - 1, un, pltpu.roll(u, tm - 1, axis=0))
    qk = _silu(cw_ref[0:1, :] * u_prev + cw_ref[1:2, :] * u + cw_ref[2:3, :] * u_next + cb_ref[...])
    nq = H_M * LANES
    q_ref[0] = qk[:, :nq].astype(BF16)
    k_ref[0] = (qk[:, nq:] * (M_DK ** -0.5)).astype(BF16)
    v_ref[0] = y[:, _M_V:_M_OG].astype(BF16)
    og_ref[0] = y[:, _M_OG:_M_G].astype(BF16)

    g = y[:, _M_G:_M_END] + gb_ref[...]
    lsg = jnp.minimum(g, 0.0) - jnp.log1p(jnp.exp(-jnp.abs(g)))
    r = lax.broadcasted_iota(jnp.int32, (tm, tm), 0)
    cidx = lax.broadcasted_iota(jnp.int32, (tm, tm), 1)
    same = (r // M_SCAN_CHUNK) == (cidx // M_SCAN_CHUNK)
    tril = jnp.where(jnp.logical_and(same, cidx <= r), 1.0, 0.0)
    triu = jnp.where(jnp.logical_and(same, cidx >= r), 1.0, 0.0)
    pre = jnp.dot(tril, lsg, precision=HIGHEST, preferred_element_type=F32)
    suf = jnp.dot(triu, lsg, precision=HIGHEST, preferred_element_type=F32)
    lane = lax.broadcasted_iota(jnp.int32, g.shape, 1)
    grp = lane // H_M
    g_ref[0] = jnp.where(grp == 1, pre, jnp.where(grp == 3, suf, g))


def _m_in_proj(xa, modsel, w_ext, conv_w, conv_b, gate_b, nct):
    b, t, d = xa.shape
    tm = ROW_TILE
    nt = t // tm
    r8 = tm // 8
    row = lambda bi, i: (bi, i, 0)
    prev = lambda bi, i: (bi, jnp.maximum(i * r8 - 1, 0), 0)
    nxt = lambda bi, i: (bi, jnp.minimum((i + 1) * r8, t // 8 - 1), 0)
    widths = (H_M * LANES, H_M * LANES, H_M * M_DV, H_M * M_DV)
    out_shapes = [jax.ShapeDtypeStruct((b, t, w), BF16) for w in widths] + \
                 [jax.ShapeDtypeStruct((b, t, LANES), F32)]
    out_specs = [pl.BlockSpec((1, tm, s.shape[-1]), row) for s in out_shapes]
    kern = functools.partial(_m_in_kernel, seg_starts=(0, nct), seg_ends=(nct - 1, nt - 1))
    return pl.pallas_call(
        kern,
        grid=(b, nt),
        in_specs=[pl.BlockSpec((1, tm, d), row),
                  pl.BlockSpec((1, 8, d), prev),
                  pl.BlockSpec((1, 8, d), nxt),
                  pl.BlockSpec((1, 1, 6, d), lambda bi, i: (bi, (i >= nct).astype(jnp.int32), 0, 0)),
                  _const_spec(w_ext.shape), _const_spec(conv_w.shape), _const_spec(conv_b.shape),
                  _const_spec(gate_b.shape)],
        out_specs=out_specs,
        out_shape=out_shapes,
        compiler_params=_cparams(("parallel", "arbitrary")),
        name="m_in_proj",
    )(xa, xa, xa, modsel, w_ext, conv_w, conv_b, gate_b)


def _mlstm_chunk(q, kt, v, gc, gt, hd, backward, cext, m_prev):
    L = M_SCAN_CHUNK
    base = 2 * H_M if backward else 0
    i_row = gt[base + hd:base + hd + 1, :]
    b_row = gt[base + H_M + hd:base + H_M + hd + 1, :]
    b_col = gc[:, base + H_M + hd:base + H_M + hd + 1]
    t_idx = lax.broadcasted_iota(jnp.int32, (L, L), 0)
    s_idx = lax.broadcasted_iota(jnp.int32, (L, L), 1)
    visible = (s_idx >= t_idx) if backward else (s_idx <= t_idx)
    logd = jnp.where(visible, b_col - b_row + i_row, NEG)
    m_loc = jnp.max(logd, axis=-1, keepdims=True)
    sc = jnp.dot(q, kt, preferred_element_type=F32) * jnp.exp(logd - m_loc)
    ones = jnp.ones((L, LANES), BF16)
    vext = jnp.concatenate([v, ones], axis=-1)
    nd_intra = jnp.dot(sc.astype(BF16), vext, preferred_element_type=F32)
    b_end = b_row[:, 0:1] if backward else b_row[:, L - 1:L]
    lw = b_end - b_row + i_row
    mu = jnp.max(lw, axis=-1, keepdims=True)
    kw = (kt.astype(F32) * jnp.exp(lw - mu)).astype(BF16)
    upd = jnp.dot(kw, vext, preferred_element_type=F32)
    inter = b_col + m_prev
    m_t = jnp.maximum(inter, m_loc)
    nd = jnp.exp(m_loc - m_t) * nd_intra \
        + jnp.exp(inter - m_t) * jnp.dot(q, cext.astype(BF16), preferred_element_type=F32)
    h = nd[:, :LANES] / jnp.maximum(jnp.abs(nd[:, LANES:]), jnp.exp(-m_t))
    m_new = jnp.maximum(b_end + m_prev, mu)
    cext_new = jnp.exp(b_end + m_prev - m_new) * cext + jnp.exp(mu - m_new) * upd
    return h, cext_new, m_new


def _mlstm_kernel(qf_ref, ktf_ref, vf_ref, gf_ref, gtf_ref, qb_ref, ktb_ref, vb_ref, gb_ref, gtb_ref,
                  hf_ref, hb_ref, cstate, mstate):
    c = pl.program_id(1)

    @pl.when(c == 0)
    def _():
        cstate[...] = jnp.zeros_like(cstate)
        mstate[...] = jnp.zeros_like(mstate)

    streams = ((qf_ref, ktf_ref, vf_ref, gf_ref, gtf_ref, hf_ref),
               (qb_ref, ktb_ref, vb_ref, gb_ref, gtb_ref, hb_ref))
    for d, (q_ref, kt_ref, v_ref, g_ref, gt_ref, h_ref) in enumerate(streams):
        for hd in range(H_M):
            sl = slice(hd * LANES, (hd + 1) * LANES)
            ch = d * H_M + hd
            h, cext, m_new = _mlstm_chunk(q_ref[0, :, sl], kt_ref[0, sl, :], v_ref[0, :, sl],
                                          g_ref[0], gt_ref[0], hd, d == 1, cstate[ch], mstate[ch][0:1, 0:1])
            h_ref[0, :, sl] = h
            cstate[ch] = cext
            mstate[ch] = jnp.broadcast_to(m_new, mstate.shape[1:])


def _mlstm(q, kt, v, g, gt, ctx_len):
    b, t, n = v.shape
    L = SCAN_BLOCK
    nb, ncb = t // L, ctx_len // L
    fwd = lambda bi, c: (bi, c, 0)
    bwd = lambda bi, c: (bi, _rev_block(c, ncb, nb), 0)
    fwd_t = lambda bi, c: (bi, 0, c)
    bwd_t = lambda bi, c: (bi, 0, _rev_block(c, ncb, nb))

    def specs(rmap, tmap):
        return [pl.BlockSpec((1, L, q.shape[-1]), rmap), pl.BlockSpec((1, kt.shape[1], L), tmap),
                pl.BlockSpec((1, L, n), rmap), pl.BlockSpec((1, L, LANES), rmap),
                pl.BlockSpec((1, gt.shape[1], L), tmap)]

    out = jax.ShapeDtypeStruct((b, t, n), F32)
    return pl.pallas_call(
        _mlstm_kernel,
        grid=(b, nb),
        in_specs=specs(fwd, fwd_t) + specs(bwd, bwd_t),
        out_specs=[pl.BlockSpec((1, L, n), fwd), pl.BlockSpec((1, L, n), bwd)],
        out_shape=[out, out],
        scratch_shapes=[pltpu.VMEM((2 * H_M, LANES, 2 * LANES), F32), pltpu.VMEM((2 * H_M, 8, LANES), F32)],
        compiler_params=_cparams(("parallel", "arbitrary")),
        name="mlstm_scan",
    )(q, kt, v, g, gt, q, kt, v, g, gt)


def _pair_swap_index(d):
    q = d // 4
    j = np.arange(d)
    return np.where((j // q) % 2 == 0, j + q, j - q)


def _rope_tables(n_lat, ctx_len, d, lane0):
    rows = n_lat // GRID_W
    row = jnp.repeat(jnp.arange(rows, dtype=F32), GRID_W)
    col = jnp.tile(jnp.arange(GRID_W, dtype=F32), rows)
    da = d // 2
    inv = ROPE_BASE ** (-jnp.arange(0, da, 2, dtype=F32) / da)
    ar = row[:, None] * inv
    ac = col[:, None] * inv
    cos = jnp.concatenate([jnp.cos(ar), jnp.cos(ar), jnp.cos(ac), jnp.cos(ac)], axis=-1)
    sin = jnp.concatenate([-jnp.sin(ar), jnp.sin(ar), -jnp.sin(ac), jnp.sin(ac)], axis=-1)
    cos_full = jnp.ones((ctx_len + n_lat, LANES), F32).at[ctx_len:, lane0:lane0 + d].set(cos)
    sin_full = jnp.zeros((ctx_len + n_lat, LANES), F32).at[ctx_len:, lane0:lane0 + d].set(sin)
    return jnp.stack([cos_full, sin_full])


def _pad_heads(w, n_heads, width):
    k = w.shape[0]
    w = w.reshape(k, n_heads, width)
    return jnp.pad(w, ((0, 0), (0, 0), (0, LANES - width))).reshape(k, n_heads * LANES)


def _prep_ab_weights(w_in, w_uq, w_ukv):
    o1 = MLA_Q_LORA
    o2 = o1 + MLA_KV_LORA
    o3 = o2 + MLA_ROPE
    o4 = o3 + H_RET * RET_DK
    o5 = o4 + H_RET * RET_DK
    o6 = o5 + H_RET * RET_DV
    d = w_in.shape[0]
    cq, ckv, kr, rq, rk, rv, rg = (w_in[:, :o1], w_in[:, o1:o2], w_in[:, o2:o3], w_in[:, o3:o4],
                                   w_in[:, o4:o5], w_in[:, o5:o6], w_in[:, o6:])
    ret_swap = np.concatenate([hd * RET_DK + _pair_swap_index(RET_DK) for hd in range(H_RET)])
    kr_blk = jnp.zeros((d, LANES), F32).at[:, MLA_NOPE:MLA_NOPE + MLA_ROPE].set(kr)
    krr_blk = jnp.zeros((d, LANES), F32).at[:, MLA_NOPE:MLA_NOPE + MLA_ROPE].set(kr[:, _pair_swap_index(MLA_ROPE)])
    w_ext = jnp.concatenate([cq, ckv, rq, rk, rv, rg, rq[:, ret_swap], rk[:, ret_swap], kr_blk, krr_blk], axis=1)

    dqh = MLA_NOPE + MLA_ROPE
    uq = w_uq.reshape(-1, H_MLA, dqh)
    uq_rot = jnp.concatenate([jnp.zeros_like(uq[..., :MLA_NOPE]),
                              uq[..., MLA_NOPE:][..., _pair_swap_index(MLA_ROPE)]], axis=-1)
    wuq_ext = jnp.concatenate([_pad_heads(uq.reshape(-1, H_MLA * dqh), H_MLA, dqh),
                               _pad_heads(uq_rot.reshape(-1, H_MLA * dqh), H_MLA, dqh)], axis=1)
    ukv = w_ukv.reshape(-1, H_MLA, MLA_NOPE + MLA_V)
    k_nope = _pad_heads(ukv[..., :MLA_NOPE].reshape(-1, H_MLA * MLA_NOPE), H_MLA, MLA_NOPE)
    v_w = ukv[..., MLA_NOPE:]
    zeros = jnp.zeros_like(v_w)
    even = (np.arange(H_MLA) % 2 == 0)[None, :, None]
    v_cols = jnp.concatenate([jnp.where(even, v_w, zeros), jnp.where(even, zeros, v_w)], axis=-1)
    wukv_ext = jnp.concatenate([k_nope, v_cols.reshape(-1, H_MLA * LANES)], axis=1)
    return w_ext.astype(BF16), wuq_ext.astype(BF16), wukv_ext.astype(BF16)


def _prep_m_weights(w_in, conv_w, conv_b, gate_b):
    o1 = 2 * H_M * M_DK
    o2 = o1 + H_M * M_DV
    o3 = o2 + H_M * M_DV
    qk_pad = _pad_heads(w_in[:, :o1], 2 * H_M, M_DK)
    g_pad = jnp.pad(w_in[:, o3:], ((0, 0), (0, LANES - 4 * H_M)))
    w_ext = jnp.concatenate([qk_pad, w_in[:, o1:o2], w_in[:, o2:o3], g_pad], axis=1).astype(BF16)
    cw = _pad_heads(conv_w, 2 * H_M, M_DK)
    cb = _pad_heads(conv_b[None, :], 2 * H_M, M_DK)
    gb = jnp.pad(gate_b.reshape(1, 4 * H_M), ((0, 0), (0, LANES - 4 * H_M)))
    return w_ext, cw, cb, gb


def kernel(x, c, ctx, c_ctx, mod_w, mod_b, ln_g, ln_b, ffn_w_in, ffn_w_out, ab_w_in, mla_q_norm, mla_w_uq,
           mla_kv_norm, mla_w_ukv, ret_log_decay, ab_w_out, m_w_in, m_conv_w, m_conv_b, m_gate_b, m_norm_g, m_w_out):
    b, s, d = x.shape
    ctx_len = ctx.shape[1]
    depth = mod_w.shape[0]
    assert ctx_len % ROW_TILE == 0 and s % ROW_TILE == 0 and s % GRID_W == 0 and b + 1 <= 8
    nct = ctx_len // ROW_TILE
    alpha = (2 * depth) ** 0.25

    c_rows = jnp.zeros((8, d), F32).at[:b].set(c).at[b].set(c_ctx)
    mods = _modulation(c_rows, mod_w, mod_b).reshape(depth, 8, 6, d)
    tab_mla = _rope_tables(s, ctx_len, MLA_ROPE, MLA_NOPE)
    tab_ret = _rope_tables(s, ctx_len, RET_DK, 0)

    xa = jnp.concatenate([ctx, x], axis=1)
    for l in range(depth):
        last = l == depth - 1
        j = l // 2
        modsel = jnp.stack([jnp.broadcast_to(mods[l, b], (b, 6, d)), mods[l, :b]], axis=1)
        ln = jnp.stack([ln_g[l, 0], ln_b[l, 0], ln_g[l, 1], ln_b[l, 1]])
        w1 = ffn_w_in[l].astype(BF16)
        w2 = ffn_w_out[l].astype(BF16)
        row0 = nct if last else 0
        if l % 2 == 0:
            w_ext, wuq_ext, wukv_ext = _prep_ab_weights(ab_w_in[j], mla_w_uq[j], mla_w_ukv[j])
            q, k, v, rq, rk, rv, rg = _ab_in_proj(xa, modsel, w_ext, mla_q_norm[j][None, :], mla_kv_norm[j][None, :],
                                                  wuq_ext, wukv_ext, tab_mla, tab_ret, nct)
            att = _attention(q, k, v, ctx_len)
            ret_f, ret_b = _retention(rq, jnp.swapaxes(rk, 1, 2), rv, ret_log_decay[j], ctx_len)
            xa_new = _post(_ab_post_kernel, xa, modsel, [att, ret_f, ret_b, rg], [],
                           ab_w_out[j].astype(BF16), ln, w1, w2, nct, row0, alpha)
        else:
            w_ext, cw, cb, gb = _prep_m_weights(m_w_in[j], m_conv_w[j], m_conv_b[j], m_gate_b[j])
            q, k, v, og, g = _m_in_proj(xa, modsel, w_ext, cw, cb, gb, nct)
            gt = jnp.swapaxes(g[:, :, :4 * H_M], 1, 2)
            hf, hb = _mlstm(q, jnp.swapaxes(k, 1, 2), v, g, gt, ctx_len)
            xa_new = _post(_m_post_kernel, xa, modsel, [hf, hb, og], [m_norm_g[j][None, :]],
                           m_w_out[j].astype(BF16), ln, w1, w2, nct, row0, alpha)
        xa = xa_new if last else xa_new
        if last:
            return xa
    return xa[:, ctx_len:]
```

```python
import functools

import jax
import jax.numpy as jnp
import numpy as np
from jax import lax
from jax.experimental import pallas as pl
from jax.experimental.pallas import tpu as pltpu

F32 = jnp.float32
BF16 = jnp.bfloat16
HIGHEST = lax.Precision.HIGHEST

GRID_W = 64
ROPE_BASE = 10000.0
EPS = 1e-5
NEG = -1e30
LOG2E = 1.4426950408889634
H_MLA, MLA_NOPE, MLA_ROPE, MLA_V = 8, 64, 32, 64
MLA_Q_LORA, MLA_KV_LORA = 384, 256
H_RET, RET_DK, RET_DV = 4, 128, 128
H_M, M_DK, M_DV = 8, 64, 128

LANES = 128
ROW_TILE = 256
SCAN_BLOCK = 128
ATT_KEY_CHUNK = 512
ATT_Q_TILE = 512
M_SCAN_CHUNK = SCAN_BLOCK
M_ONES_ROWS = 16
VMEM_LIMIT = 56 * 1024 * 1024


def _cparams(sem, flags=None):
    return pltpu.CompilerParams(dimension_semantics=sem, vmem_limit_bytes=VMEM_LIMIT, flags=flags)


def _const_spec(shape):
    nd = len(shape)
    return pl.BlockSpec(shape, lambda *_: (0,) * nd, pipeline_mode=pl.Buffered(1))


def _split_rows_specs(parts, nlt, tm):
    lat, ctx, ctx_tile0 = parts
    n = lat.shape[-1]
    specs = [pl.BlockSpec((1, tm, n), lambda bi, i: (bi, jnp.minimum(i, nlt - 1), 0)),
             pl.BlockSpec((1, tm, n), lambda bi, i: (bi, ctx_tile0 + jnp.maximum(i - nlt, 0), 0))]
    return specs, [lat, ctx]


def _select_rows(nlt, lat_ref, ctx_ref):
    return jnp.where(pl.program_id(1) >= nlt, ctx_ref[0], lat_ref[0])


def _silu(v):
    return v * jax.nn.sigmoid(v)


def _layer_norm_rows(v, g, b):
    mu = jnp.mean(v, axis=-1, keepdims=True)
    d = v - mu
    var = jnp.mean(d * d, axis=-1, keepdims=True)
    return d * lax.rsqrt(var + EPS) * g + b


def _head_norm_lanes(v):
    outs = []
    for h in range(v.shape[-1] // LANES):
        blk = v[:, h * LANES:(h + 1) * LANES]
        mu = jnp.mean(blk, axis=-1, keepdims=True)
        d = blk - mu
        var = jnp.mean(d * d, axis=-1, keepdims=True)
        outs.append(d * lax.rsqrt(var + EPS))
    return outs


def _mod_kernel(c_ref, w_ref, b_ref, o_ref):
    sc = _silu(c_ref[...])
    o_ref[0] = jnp.dot(sc, w_ref[0], precision=HIGHEST, preferred_element_type=F32) + b_ref[0]


def _modulation(c_rows, mod_w, mod_b):
    depth, d, n = mod_w.shape
    tn = 1536
    return pl.pallas_call(
        _mod_kernel,
        grid=(depth, n // tn),
        in_specs=[pl.BlockSpec((8, d), lambda l, j: (0, 0)),
                  pl.BlockSpec((1, d, tn), lambda l, j: (l, 0, j)),
                  pl.BlockSpec((1, 1, tn), lambda l, j: (l, 0, j))],
        out_specs=pl.BlockSpec((1, 8, tn), lambda l, j: (l, 0, j)),
        out_shape=jax.ShapeDtypeStruct((depth, 8, n), F32),
        compiler_params=_cparams(("arbitrary", "arbitrary")),
        name="modulation",
    )(c_rows, mod_w, mod_b.reshape(depth, 1, n))


_A_CQ, _A_CKV, _A_RQ, _A_RK, _A_RV, _A_RG, _A_RQR, _A_RKR, _A_KR, _A_KRR, _A_END = (
    0, 384, 640, 1152, 1664, 2176, 2688, 3200, 3712, 3840, 3968)


def _ab_in_kernel(x_ref, xc_ref, mod_ref, w_ref, qn_ref, kvn_ref, wuq_ref, wukv_ref, tm_ref, tr_ref,
                  q_ref, kt_ref, v_ref, rq_ref, rkt_ref, rv_ref, rg_ref, *, nlt):
    x = _select_rows(nlt, x_ref, xc_ref)
    shift = mod_ref[0, 0, 0:1, :]
    scale = mod_ref[0, 0, 1:2, :]
    h = (x * (1.0 + scale) + shift).astype(BF16)
    y = jnp.dot(h, w_ref[...], preferred_element_type=F32)

    cq = y[:, _A_CQ:_A_CKV]
    ckv = y[:, _A_CKV:_A_RQ]
    ncq = (cq * lax.rsqrt(jnp.mean(cq * cq, axis=-1, keepdims=True) + EPS) * qn_ref[...]).astype(BF16)
    nckv = (ckv * lax.rsqrt(jnp.mean(ckv * ckv, axis=-1, keepdims=True) + EPS) * kvn_ref[...]).astype(BF16)
    q2 = jnp.dot(ncq, wuq_ref[...], preferred_element_type=F32)
    kv = jnp.dot(nckv, wukv_ref[...], preferred_element_type=F32)

    cos_m = tm_ref[0]
    sin_m = tm_ref[1]
    q_scale = (MLA_NOPE + MLA_ROPE) ** -0.5 * LOG2E
    k_rope = y[:, _A_KR:_A_KRR] * cos_m + y[:, _A_KRR:_A_END] * sin_m
    nq = H_MLA * LANES
    for hd in range(H_MLA):
        sl = slice(hd * LANES, (hd + 1) * LANES)
        qh = q2[:, sl] * cos_m + q2[:, nq + hd * LANES:nq + (hd + 1) * LANES] * sin_m
        q_ref[0, :, sl] = (qh * q_scale).astype(BF16)
        kt_ref[0, sl, :] = (kv[:, sl] + k_rope).T.astype(BF16)
    lane = lax.broadcasted_iota(jnp.int32, (1, nq), 1)
    ones_half = jnp.where(((lane // MLA_V) % 2) != ((lane // LANES) % 2), 1.0, 0.0)
    v_ref[0] = (kv[:, nq:] + ones_half).astype(BF16)

    cos_r = tr_ref[0]
    sin_r = tr_ref[1]
    k_scale = RET_DK ** -0.5
    for hd in range(H_RET):
        sl = slice(hd * LANES, (hd + 1) * LANES)
        rq = y[:, _A_RQ + hd * LANES:_A_RQ + (hd + 1) * LANES] * cos_r \
            + y[:, _A_RQR + hd * LANES:_A_RQR + (hd + 1) * LANES] * sin_r
        rk = y[:, _A_RK + hd * LANES:_A_RK + (hd + 1) * LANES] * cos_r \
            + y[:, _A_RKR + hd * LANES:_A_RKR + (hd + 1) * LANES] * sin_r
        rq_ref[0, :, sl] = rq.astype(BF16)
        rkt_ref[0, sl, :] = (rk * k_scale).T.astype(BF16)
    rv_ref[0] = y[:, _A_RV:_A_RG].astype(BF16)
    rg_ref[0] = y[:, _A_RG:_A_RQR].astype(BF16)


def _ab_in_proj(x_parts, modsel, w_ext, q_norm, kv_norm, wuq_ext, wukv_ext, tab_mla, tab_ret, nlt, t):
    b, d = modsel.shape[0], modsel.shape[-1]
    tm = ROW_TILE
    row = lambda bi, i: (bi, i, 0)
    col = lambda bi, i: (bi, 0, i)
    n_ret = H_RET * RET_DK
    widths = (H_MLA * LANES, H_MLA * LANES, H_MLA * LANES, n_ret, n_ret, n_ret, n_ret)
    transposed = (False, True, False, False, True, False, False)
    out_shapes = [jax.ShapeDtypeStruct((b, w, t) if tr else (b, t, w), BF16) for w, tr in zip(widths, transposed)]
    out_specs = [pl.BlockSpec((1, w, tm), col) if tr else pl.BlockSpec((1, tm, w), row)
                 for w, tr in zip(widths, transposed)]
    x_specs, x_args = _split_rows_specs(x_parts, nlt, tm)
    return pl.pallas_call(
        functools.partial(_ab_in_kernel, nlt=nlt),
        grid=(b, t // tm),
        in_specs=x_specs + [
                  pl.BlockSpec((1, 1, 6, d), lambda bi, i: (bi, (i >= nlt).astype(jnp.int32), 0, 0)),
                  _const_spec(w_ext.shape), _const_spec(q_norm.shape), _const_spec(kv_norm.shape),
                  _const_spec(wuq_ext.shape), _const_spec(wukv_ext.shape),
                  pl.BlockSpec((2, tm, LANES), lambda bi, i: (0, i, 0)),
                  pl.BlockSpec((2, tm, LANES), lambda bi, i: (0, i, 0))],
        out_specs=out_specs,
        out_shape=out_shapes,
        compiler_params=_cparams(("parallel", "arbitrary")),
        name="ab_in_proj",
    )(*x_args, modsel, w_ext, q_norm, kv_norm, wuq_ext, wukv_ext, tab_mla, tab_ret)


def _attn_kernel(q_ref, kt_ref, v_ref, o_ref, s_scr, m_scr, acc_scr, *, tk, n_lat, ctx_len, n_lat_chunks):
    tq = q_ref.shape[1]
    m_scr[...] = jnp.full(m_scr.shape, NEG, F32)
    acc_scr[...] = jnp.zeros_like(acc_scr)

    def scores(slot, r0, size):
        for hh in range(2):
            sl = slice(hh * LANES, (hh + 1) * LANES)
            s_scr[slot, hh, :, 0:size] = jnp.dot(q_ref[0, :, sl], kt_ref[0, sl, pl.ds(r0, size)],
                                                 preferred_element_type=F32)

    def accumulate(slot, r0, size):
        for hh in range(2):
            sl = slice(hh * LANES, (hh + 1) * LANES)
            s = s_scr[slot, hh, :, 0:size]
            m = m_scr[hh]
            m_new = jnp.maximum(m, jnp.max(s, axis=-1, keepdims=True))
            m_scr[hh] = m_new
            p = jnp.exp2(s - m_new[:, 0:1]).astype(BF16)
            acc_scr[hh] = jnp.exp2(m - m_new) * acc_scr[hh] \
                + jnp.dot(p, v_ref[0, pl.ds(r0, size), sl], preferred_element_type=F32)

    def lat_row(c):
        return pl.multiple_of(jnp.minimum(c, n_lat_chunks - 1) * tk, tk)

    scores(0, n_lat, ctx_len)
    accumulate(0, n_lat, ctx_len)
    if n_lat_chunks:
        scores(0, 0, tk)

        def body(j, carry):
            scores(1, lat_row(2 * j + 1), tk)
            accumulate(0, lat_row(2 * j), tk)
            scores(0, lat_row(2 * j + 2), tk)
            accumulate(1, lat_row(2 * j + 1), tk)
            return carry

        lax.fori_loop(0, n_lat_chunks // 2, body, 0)
    outs = [acc_scr[hh] / pltpu.roll(acc_scr[hh], MLA_V, axis=1) for hh in range(2)]
    lane = lax.broadcasted_iota(jnp.int32, (tq, LANES), 1)
    o_ref[0] = jnp.where(lane < MLA_V, outs[0], outs[1]).astype(BF16)


def _attention(q, kt, v, n_lat, for_ctx):
    b, t, _ = q.shape
    ctx_len = t - n_lat
    tk = ATT_KEY_CHUNK
    assert n_lat % (2 * tk) == 0 and n_lat % ATT_Q_TILE == 0 and ctx_len % ROW_TILE == 0
    tq, n_rows, row0 = (ROW_TILE, ctx_len, n_lat // ROW_TILE) if for_ctx else (ATT_Q_TILE, n_lat, 0)
    kern = functools.partial(_attn_kernel, tk=tk, n_lat=n_lat, ctx_len=ctx_len,
                             n_lat_chunks=0 if for_ctx else n_lat // tk)
    pair = 2 * LANES
    return pl.pallas_call(
        kern,
        grid=(b, H_MLA // 2, n_rows // tq),
        in_specs=[pl.BlockSpec((1, tq, pair), lambda bi, hp, i: (bi, i + row0, hp)),
                  pl.BlockSpec((1, pair, t), lambda bi, hp, i: (bi, hp, 0)),
                  pl.BlockSpec((1, t, pair), lambda bi, hp, i: (bi, 0, hp))],
        out_specs=pl.BlockSpec((1, tq, LANES), lambda bi, hp, i: (bi, i, hp)),
        out_shape=jax.ShapeDtypeStruct((b, n_rows, H_MLA * MLA_V), BF16),
        scratch_shapes=[pltpu.VMEM((2, 2, tq, max(tk, ctx_len)), F32), pltpu.VMEM((2, tq, LANES), F32),
                        pltpu.VMEM((2, tq, LANES), F32)],
        compiler_params=_cparams(("parallel", "parallel", "arbitrary")),
        name="mla_attention_ctx" if for_ctx else "mla_attention",
    )(q, kt, v)


def _ret_kernel(ld_ref, qf_ref, ktf_ref, vf_ref, qb_ref, ktb_ref, vb_ref, of_ref, ob_ref,
                state, dmat, dq, wk, gl):
    c = pl.program_id(1)
    L = SCAN_BLOCK

    @pl.when(c == 0)
    def _():
        state[...] = jnp.zeros_like(state)
        row = lax.broadcasted_iota(jnp.int32, (L, L), 0).astype(F32)
        col = lax.broadcasted_iota(jnp.int32, (L, L), 1).astype(F32)
        for d in range(2):
            for hd in range(H_RET):
                lg = ld_ref[d, hd]
                if d == 0:
                    rel = row - col
                    dq[d, hd] = jnp.exp(lg * (row + 1.0))
                    wk[d, hd] = jnp.exp(lg * (L - 1.0 - col))
                else:
                    rel = col - row
                    dq[d, hd] = jnp.exp(lg * (L - row))
                    wk[d, hd] = jnp.exp(lg * col)
                dmat[d, hd] = jnp.where(rel >= 0, jnp.exp(lg * jnp.maximum(rel, 0.0)), 0.0)
                gl[d, hd] = jnp.exp(jnp.zeros((L, L), F32) + lg * L)

    streams = ((qf_ref, ktf_ref, vf_ref, of_ref), (qb_ref, ktb_ref, vb_ref, ob_ref))
    for d, (q_ref, kt_ref, v_ref, o_ref) in enumerate(streams):
        for hd in range(H_RET):
            sl = slice(hd * LANES, (hd + 1) * LANES)
            q = q_ref[0, :, sl]
            kt = kt_ref[0, sl, :]
            v = v_ref[0, :, sl]
            r = state[d, hd]
            s = jnp.dot(q, kt, preferred_element_type=F32) * dmat[d, hd]
            qd = (q.astype(F32) * dq[d, hd]).astype(BF16)
            o = jnp.dot(s.astype(BF16), v, preferred_element_type=F32) \
                + jnp.dot(qd, r.astype(BF16), preferred_element_type=F32)
            o_ref[0, :, sl] = o
            kw = (kt.astype(F32) * wk[d, hd]).astype(BF16)
            state[d, hd] = r * gl[d, hd] + jnp.dot(kw, v, preferred_element_type=F32)


def _scan_maps(n_lat_blocks, n_ctx_blocks):
    nlb, ncb = n_lat_blocks, n_ctx_blocks
    fwd_blk = lambda c: jnp.where(c < ncb, nlb + c, c - ncb)
    bwd_blk = lambda c: nlb + ncb - 1 - c
    fwd = lambda bi, c: (bi, fwd_blk(c), 0)
    bwd = lambda bi, c: (bi, bwd_blk(c), 0)
    fwd_t = lambda bi, c: (bi, 0, fwd_blk(c))
    bwd_t = lambda bi, c: (bi, 0, bwd_blk(c))
    return fwd, bwd, fwd_t, bwd_t


def _retention(rq, rkt, rv, log_decay, n_lat):
    b, t, n = rq.shape
    L = SCAN_BLOCK
    nb = t // L
    fwd, bwd, fwd_t, bwd_t = _scan_maps(n_lat // L, nb - n_lat // L)
    tile = pl.BlockSpec((1, L, n), fwd)
    tile_b = pl.BlockSpec((1, L, n), bwd)
    out = jax.ShapeDtypeStruct((b, t, n), F32)
    sq = pltpu.VMEM((2, H_RET, L, L), F32)
    return pl.pallas_call(
        _ret_kernel,
        grid=(b, nb),
        in_specs=[pl.BlockSpec(memory_space=pltpu.SMEM),
                  tile, pl.BlockSpec((1, n, L), fwd_t), tile,
                  tile_b, pl.BlockSpec((1, n, L), bwd_t), tile_b],
        out_specs=[tile, tile_b],
        out_shape=[out, out],
        scratch_shapes=[sq, sq, sq, sq, sq],
        compiler_params=_cparams(("parallel", "arbitrary")),
        name="retention_scan",
    )(log_decay, rq, rkt, rv, rq, rkt, rv)


def _post_body(x, mod_ref, mix_parts, wo_ref, ln_ref, w1_ref, w2_ref, o_ref, alpha):
    y = jnp.dot(jnp.concatenate(mix_parts, axis=-1), wo_ref[...], preferred_element_type=F32)
    g1 = mod_ref[0, 0, 2:3, :]
    sh2 = mod_ref[0, 0, 3:4, :]
    sc2 = mod_ref[0, 0, 4:5, :]
    g2 = mod_ref[0, 0, 5:6, :]
    x1 = _layer_norm_rows(alpha * x + g1 * y, ln_ref[0:1, :], ln_ref[1:2, :])
    h2 = (x1 * (1.0 + sc2) + sh2).astype(BF16)
    u = jnp.dot(h2, w1_ref[...], preferred_element_type=F32)
    dff = u.shape[-1] // 2
    a = (_silu(u[:, dff:]) * u[:, :dff]).astype(BF16)
    y2 = jnp.dot(a, w2_ref[...], preferred_element_type=F32)
    o_ref[0] = _layer_norm_rows(alpha * x1 + g2 * y2, ln_ref[2:3, :], ln_ref[3:4, :])


def _ab_post_kernel(x_ref, xc_ref, mod_ref, att_ref, attc_ref, rf_ref, rb_ref, rg_ref,
                    wo_ref, ln_ref, w1_ref, w2_ref, o_ref, *, alpha, nlt):
    ret = rf_ref[0] + rb_ref[0]
    rg = rg_ref[0].astype(F32)
    gate = _silu(rg)
    normed = _head_norm_lanes(ret)
    parts = [_select_rows(nlt, att_ref, attc_ref)]
    for hd, nh in enumerate(normed):
        parts.append((gate[:, hd * LANES:(hd + 1) * LANES] * nh).astype(BF16))
    _post_body(_select_rows(nlt, x_ref, xc_ref), mod_ref, parts, wo_ref, ln_ref, w1_ref, w2_ref, o_ref, alpha)


def _m_post_kernel(x_ref, xc_ref, mod_ref, hf_ref, hb_ref, og_ref, ng_ref,
                   wo_ref, ln_ref, w1_ref, w2_ref, o_ref, *, alpha, nlt):
    hs = hf_ref[0] + hb_ref[0]
    og = jax.nn.sigmoid(og_ref[0].astype(F32))
    normed = _head_norm_lanes(hs)
    parts = []
    for hd, nh in enumerate(normed):
        sl = slice(hd * LANES, (hd + 1) * LANES)
        parts.append((og[:, sl] * (nh * ng_ref[:, sl])).astype(BF16))
    _post_body(_select_rows(nlt, x_ref, xc_ref), mod_ref, parts, wo_ref, ln_ref, w1_ref, w2_ref, o_ref, alpha)


def _post(kernel_fn, x_parts, modsel, acts, consts, w_out, ln, w1, w2, nlt, n_tiles, alpha):
    b, d = modsel.shape[0], modsel.shape[-1]
    tm = ROW_TILE
    row = lambda bi, i: (bi, i, 0)
    in_specs, args = _split_rows_specs(x_parts, nlt, tm)
    in_specs.append(pl.BlockSpec((1, 1, 6, d), lambda bi, i: (bi, (i >= nlt).astype(jnp.int32), 0, 0)))
    args.append(modsel)
    for a in acts:
        if isinstance(a, tuple):
            sp, ar = _split_rows_specs(a, nlt, tm)
            in_specs += sp
            args += ar
        else:
            in_specs.append(pl.BlockSpec((1, tm, a.shape[-1]), row))
            args.append(a)
    weights = [*consts, w_out, ln, w1, w2]
    in_specs += [_const_spec(w.shape) for w in weights]
    return pl.pallas_call(
        functools.partial(kernel_fn, alpha=alpha, nlt=nlt),
        grid=(b, n_tiles),
        in_specs=in_specs,
        out_specs=pl.BlockSpec((1, tm, d), row),
        out_shape=jax.ShapeDtypeStruct((b, n_tiles * tm, d), F32),
        compiler_params=_cparams(("parallel", "arbitrary")),
        name="post_" + kernel_fn.__name__,
    )(*args, *weights)


_M_QK, _M_V, _M_OG, _M_G, _M_END = 0, 2048, 3072, 4096, 4224


def _m_in_kernel(x_ref, xp_ref, xn_ref, mod_ref, w_ref, cw_ref, cb_ref, gb_ref,
                 qt_ref, k_ref, vt_ref, og_ref, g1_ref, g2_ref, gc_ref, *, seg_starts, seg_ends):
    i = pl.program_id(1)
    tm = x_ref.shape[1]
    shift = mod_ref[0, 0, 0:1, :]
    scale = mod_ref[0, 0, 1:2, :]
    h = (x_ref[0] * (1.0 + scale) + shift).astype(BF16)
    y = jnp.dot(h, w_ref[...], preferred_element_type=F32)
    hp = (xp_ref[0] * (1.0 + scale) + shift).astype(BF16)
    hn = (xn_ref[0] * (1.0 + scale) + shift).astype(BF16)
    wqk = w_ref[:, _M_QK:_M_V]
    up = jnp.dot(hp, wqk, preferred_element_type=F32)[7:8, :]
    un = jnp.dot(hn, wqk, preferred_element_type=F32)[0:1, :]
    is_start = functools.reduce(jnp.logical_or, [i == s for s in seg_starts])
    is_end = functools.reduce(jnp.logical_or, [i == e for e in seg_ends])
    up = jnp.where(is_start, 0.0, up)
    un = jnp.where(is_end, 0.0, un)

    u = y[:, _M_QK:_M_V]
    rows = lax.broadcasted_iota(jnp.int32, u.shape, 0)
    u_prev = jnp.where(rows == 0, up, pltpu.roll(u, 1, axis=0))
    u_next = jnp.where(rows == tm - 1, un, pltpu.roll(u, tm - 1, axis=0))
    qk = _silu(cw_ref[0:1, :] * u_prev + cw_ref[1:2, :] * u + cw_ref[2:3, :] * u_next + cb_ref[...])
    nq = H_M * LANES
    for hd in range(H_M):
        sl = slice(hd * LANES, (hd + 1) * LANES)
        qt_ref[0, sl, :] = qk[:, sl].T.astype(BF16)
        vt_ref[0, sl, :] = y[:, _M_V + hd * LANES:_M_V + (hd + 1) * LANES].T.astype(BF16)
    k_ref[0] = (qk[:, nq:] * (M_DK ** -0.5)).astype(BF16)
    og_ref[0] = y[:, _M_OG:_M_G].astype(BF16)

    g = y[:, _M_G:_M_END] + gb_ref[...]
    lsg = jnp.minimum(g, 0.0) - jnp.log1p(jnp.exp(-jnp.abs(g)))
    r = lax.broadcasted_iota(jnp.int32, (tm, tm), 0)
    cidx = lax.broadcasted_iota(jnp.int32, (tm, tm), 1)
    same = (r // M_SCAN_CHUNK) == (cidx // M_SCAN_CHUNK)
    tril = jnp.where(jnp.logical_and(same, cidx <= r), 1.0, 0.0)
    triu = jnp.where(jnp.logical_and(same, cidx >= r), 1.0, 0.0)
    pre = jnp.dot(tril, lsg, precision=HIGHEST, preferred_element_type=F32)
    suf = jnp.dot(triu, lsg, precision=HIGHEST, preferred_element_type=F32)
    lane = lax.broadcasted_iota(jnp.int32, g.shape, 1)
    grp = lane // H_M
    g1_ref[0] = jnp.where(grp == 1, pre, jnp.where(grp == 3, suf, g))
    bsum = jnp.where(grp == 1, pre, jnp.where(grp == 3, suf, 0.0))
    b_at_i = pltpu.roll(bsum, LANES - H_M, axis=1)
    r_gate = g - b_at_i
    pos = lax.broadcasted_iota(jnp.int32, g.shape, 0) % M_SCAN_CHUNK
    pmax = r_gate
    smax = r_gate
    step = 1
    while step < M_SCAN_CHUNK:
        pmax = jnp.where(pos >= step, jnp.maximum(pmax, pltpu.roll(pmax, step, axis=0)), pmax)
        smax = jnp.where(pos < M_SCAN_CHUNK - step, jnp.maximum(smax, pltpu.roll(smax, tm - step, axis=0)), smax)
        step *= 2
    gc_ref[0] = r_gate
    g2_ref[0] = b_at_i + jnp.where(grp == 0, pmax, smax)


def _m_in_proj(xa, modsel, w_ext, conv_w, conv_b, gate_b, nlt):
    b, t, d = xa.shape
    tm = ROW_TILE
    nt = t // tm
    r8 = tm // 8
    row = lambda bi, i: (bi, i, 0)
    prev = lambda bi, i: (bi, jnp.maximum(i * r8 - 1, 0), 0)
    nxt = lambda bi, i: (bi, jnp.minimum((i + 1) * r8, t // 8 - 1), 0)
    col = lambda bi, i: (bi, 0, i)
    n = H_M * LANES
    out_shapes = [jax.ShapeDtypeStruct((b, n, t), BF16), jax.ShapeDtypeStruct((b, t, n), BF16),
                  jax.ShapeDtypeStruct((b, n, t), BF16), jax.ShapeDtypeStruct((b, t, n), BF16)] + \
                 [jax.ShapeDtypeStruct((b, t, LANES), F32)] * 3
    out_specs = [pl.BlockSpec((1, n, tm), col), pl.BlockSpec((1, tm, n), row),
                 pl.BlockSpec((1, n, tm), col), pl.BlockSpec((1, tm, n), row)] + \
                [pl.BlockSpec((1, tm, LANES), row)] * 3
    kern = functools.partial(_m_in_kernel, seg_starts=(0, nlt), seg_ends=(nlt - 1, nt - 1))
    return pl.pallas_call(
        kern,
        grid=(b, nt),
        in_specs=[pl.BlockSpec((1, tm, d), row),
                  pl.BlockSpec((1, 8, d), prev),
                  pl.BlockSpec((1, 8, d), nxt),
                  pl.BlockSpec((1, 1, 6, d), lambda bi, i: (bi, (i >= nlt).astype(jnp.int32), 0, 0)),
                  _const_spec(w_ext.shape), _const_spec(conv_w.shape), _const_spec(conv_b.shape),
                  _const_spec(gate_b.shape)],
        out_specs=out_specs,
        out_shape=out_shapes,
        compiler_params=_cparams(("parallel", "arbitrary")),
        name="m_in_proj",
    )(xa, xa, xa, modsel, w_ext, conv_w, conv_b, gate_b)


def _mlstm_chunk(k, qt, vt, gc, gt1, gt2, hd, backward, cext_t, m_prev):
    L = M_SCAN_CHUNK
    gi = 2 * H_M if backward else 0
    i_row = gt1[gi + hd:gi + hd + 1, :]
    b_row = gt1[gi + H_M + hd:gi + H_M + hd + 1, :]
    mloc_row = gt2[gi + hd:gi + hd + 1, :]
    r_col = gc[:, gi + hd:gi + hd + 1]
    inter = b_row + m_prev
    m_t = jnp.maximum(inter, mloc_row)
    s_idx = lax.broadcasted_iota(jnp.int32, (L, L), 0)
    t_idx = lax.broadcasted_iota(jnp.int32, (L, L), 1)
    visible = (s_idx >= t_idx) if backward else (s_idx <= t_idx)
    dt = jnp.exp(jnp.where(visible, r_col + (b_row - m_t), NEG))
    sc_t = jnp.dot(k, qt, preferred_element_type=F32) * dt
    vext_t = jnp.concatenate([vt, jnp.ones((M_ONES_ROWS, L), BF16)], axis=0)
    nd_t = jnp.dot(vext_t, sc_t.astype(BF16), preferred_element_type=F32) \
        + jnp.dot(cext_t.astype(BF16), qt, preferred_element_type=F32) * jnp.exp(inter - m_t)
    den = jnp.broadcast_to(nd_t[LANES:LANES + 1, :], (LANES, L))
    h_t = nd_t[:LANES] / jnp.maximum(jnp.abs(den), jnp.exp(-m_t))
    e = 0 if backward else L - 1
    b_end = b_row[:, e:e + 1]
    m_new = jnp.maximum(b_end + m_prev, mloc_row[:, e:e + 1])
    w_row = jnp.exp(b_end - b_row + i_row - m_new)
    cext_new = jnp.exp(b_end + m_prev - m_new) * cext_t \
        + jnp.dot((vext_t.astype(F32) * w_row).astype(BF16), k, preferred_element_type=F32)
    return h_t.T, cext_new, m_new


def _mlstm_kernel(*refs):
    ins, (hf_ref, hb_ref, cstate, mstate) = refs[:12], refs[12:]
    c = pl.program_id(1)

    @pl.when(c == 0)
    def _():
        cstate[...] = jnp.zeros_like(cstate)
        mstate[...] = jnp.zeros_like(mstate)

    for d, h_ref in enumerate((hf_ref, hb_ref)):
        k_ref, qt_ref, vt_ref, gc_ref, gt1_ref, gt2_ref = ins[6 * d:6 * d + 6]
        for hd in range(H_M):
            sl = slice(hd * LANES, (hd + 1) * LANES)
            ch = d * H_M + hd
            h, cext, m_new = _mlstm_chunk(k_ref[0, :, sl], qt_ref[0, sl, :], vt_ref[0, sl, :], gc_ref[0],
                                          gt1_ref[0], gt2_ref[0], hd, d == 1, cstate[ch], mstate[ch][0:1, 0:1])
            h_ref[0, :, sl] = h
            cstate[ch] = cext
            mstate[ch] = jnp.broadcast_to(m_new, mstate.shape[1:])


def _mlstm(k, qt, vt, gc, gt1, gt2, n_lat):
    b, t, n = k.shape
    L = SCAN_BLOCK
    nb = t // L
    fwd, bwd, fwd_t, bwd_t = _scan_maps(n_lat // L, nb - n_lat // L)

    def specs(rmap, tmap):
        return [pl.BlockSpec((1, L, n), rmap), pl.BlockSpec((1, n, L), tmap), pl.BlockSpec((1, n, L), tmap),
                pl.BlockSpec((1, L, LANES), rmap), pl.BlockSpec((1, gt1.shape[1], L), tmap),
                pl.BlockSpec((1, gt2.shape[1], L), tmap)]

    out = jax.ShapeDtypeStruct((b, t, H_M * M_DV), F32)
    args = (k, qt, vt, gc, gt1, gt2)
    return pl.pallas_call(
        _mlstm_kernel,
        grid=(b, nb),
        in_specs=specs(fwd, fwd_t) + specs(bwd, bwd_t),
        out_specs=[pl.BlockSpec((1, L, H_M * M_DV), fwd), pl.BlockSpec((1, L, H_M * M_DV), bwd)],
        out_shape=[out, out],
        scratch_shapes=[pltpu.VMEM((2 * H_M, LANES + M_ONES_ROWS, LANES), F32),
                        pltpu.VMEM((2 * H_M, 8, LANES), F32)],
        compiler_params=_cparams(("parallel", "arbitrary")),
        name="mlstm_scan",
    )(*args, *args)


def _pair_swap_index(d):
    q = d // 4
    j = np.arange(d)
    return np.where((j // q) % 2 == 0, j + q, j - q)


def _rope_tables(n_lat, ctx_len, d, lane0):
    rows = n_lat // GRID_W
    row = jnp.repeat(jnp.arange(rows, dtype=F32), GRID_W)
    col = jnp.tile(jnp.arange(GRID_W, dtype=F32), rows)
    da = d // 2
    inv = ROPE_BASE ** (-jnp.arange(0, da, 2, dtype=F32) / da)
    ar = row[:, None] * inv
    ac = col[:, None] * inv
    cos = jnp.concatenate([jnp.cos(ar), jnp.cos(ar), jnp.cos(ac), jnp.cos(ac)], axis=-1)
    sin = jnp.concatenate([-jnp.sin(ar), jnp.sin(ar), -jnp.sin(ac), jnp.sin(ac)], axis=-1)
    cos_full = jnp.ones((n_lat + ctx_len, LANES), F32).at[:n_lat, lane0:lane0 + d].set(cos)
    sin_full = jnp.zeros((n_lat + ctx_len, LANES), F32).at[:n_lat, lane0:lane0 + d].set(sin)
    return jnp.stack([cos_full, sin_full])


def _pad_heads(w, n_heads, width):
    k = w.shape[0]
    w = w.reshape(k, n_heads, width)
    return jnp.pad(w, ((0, 0), (0, 0), (0, LANES - width))).reshape(k, n_heads * LANES)


def _prep_ab_weights(w_in, w_uq, w_ukv):
    o1 = MLA_Q_LORA
    o2 = o1 + MLA_KV_LORA
    o3 = o2 + MLA_ROPE
    o4 = o3 + H_RET * RET_DK
    o5 = o4 + H_RET * RET_DK
    o6 = o5 + H_RET * RET_DV
    d = w_in.shape[0]
    cq, ckv, kr, rq, rk, rv, rg = (w_in[:, :o1], w_in[:, o1:o2], w_in[:, o2:o3], w_in[:, o3:o4],
                                   w_in[:, o4:o5], w_in[:, o5:o6], w_in[:, o6:])
    ret_swap = np.concatenate([hd * RET_DK + _pair_swap_index(RET_DK) for hd in range(H_RET)])
    kr_blk = jnp.zeros((d, LANES), F32).at[:, MLA_NOPE:MLA_NOPE + MLA_ROPE].set(kr)
    krr_blk = jnp.zeros((d, LANES), F32).at[:, MLA_NOPE:MLA_NOPE + MLA_ROPE].set(kr[:, _pair_swap_index(MLA_ROPE)])
    w_ext = jnp.concatenate([cq, ckv, rq, rk, rv, rg, rq[:, ret_swap], rk[:, ret_swap], kr_blk, krr_blk], axis=1)

    dqh = MLA_NOPE + MLA_ROPE
    uq = w_uq.reshape(-1, H_MLA, dqh)
    uq_rot = jnp.concatenate([jnp.zeros_like(uq[..., :MLA_NOPE]),
                              uq[..., MLA_NOPE:][..., _pair_swap_index(MLA_ROPE)]], axis=-1)
    wuq_ext = jnp.concatenate([_pad_heads(uq.reshape(-1, H_MLA * dqh), H_MLA, dqh),
                               _pad_heads(uq_rot.reshape(-1, H_MLA * dqh), H_MLA, dqh)], axis=1)
    ukv = w_ukv.reshape(-1, H_MLA, MLA_NOPE + MLA_V)
    k_nope = _pad_heads(ukv[..., :MLA_NOPE].reshape(-1, H_MLA * MLA_NOPE), H_MLA, MLA_NOPE)
    v_w = ukv[..., MLA_NOPE:]
    zeros = jnp.zeros_like(v_w)
    even = (np.arange(H_MLA) % 2 == 0)[None, :, None]
    v_cols = jnp.concatenate([jnp.where(even, v_w, zeros), jnp.where(even, zeros, v_w)], axis=-1)
    wukv_ext = jnp.concatenate([k_nope, v_cols.reshape(-1, H_MLA * LANES)], axis=1)
    return w_ext.astype(BF16), wuq_ext.astype(BF16), wukv_ext.astype(BF16)


def _prep_m_weights(w_in, conv_w, conv_b, gate_b):
    o1 = 2 * H_M * M_DK
    o2 = o1 + H_M * M_DV
    o3 = o2 + H_M * M_DV
    qk_pad = _pad_heads(w_in[:, :o1], 2 * H_M, M_DK)
    g_pad = jnp.pad(w_in[:, o3:], ((0, 0), (0, LANES - 4 * H_M)))
    w_ext = jnp.concatenate([qk_pad, w_in[:, o1:o2], w_in[:, o2:o3], g_pad], axis=1).astype(BF16)
    cw = _pad_heads(conv_w, 2 * H_M, M_DK)
    cb = _pad_heads(conv_b[None, :], 2 * H_M, M_DK)
    gb = jnp.pad(gate_b.reshape(1, 4 * H_M), ((0, 0), (0, LANES - 4 * H_M)))
    return w_ext, cw, cb, gb


def kernel(x, c, ctx, c_ctx, mod_w, mod_b, ln_g, ln_b, ffn_w_in, ffn_w_out, ab_w_in, mla_q_norm, mla_w_uq,
           mla_kv_norm, mla_w_ukv, ret_log_decay, ab_w_out, m_w_in, m_conv_w, m_conv_b, m_gate_b, m_norm_g, m_w_out):
    b, s, d = x.shape
    ctx_len = ctx.shape[1]
    depth = mod_w.shape[0]
    assert ctx_len % ROW_TILE == 0 and s % ROW_TILE == 0 and s % GRID_W == 0 and b + 1 <= 8
    nlt = s // ROW_TILE
    alpha = (2 * depth) ** 0.25

    c_rows = jnp.zeros((8, d), F32).at[:b].set(c).at[b].set(c_ctx)
    mods = _modulation(c_rows, mod_w, mod_b).reshape(depth, 8, 6, d)
    tab_mla = _rope_tables(s, ctx_len, MLA_ROPE, MLA_NOPE)
    tab_ret = _rope_tables(s, ctx_len, RET_DK, 0)

    t = s + ctx_len
    x_parts = (x, ctx, 0)
    xa = None
    for l in range(depth):
        last = l == depth - 1
        j = l // 2
        modsel = jnp.stack([mods[l, :b], jnp.broadcast_to(mods[l, b], (b, 6, d))], axis=1)
        ln = jnp.stack([ln_g[l, 0], ln_b[l, 0], ln_g[l, 1], ln_b[l, 1]])
        w1 = ffn_w_in[l].astype(BF16)
        w2 = ffn_w_out[l].astype(BF16)
        n_tiles = nlt if last else t // ROW_TILE
        if l % 2 == 0:
            w_ext, wuq_ext, wukv_ext = _prep_ab_weights(ab_w_in[j], mla_w_uq[j], mla_w_ukv[j])
            q, kt, v, rq, rkt, rv, rg = _ab_in_proj(x_parts, modsel, w_ext, mla_q_norm[j][None, :],
                                                    mla_kv_norm[j][None, :], wuq_ext, wukv_ext,
                                                    tab_mla, tab_ret, nlt, t)
            att = (_attention(q, kt, v, s, False), _attention(q, kt, v, s, True), 0)
            ret_f, ret_b = _retention(rq, rkt, rv, ret_log_decay[j], s)
            xa = _post(_ab_post_kernel, x_parts, modsel, [att, ret_f, ret_b, rg], [],
                       ab_w_out[j].astype(BF16), ln, w1, w2, nlt, n_tiles, alpha)
        else:
            if xa is None:
                xa = jnp.concatenate([x, ctx], axis=1)
            w_ext, cw, cb, gb = _prep_m_weights(m_w_in[j], m_conv_w[j], m_conv_b[j], m_gate_b[j])
            qt, k, vt, og, g1, g2, gc = _m_in_proj(xa, modsel, w_ext, cw, cb, gb, nlt)
            gt1 = jnp.swapaxes(g1[:, :, :4 * H_M], 1, 2)
            gt2 = jnp.swapaxes(g2[:, :, :4 * H_M], 1, 2)
            hf, hb = _mlstm(k, qt, vt, gc, gt1, gt2, s)
            xa = _post(_m_post_kernel, x_parts, modsel, [hf, hb, og], [m_norm_g[j][None, :]],
                       m_w_out[j].astype(BF16), ln, w1, w2, nlt, n_tiles, alpha)
        x_parts = (xa, xa, nlt)
    return xa[:, :s]
```

```python
import functools

import jax
import jax.numpy as jnp
import numpy as np
from jax import lax
from jax.experimental import pallas as pl
from jax.experimental.pallas import tpu as pltpu

F32 = jnp.float32
BF16 = jnp.bfloat16
HIGHEST = lax.Precision.HIGHEST

GRID_W = 64
ROPE_BASE = 10000.0
EPS = 1e-5
NEG = -1e30
LOG2E = 1.4426950408889634
H_MLA, MLA_NOPE, MLA_ROPE, MLA_V = 8, 64, 32, 64
MLA_Q_LORA, MLA_KV_LORA = 384, 256
H_RET, RET_DK, RET_DV = 4, 128, 128
H_M, M_DK, M_DV = 8, 64, 128

LANES = 128
ROW_TILE = 256
POST_TILE = 512
SCAN_BLOCK = 128
ATT_KEY_CHUNK = 512
ATT_Q_TILE = 1024
M_SCAN_CHUNK = SCAN_BLOCK
M_ONES_ROWS = 16
VMEM_LIMIT = 56 * 1024 * 1024


def _cparams(sem, flags=None):
    return pltpu.CompilerParams(dimension_semantics=sem, vmem_limit_bytes=VMEM_LIMIT, flags=flags)


def _const_spec(shape):
    nd = len(shape)
    return pl.BlockSpec(shape, lambda *_: (0,) * nd, pipeline_mode=pl.Buffered(1))


def _split_rows_specs(parts, nlt, tm, tile0=0):
    lat, ctx, ctx_tile0 = parts
    n = lat.shape[-1]
    specs = [pl.BlockSpec((1, tm, n), lambda bi, i: (bi, jnp.minimum(i + tile0, nlt - 1), 0)),
             pl.BlockSpec((1, tm, n), lambda bi, i: (bi, ctx_tile0 + jnp.maximum(i + tile0 - nlt, 0), 0))]
    return specs, [lat, ctx]


def _select_rows(nlt, lat_ref, ctx_ref, tile0=0, rows=slice(None)):
    return jnp.where(pl.program_id(1) + tile0 >= nlt, ctx_ref[0, rows, :], lat_ref[0, rows, :])


def _silu(v):
    return v * jax.nn.sigmoid(v)


def _layer_norm_rows(v, g, b):
    mu = jnp.mean(v, axis=-1, keepdims=True)
    d = v - mu
    var = jnp.mean(d * d, axis=-1, keepdims=True)
    return d * lax.rsqrt(var + EPS) * g + b


def _head_norm_lanes(v):
    outs = []
    for h in range(v.shape[-1] // LANES):
        blk = v[:, h * LANES:(h + 1) * LANES]
        mu = jnp.mean(blk, axis=-1, keepdims=True)
        d = blk - mu
        var = jnp.mean(d * d, axis=-1, keepdims=True)
        outs.append(d * lax.rsqrt(var + EPS))
    return outs


def _mod_kernel(c_ref, w_ref, b_ref, o_ref):
    sc = _silu(c_ref[...])
    o_ref[0] = jnp.dot(sc, w_ref[0], precision=HIGHEST, preferred_element_type=F32) + b_ref[0]


def _modulation(c_rows, mod_w, mod_b):
    depth, d, n = mod_w.shape
    tn = 1536
    return pl.pallas_call(
        _mod_kernel,
        grid=(depth, n // tn),
        in_specs=[pl.BlockSpec((8, d), lambda l, j: (0, 0)),
                  pl.BlockSpec((1, d, tn), lambda l, j: (l, 0, j)),
                  pl.BlockSpec((1, 1, tn), lambda l, j: (l, 0, j))],
        out_specs=pl.BlockSpec((1, 8, tn), lambda l, j: (l, 0, j)),
        out_shape=jax.ShapeDtypeStruct((depth, 8, n), F32),
        compiler_params=_cparams(("arbitrary", "arbitrary")),
        name="modulation",
    )(c_rows, mod_w, mod_b.reshape(depth, 1, n))


_A_CQ, _A_CKV, _A_RQ, _A_RK, _A_RV, _A_RG, _A_RQR, _A_RKR, _A_KR, _A_KRR, _A_END = (
    0, 384, 640, 1152, 1664, 2176, 2688, 3200, 3712, 3840, 3968)


def _ab_in_kernel(x_ref, xc_ref, mod_ref, w_ref, qn_ref, kvn_ref, wuq_ref, wukv_ref, tm_ref, tr_ref,
                  q_ref, kt_ref, v_ref, rq_ref, rkt_ref, rv_ref, rg_ref, *, nlt):
    x = _select_rows(nlt, x_ref, xc_ref)
    shift = mod_ref[0, 0, 0:1, :]
    scale = mod_ref[0, 0, 1:2, :]
    h = (x * (1.0 + scale) + shift).astype(BF16)
    y = jnp.dot(h, w_ref[...], preferred_element_type=F32)

    cq = y[:, _A_CQ:_A_CKV]
    ckv = y[:, _A_CKV:_A_RQ]
    ncq = (cq * lax.rsqrt(jnp.mean(cq * cq, axis=-1, keepdims=True) + EPS) * qn_ref[...]).astype(BF16)
    nckv = (ckv * lax.rsqrt(jnp.mean(ckv * ckv, axis=-1, keepdims=True) + EPS) * kvn_ref[...]).astype(BF16)
    q2 = jnp.dot(ncq, wuq_ref[...], preferred_element_type=F32)
    kv = jnp.dot(nckv, wukv_ref[...], preferred_element_type=F32)

    cos_m = tm_ref[0]
    sin_m = tm_ref[1]
    q_scale = (MLA_NOPE + MLA_ROPE) ** -0.5 * LOG2E
    k_rope = y[:, _A_KR:_A_KRR] * cos_m + y[:, _A_KRR:_A_END] * sin_m
    nq = H_MLA * LANES
    for hd in range(H_MLA):
        sl = slice(hd * LANES, (hd + 1) * LANES)
        qh = q2[:, sl] * cos_m + q2[:, nq + hd * LANES:nq + (hd + 1) * LANES] * sin_m
        q_ref[0, :, sl] = (qh * q_scale).astype(BF16)
        kt_ref[0, sl, :] = (kv[:, sl] + k_rope).T.astype(BF16)
    lane = lax.broadcasted_iota(jnp.int32, (1, nq), 1)
    ones_half = jnp.where(((lane // MLA_V) % 2) != ((lane // LANES) % 2), 1.0, 0.0)
    v_ref[0] = (kv[:, nq:] + ones_half).astype(BF16)

    cos_r = tr_ref[0]
    sin_r = tr_ref[1]
    k_scale = RET_DK ** -0.5
    for hd in range(H_RET):
        sl = slice(hd * LANES, (hd + 1) * LANES)
        rq = y[:, _A_RQ + hd * LANES:_A_RQ + (hd + 1) * LANES] * cos_r \
            + y[:, _A_RQR + hd * LANES:_A_RQR + (hd + 1) * LANES] * sin_r
        rk = y[:, _A_RK + hd * LANES:_A_RK + (hd + 1) * LANES] * cos_r \
            + y[:, _A_RKR + hd * LANES:_A_RKR + (hd + 1) * LANES] * sin_r
        rq_ref[0, :, sl] = rq.astype(BF16)
        rkt_ref[0, sl, :] = (rk * k_scale).T.astype(BF16)
    rv_ref[0] = y[:, _A_RV:_A_RG].astype(BF16)
    rg_ref[0] = y[:, _A_RG:_A_RQR].astype(BF16)


def _ab_in_proj(x_parts, modsel, w_ext, q_norm, kv_norm, wuq_ext, wukv_ext, tab_mla, tab_ret, nlt, t):
    b, d = modsel.shape[0], modsel.shape[-1]
    tm = ROW_TILE
    row = lambda bi, i: (bi, i, 0)
    col = lambda bi, i: (bi, 0, i)
    n_ret = H_RET * RET_DK
    widths = (H_MLA * LANES, H_MLA * LANES, H_MLA * LANES, n_ret, n_ret, n_ret, n_ret)
    transposed = (False, True, False, False, True, False, False)
    out_shapes = [jax.ShapeDtypeStruct((b, w, t) if tr else (b, t, w), BF16) for w, tr in zip(widths, transposed)]
    out_specs = [pl.BlockSpec((1, w, tm), col) if tr else pl.BlockSpec((1, tm, w), row)
                 for w, tr in zip(widths, transposed)]
    x_specs, x_args = _split_rows_specs(x_parts, nlt, tm)
    return pl.pallas_call(
        functools.partial(_ab_in_kernel, nlt=nlt),
        grid=(b, t // tm),
        in_specs=x_specs + [
                  pl.BlockSpec((1, 1, 6, d), lambda bi, i: (bi, (i >= nlt).astype(jnp.int32), 0, 0)),
                  _const_spec(w_ext.shape), _const_spec(q_norm.shape), _const_spec(kv_norm.shape),
                  _const_spec(wuq_ext.shape), _const_spec(wukv_ext.shape),
                  pl.BlockSpec((2, tm, LANES), lambda bi, i: (0, i, 0)),
                  pl.BlockSpec((2, tm, LANES), lambda bi, i: (0, i, 0))],
        out_specs=out_specs,
        out_shape=out_shapes,
        compiler_params=_cparams(("parallel", "arbitrary")),
        name="ab_in_proj",
    )(*x_args, modsel, w_ext, q_norm, kv_norm, wuq_ext, wukv_ext, tab_mla, tab_ret)


def _attn_kernel(q_ref, kt_ref, v_ref, o_ref, s_scr, m_scr, acc_scr, *, tk, n_lat, ctx_len, n_lat_chunks):
    tq = q_ref.shape[1]
    m_scr[...] = jnp.full(m_scr.shape, NEG, F32)
    acc_scr[...] = jnp.zeros_like(acc_scr)

    def scores(slot, r0, size):
        for hh in range(2):
            sl = slice(hh * LANES, (hh + 1) * LANES)
            s_scr[slot, hh, :, 0:size] = jnp.dot(q_ref[0, :, sl], kt_ref[0, sl, pl.ds(r0, size)],
                                                 preferred_element_type=F32)

    def accumulate(slot, r0, size):
        for hh in range(2):
            sl = slice(hh * LANES, (hh + 1) * LANES)
            s = s_scr[slot, hh, :, 0:size]
            m = m_scr[hh]
            m_new = jnp.maximum(m, jnp.max(s, axis=-1, keepdims=True))
            m_scr[hh] = m_new
            p = jnp.exp2(s - m_new[:, 0:1]).astype(BF16)
            acc_scr[hh] = jnp.exp2(m - m_new) * acc_scr[hh] \
                + jnp.dot(p, v_ref[0, pl.ds(r0, size), sl], preferred_element_type=F32)

    def lat_row(c):
        return pl.multiple_of(jnp.minimum(c, n_lat_chunks - 1) * tk, tk)

    scores(1, n_lat, ctx_len)
    if n_lat_chunks:
        scores(0, 0, tk)
    accumulate(1, n_lat, ctx_len)
    if n_lat_chunks:

        def body(j, carry):
            scores(1, lat_row(2 * j + 1), tk)
            accumulate(0, lat_row(2 * j), tk)
            scores(0, lat_row(2 * j + 2), tk)
            accumulate(1, lat_row(2 * j + 1), tk)
            return carry

        lax.fori_loop(0, n_lat_chunks // 2, body, 0)
    outs = [acc_scr[hh] / pltpu.roll(acc_scr[hh], MLA_V, axis=1) for hh in range(2)]
    lane = lax.broadcasted_iota(jnp.int32, (tq, LANES), 1)
    o_ref[0] = jnp.where(lane < MLA_V, outs[0], outs[1]).astype(BF16)


def _attention(q, kt, v, n_lat, for_ctx):
    b, t, _ = q.shape
    ctx_len = t - n_lat
    tk = ATT_KEY_CHUNK
    assert n_lat % (2 * tk) == 0 and n_lat % ATT_Q_TILE == 0 and ctx_len % ROW_TILE == 0
    tq, n_rows, row0 = (ROW_TILE, ctx_len, n_lat // ROW_TILE) if for_ctx else (ATT_Q_TILE, n_lat, 0)
    kern = functools.partial(_attn_kernel, tk=tk, n_lat=n_lat, ctx_len=ctx_len,
                             n_lat_chunks=0 if for_ctx else n_lat // tk)
    pair = 2 * LANES
    return pl.pallas_call(
        kern,
        grid=(b, H_MLA // 2, n_rows // tq),
        in_specs=[pl.BlockSpec((1, tq, pair), lambda bi, hp, i: (bi, i + row0, hp)),
                  pl.BlockSpec((1, pair, t), lambda bi, hp, i: (bi, hp, 0)),
                  pl.BlockSpec((1, t, pair), lambda bi, hp, i: (bi, 0, hp))],
        out_specs=pl.BlockSpec((1, tq, LANES), lambda bi, hp, i: (bi, i, hp)),
        out_shape=jax.ShapeDtypeStruct((b, n_rows, H_MLA * MLA_V), BF16),
        scratch_shapes=[pltpu.VMEM((2, 2, tq, max(tk, ctx_len)), F32), pltpu.VMEM((2, tq, LANES), F32),
                        pltpu.VMEM((2, tq, LANES), F32)],
        compiler_params=_cparams(("parallel", "parallel", "arbitrary")),
        name="mla_attention_ctx" if for_ctx else "mla_attention",
    )(q, kt, v)


def _ret_kernel(ld_ref, qf_ref, ktf_ref, vf_ref, qb_ref, ktb_ref, vb_ref, of_ref, ob_ref,
                state, dmat, dq, wk, gl, s_scr):
    c = pl.program_id(1)
    L = SCAN_BLOCK

    @pl.when(c == 0)
    def _():
        state[...] = jnp.zeros_like(state)
        row = lax.broadcasted_iota(jnp.int32, (L, L), 0).astype(F32)
        col = lax.broadcasted_iota(jnp.int32, (L, L), 1).astype(F32)
        for d in range(2):
            for hd in range(H_RET):
                lg = ld_ref[d, hd]
                if d == 0:
                    rel = row - col
                    dq[d, hd] = jnp.exp(lg * (row + 1.0))
                    wk[d, hd] = jnp.exp(lg * (L - 1.0 - col))
                else:
                    rel = col - row
                    dq[d, hd] = jnp.exp(lg * (L - row))
                    wk[d, hd] = jnp.exp(lg * col)
                dmat[d, hd] = jnp.where(rel >= 0, jnp.exp(lg * jnp.maximum(rel, 0.0)), 0.0)
                gl[d, hd] = jnp.exp(jnp.zeros((L, L), F32) + lg * L)

    streams = ((qf_ref, ktf_ref, vf_ref, of_ref), (qb_ref, ktb_ref, vb_ref, ob_ref))
    chains = [(d, hd, slice(hd * LANES, (hd + 1) * LANES)) + streams[d]
              for d in range(2) for hd in range(H_RET)]
    for d, hd, sl, q_ref, kt_ref, _, _ in chains:
        s = jnp.dot(q_ref[0, :, sl], kt_ref[0, sl, :], preferred_element_type=F32) * dmat[d, hd]
        s_scr[d, hd] = s.astype(BF16)
    for d, hd, sl, q_ref, _, v_ref, o_ref in chains:
        qd = (q_ref[0, :, sl].astype(F32) * dq[d, hd]).astype(BF16)
        o_ref[0, :, sl] = jnp.dot(s_scr[d, hd], v_ref[0, :, sl], preferred_element_type=F32) \
            + jnp.dot(qd, state[d, hd].astype(BF16), preferred_element_type=F32)
    for d, hd, sl, _, kt_ref, v_ref, _ in chains:
        kw = (kt_ref[0, sl, :].astype(F32) * wk[d, hd]).astype(BF16)
        state[d, hd] = state[d, hd] * gl[d, hd] + jnp.dot(kw, v_ref[0, :, sl], preferred_element_type=F32)


def _scan_maps(n_lat_blocks, n_ctx_blocks):
    nlb, ncb = n_lat_blocks, n_ctx_blocks
    fwd_blk = lambda c: jnp.where(c < ncb, nlb + c, c - ncb)
    bwd_blk = lambda c: nlb + ncb - 1 - c
    fwd = lambda bi, c: (bi, fwd_blk(c), 0)
    bwd = lambda bi, c: (bi, bwd_blk(c), 0)
    fwd_t = lambda bi, c: (bi, 0, fwd_blk(c))
    bwd_t = lambda bi, c: (bi, 0, bwd_blk(c))
    return fwd, bwd, fwd_t, bwd_t


def _retention(rq, rkt, rv, log_decay, n_lat):
    b, t, n = rq.shape
    L = SCAN_BLOCK
    nb = t // L
    fwd, bwd, fwd_t, bwd_t = _scan_maps(n_lat // L, nb - n_lat // L)
    tile = pl.BlockSpec((1, L, n), fwd)
    tile_b = pl.BlockSpec((1, L, n), bwd)
    out = jax.ShapeDtypeStruct((b, t, n), F32)
    sq = pltpu.VMEM((2, H_RET, L, L), F32)
    return pl.pallas_call(
        _ret_kernel,
        grid=(b, nb),
        in_specs=[pl.BlockSpec(memory_space=pltpu.SMEM),
                  tile, pl.BlockSpec((1, n, L), fwd_t), tile,
                  tile_b, pl.BlockSpec((1, n, L), bwd_t), tile_b],
        out_specs=[tile, tile_b],
        out_shape=[out, out],
        scratch_shapes=[sq, sq, sq, sq, sq, pltpu.VMEM((2, H_RET, L, L), BF16)],
        compiler_params=_cparams(("parallel", "arbitrary")),
        name="retention_scan",
    )(log_decay, rq, rkt, rv, rq, rkt, rv)


def _post_stages(xs, mixes, mod_ref, wo_ref, ln_ref, w1_ref, w2_ref, alpha):
    g1 = mod_ref[0, 0, 2:3, :]
    sh2 = mod_ref[0, 0, 3:4, :]
    sc2 = mod_ref[0, 0, 4:5, :]
    g2 = mod_ref[0, 0, 5:6, :]
    ys = [jnp.dot(m, wo_ref[...], preferred_element_type=F32) for m in mixes]
    x1s = [_layer_norm_rows(alpha * x + g1 * y, ln_ref[0:1, :], ln_ref[1:2, :]) for x, y in zip(xs, ys)]
    us = [jnp.dot((x1 * (1.0 + sc2) + sh2).astype(BF16), w1_ref[...], preferred_element_type=F32) for x1 in x1s]
    dff = w1_ref.shape[-1] // 2
    acts = [(_silu(u[:, dff:]) * u[:, :dff]).astype(BF16) for u in us]
    y2s = [jnp.dot(a, w2_ref[...], preferred_element_type=F32) for a in acts]
    return [_layer_norm_rows(alpha * x1 + g2 * y2, ln_ref[2:3, :], ln_ref[3:4, :]) for x1, y2 in zip(x1s, y2s)]


def _sub_tiles(o_ref):
    return [slice(r, r + ROW_TILE) for r in range(0, o_ref.shape[1], ROW_TILE)]


def _ab_post_kernel(x_ref, xc_ref, mod_ref, att_ref, attc_ref, rf_ref, rb_ref, rg_ref,
                    wo_ref, ln_ref, w1_ref, w2_ref, o_ref, *, alpha, nlt, tile0):
    subs = _sub_tiles(o_ref)
    mixes = []
    for rows in subs:
        ret = rf_ref[0, rows, :] + rb_ref[0, rows, :]
        gate = _silu(rg_ref[0, rows, :].astype(F32))
        parts = [_select_rows(nlt, att_ref, attc_ref, tile0, rows)]
        for hd, nh in enumerate(_head_norm_lanes(ret)):
            parts.append((gate[:, hd * LANES:(hd + 1) * LANES] * nh).astype(BF16))
        mixes.append(jnp.concatenate(parts, axis=-1))
    xs = [_select_rows(nlt, x_ref, xc_ref, tile0, rows) for rows in subs]
    for rows, out in zip(subs, _post_stages(xs, mixes, mod_ref, wo_ref, ln_ref, w1_ref, w2_ref, alpha)):
        o_ref[0, rows, :] = out


def _m_post_kernel(x_ref, xc_ref, mod_ref, hf_ref, hb_ref, og_ref, ng_ref,
                   wo_ref, ln_ref, w1_ref, w2_ref, o_ref, *, alpha, nlt, tile0):
    subs = _sub_tiles(o_ref)
    mixes = []
    for rows in subs:
        hs = hf_ref[0, rows, :] + hb_ref[0, rows, :]
        og = jax.nn.sigmoid(og_ref[0, rows, :].astype(F32))
        parts = []
        for hd, nh in enumerate(_head_norm_lanes(hs)):
            sl = slice(hd * LANES, (hd + 1) * LANES)
            parts.append((og[:, sl] * (nh * ng_ref[:, sl])).astype(BF16))
        mixes.append(jnp.concatenate(parts, axis=-1))
    xs = [_select_rows(nlt, x_ref, xc_ref, tile0, rows) for rows in subs]
    for rows, out in zip(subs, _post_stages(xs, mixes, mod_ref, wo_ref, ln_ref, w1_ref, w2_ref, alpha)):
        o_ref[0, rows, :] = out


def _post(kernel_fn, x_parts, modsel, acts, consts, w_out, ln, w1, w2, n_lat, tm, tile0, n_tiles, alpha):
    b, d = modsel.shape[0], modsel.shape[-1]
    assert n_lat % tm == 0 and tm % ROW_TILE == 0
    nlt = n_lat // tm
    row = lambda bi, i: (bi, i + tile0, 0)
    in_specs, args = _split_rows_specs(x_parts, nlt, tm, tile0)
    in_specs.append(pl.BlockSpec((1, 1, 6, d), lambda bi, i: (bi, (i + tile0 >= nlt).astype(jnp.int32), 0, 0)))
    args.append(modsel)
    for a in acts:
        if isinstance(a, tuple):
            sp, ar = _split_rows_specs(a, nlt, tm, tile0)
            in_specs += sp
            args += ar
        else:
            in_specs.append(pl.BlockSpec((1, tm, a.shape[-1]), row))
            args.append(a)
    weights = [*consts, w_out, ln, w1, w2]
    in_specs += [_const_spec(w.shape) for w in weights]
    return pl.pallas_call(
        functools.partial(kernel_fn, alpha=alpha, nlt=nlt, tile0=tile0),
        grid=(b, n_tiles),
        in_specs=in_specs,
        out_specs=pl.BlockSpec((1, tm, d), lambda bi, i: (bi, i, 0)),
        out_shape=jax.ShapeDtypeStruct((b, n_tiles * tm, d), F32),
        compiler_params=_cparams(("parallel", "arbitrary")),
        name="post_" + kernel_fn.__name__ + ("_ctx" if tile0 else ""),
    )(*args, *weights)


_M_QK, _M_V, _M_OG, _M_G, _M_END = 0, 1024, 2048, 3072, 3200


def _m_in_kernel(x_ref, xc_ref, xp_ref, xpc_ref, xn_ref, xnc_ref, mod_ref, w_ref, cw_ref, cb_ref, gb_ref,
                 qt_ref, k_ref, vt_ref, og_ref, g1_ref, g2_ref, gc_ref, *, nlt, seg_starts, seg_ends):
    i = pl.program_id(1)
    tm = x_ref.shape[1]
    shift = mod_ref[0, 0, 0:1, :]
    scale = mod_ref[0, 0, 1:2, :]
    h = (_select_rows(nlt, x_ref, xc_ref) * (1.0 + scale) + shift).astype(BF16)
    y = jnp.dot(h, w_ref[...], preferred_element_type=F32)
    hp = (_select_rows(nlt, xp_ref, xpc_ref) * (1.0 + scale) + shift).astype(BF16)
    hn = (_select_rows(nlt, xn_ref, xnc_ref) * (1.0 + scale) + shift).astype(BF16)
    wqk = w_ref[:, _M_QK:_M_V]
    up = jnp.dot(hp, wqk, preferred_element_type=F32)[7:8, :]
    un = jnp.dot(hn, wqk, preferred_element_type=F32)[0:1, :]
    is_start = functools.reduce(jnp.logical_or, [i == s for s in seg_starts])
    is_end = functools.reduce(jnp.logical_or, [i == e for e in seg_ends])
    up = jnp.where(is_start, 0.0, up)
    un = jnp.where(is_end, 0.0, un)

    u = y[:, _M_QK:_M_V]
    rows = lax.broadcasted_iota(jnp.int32, u.shape, 0)
    u_prev = jnp.where(rows == 0, up, pltpu.roll(u, 1, axis=0))
    u_next = jnp.where(rows == tm - 1, un, pltpu.roll(u, tm - 1, axis=0))
    qk = _silu(cw_ref[0:1, :] * u_prev + cw_ref[1:2, :] * u + cw_ref[2:3, :] * u_next + cb_ref[...])
    nq = H_M * M_DK
    for grp in range(nq // LANES):
        sl = slice(grp * LANES, (grp + 1) * LANES)
        qt_ref[0, sl, :] = qk[:, sl].T.astype(BF16)
    for hd in range(H_M):
        sl = slice(hd * LANES, (hd + 1) * LANES)
        vt_ref[0, sl, :] = y[:, _M_V + hd * LANES:_M_V + (hd + 1) * LANES].T.astype(BF16)
    k_ref[0] = (qk[:, nq:] * (M_DK ** -0.5)).astype(BF16)
    og_ref[0] = y[:, _M_OG:_M_G].astype(BF16)

    g = y[:, _M_G:_M_END] + gb_ref[...]
    lsg = jnp.minimum(g, 0.0) - jnp.log1p(jnp.exp(-jnp.abs(g)))
    r = lax.broadcasted_iota(jnp.int32, (tm, tm), 0)
    cidx = lax.broadcasted_iota(jnp.int32, (tm, tm), 1)
    same = (r // M_SCAN_CHUNK) == (cidx // M_SCAN_CHUNK)
    tril = jnp.where(jnp.logical_and(same, cidx <= r), 1.0, 0.0)
    triu = jnp.where(jnp.logical_and(same, cidx >= r), 1.0, 0.0)
    pre = jnp.dot(tril, lsg, precision=HIGHEST, preferred_element_type=F32)
    suf = jnp.dot(triu, lsg, precision=HIGHEST, preferred_element_type=F32)
    lane = lax.broadcasted_iota(jnp.int32, g.shape, 1)
    grp = lane // H_M
    g1_ref[0] = jnp.where(grp == 1, pre, jnp.where(grp == 3, suf, g))
    bsum = jnp.where(grp == 1, pre, jnp.where(grp == 3, suf, 0.0))
    b_at_i = pltpu.roll(bsum, LANES - H_M, axis=1)
    r_gate = g - b_at_i
    pos = lax.broadcasted_iota(jnp.int32, g.shape, 0) % M_SCAN_CHUNK
    pmax = r_gate
    smax = r_gate
    step = 1
    while step < M_SCAN_CHUNK:
        pmax = jnp.where(pos >= step, jnp.maximum(pmax, pltpu.roll(pmax, step, axis=0)), pmax)
        smax = jnp.where(pos < M_SCAN_CHUNK - step, jnp.maximum(smax, pltpu.roll(smax, tm - step, axis=0)), smax)
        step *= 2
    gc_ref[0] = r_gate
    g2_ref[0] = b_at_i + jnp.where(grp == 0, pmax, smax)


def _m_in_proj(x_parts, modsel, w_ext, conv_w, conv_b, gate_b, nlt, t):
    b, d = modsel.shape[0], modsel.shape[-1]
    tm = ROW_TILE
    nt = t // tm
    r8 = tm // 8
    lat, ctx, ctx_tile0 = x_parts
    c0, n8_lat, n8_ctx = ctx_tile0 * r8, nlt * r8, (nt - nlt) * r8
    halo = lambda off, lo, n8: (lambda bi, i: (bi, lo + jnp.clip(i * r8 + off, 0, n8 - 1), 0))
    halo_ctx = lambda off: (lambda bi, i: (bi, c0 + jnp.clip((i - nlt) * r8 + off, 0, n8_ctx - 1), 0))
    x_specs, x_args = _split_rows_specs(x_parts, nlt, tm)
    x_specs += [pl.BlockSpec((1, 8, d), halo(-1, 0, n8_lat)), pl.BlockSpec((1, 8, d), halo_ctx(-1)),
                pl.BlockSpec((1, 8, d), halo(r8, 0, n8_lat)), pl.BlockSpec((1, 8, d), halo_ctx(r8))]
    x_args += [lat, ctx, lat, ctx]
    row = lambda bi, i: (bi, i, 0)
    col = lambda bi, i: (bi, 0, i)
    nk, n = H_M * M_DK, H_M * M_DV
    out_shapes = [jax.ShapeDtypeStruct((b, nk, t), BF16), jax.ShapeDtypeStruct((b, t, nk), BF16),
                  jax.ShapeDtypeStruct((b, n, t), BF16), jax.ShapeDtypeStruct((b, t, n), BF16)] + \
                 [jax.ShapeDtypeStruct((b, t, LANES), F32)] * 3
    out_specs = [pl.BlockSpec((1, nk, tm), col), pl.BlockSpec((1, tm, nk), row),
                 pl.BlockSpec((1, n, tm), col), pl.BlockSpec((1, tm, n), row)] + \
                [pl.BlockSpec((1, tm, LANES), row)] * 3
    kern = functools.partial(_m_in_kernel, nlt=nlt, seg_starts=(0, nlt), seg_ends=(nlt - 1, nt - 1))
    return pl.pallas_call(
        kern,
        grid=(b, nt),
        in_specs=x_specs + [
                  pl.BlockSpec((1, 1, 6, d), lambda bi, i: (bi, (i >= nlt).astype(jnp.int32), 0, 0)),
                  _const_spec(w_ext.shape), _const_spec(conv_w.shape), _const_spec(conv_b.shape),
                  _const_spec(gate_b.shape)],
        out_specs=out_specs,
        out_shape=out_shapes,
        compiler_params=_cparams(("parallel", "arbitrary")),
        name="m_in_proj",
    )(*x_args, modsel, w_ext, conv_w, conv_b, gate_b)


def _mlstm_gates(gc, gt1, gt2, hd, backward, m_prev):
    L = M_SCAN_CHUNK
    gi = 2 * H_M if backward else 0
    i_row = gt1[gi + hd:gi + hd + 1, :]
    b_row = gt1[gi + H_M + hd:gi + H_M + hd + 1, :]
    mloc_row = gt2[gi + hd:gi + hd + 1, :]
    r_col = gc[:, gi + hd:gi + hd + 1]
    inter = b_row + m_prev
    m_t = jnp.maximum(inter, mloc_row)
    e = 0 if backward else L - 1
    b_end = b_row[:, e:e + 1]
    m_new = jnp.maximum(b_end + m_prev, mloc_row[:, e:e + 1])
    return i_row, b_row, r_col, inter, m_t, b_end, m_new


def _mlstm_kernel(*refs):
    ins, (hf_ref, hb_ref, cstate, mstate, sc_scr) = refs[:12], refs[12:]
    c = pl.program_id(1)
    L = M_SCAN_CHUNK

    @pl.when(c == 0)
    def _():
        cstate[...] = jnp.zeros_like(cstate)
        mstate[...] = jnp.zeros_like(mstate)

    chains = [(d, hd) for d in range(2) for hd in range(H_M)]

    def operands(d, hd):
        k_ref, qt_ref, vt_ref, gc_ref, gt1_ref, gt2_ref = ins[6 * d:6 * d + 6]
        pair = slice((hd // 2) * LANES, (hd // 2 + 1) * LANES)
        k_pair = k_ref[0, :, pair]
        lane = lax.broadcasted_iota(jnp.int32, k_pair.shape, 1)
        k_own = jnp.where((lane // M_DK) == (hd % 2), k_pair, jnp.zeros_like(k_pair))
        gates = _mlstm_gates(gc_ref[0], gt1_ref[0], gt2_ref[0], hd, d == 1, mstate[d * H_M + hd][0:1, 0:1])
        return k_own, qt_ref[0, pair, :], vt_ref[0, hd * LANES:(hd + 1) * LANES, :], gates

    ones_rows = jnp.ones((M_ONES_ROWS, L), BF16)
    s_idx = lax.broadcasted_iota(jnp.int32, (L, L), 0)
    t_idx = lax.broadcasted_iota(jnp.int32, (L, L), 1)

    for ch, (d, hd) in enumerate(chains):
        k_own, qt, _, (_, b_row, r_col, _, m_t, _, _) = operands(d, hd)
        visible = (s_idx >= t_idx) if d == 1 else (s_idx <= t_idx)
        dt = jnp.exp(jnp.where(visible, r_col + (b_row - m_t), NEG))
        sc_scr[ch] = (jnp.dot(k_own, qt, preferred_element_type=F32) * dt).astype(BF16)

    for ch, (d, hd) in enumerate(chains):
        _, qt, vt, (_, _, _, inter, m_t, _, _) = operands(d, hd)
        vext_t = jnp.concatenate([vt, ones_rows], axis=0)
        nd_t = jnp.dot(vext_t, sc_scr[ch], preferred_element_type=F32) \
            + jnp.dot(cstate[ch].astype(BF16), qt, preferred_element_type=F32) * jnp.exp(inter - m_t)
        den = jnp.broadcast_to(nd_t[LANES:LANES + 1, :], (LANES, L))
        h_t = nd_t[:LANES] / jnp.maximum(jnp.abs(den), jnp.exp(-m_t))
        (hb_ref if d else hf_ref)[0, :, hd * LANES:(hd + 1) * LANES] = h_t.T

    for ch, (d, hd) in enumerate(chains):
        k_own, _, vt, (i_row, b_row, _, _, _, b_end, m_new) = operands(d, hd)
        m_prev = mstate[ch][0:1, 0:1]
        vext_t = jnp.concatenate([vt, ones_rows], axis=0)
        w_row = jnp.exp(b_end - b_row + i_row - m_new)
        cstate[ch] = jnp.exp(b_end + m_prev - m_new) * cstate[ch] \
            + jnp.dot((vext_t.astype(F32) * w_row).astype(BF16), k_own, preferred_element_type=F32)
        mstate[ch] = jnp.broadcast_to(m_new, mstate.shape[1:])


def _mlstm(k, qt, vt, gc, gt1, gt2, n_lat):
    b, t, nk = k.shape
    n = vt.shape[1]
    L = SCAN_BLOCK
    nb = t // L
    fwd, bwd, fwd_t, bwd_t = _scan_maps(n_lat // L, nb - n_lat // L)

    def specs(rmap, tmap):
        return [pl.BlockSpec((1, L, nk), rmap), pl.BlockSpec((1, nk, L), tmap), pl.BlockSpec((1, n, L), tmap),
                pl.BlockSpec((1, L, LANES), rmap), pl.BlockSpec((1, gt1.shape[1], L), tmap),
                pl.BlockSpec((1, gt2.shape[1], L), tmap)]

    out = jax.ShapeDtypeStruct((b, t, H_M * M_DV), F32)
    args = (k, qt, vt, gc, gt1, gt2)
    return pl.pallas_call(
        _mlstm_kernel,
        grid=(b, nb),
        in_specs=specs(fwd, fwd_t) + specs(bwd, bwd_t),
        out_specs=[pl.BlockSpec((1, L, H_M * M_DV), fwd), pl.BlockSpec((1, L, H_M * M_DV), bwd)],
        out_shape=[out, out],
        scratch_shapes=[pltpu.VMEM((2 * H_M, LANES + M_ONES_ROWS, LANES), F32),
                        pltpu.VMEM((2 * H_M, 8, LANES), F32),
                        pltpu.VMEM((2 * H_M, L, L), BF16)],
        compiler_params=_cparams(("parallel", "arbitrary")),
        name="mlstm_scan",
    )(*args, *args)


def _pair_swap_index(d):
    q = d // 4
    j = np.arange(d)
    return np.where((j // q) % 2 == 0, j + q, j - q)


def _rope_tables(n_lat, ctx_len, d, lane0):
    rows = n_lat // GRID_W
    row = np.repeat(np.arange(rows, dtype=np.float64), GRID_W)
    col = np.tile(np.arange(GRID_W, dtype=np.float64), rows)
    da = d // 2
    inv = ROPE_BASE ** (-np.arange(0, da, 2, dtype=np.float64) / da)
    ar = row[:, None] * inv
    ac = col[:, None] * inv
    tab = np.zeros((2, n_lat + ctx_len, LANES), np.float32)
    tab[0] = 1.0
    tab[0, :n_lat, lane0:lane0 + d] = np.concatenate([np.cos(ar), np.cos(ar), np.cos(ac), np.cos(ac)], axis=-1)
    tab[1, :n_lat, lane0:lane0 + d] = np.concatenate([-np.sin(ar), np.sin(ar), -np.sin(ac), np.sin(ac)], axis=-1)
    return jnp.asarray(tab)


def _pad_heads(w, n_heads, width):
    k = w.shape[0]
    w = w.reshape(k, n_heads, width)
    return jnp.pad(w, ((0, 0), (0, 0), (0, LANES - width))).reshape(k, n_heads * LANES)


def _prep_ab_weights(w_in, w_uq, w_ukv):
    o1 = MLA_Q_LORA
    o2 = o1 + MLA_KV_LORA
    o3 = o2 + MLA_ROPE
    o4 = o3 + H_RET * RET_DK
    o5 = o4 + H_RET * RET_DK
    o6 = o5 + H_RET * RET_DV
    d = w_in.shape[0]
    cq, ckv, kr, rq, rk, rv, rg = (w_in[:, :o1], w_in[:, o1:o2], w_in[:, o2:o3], w_in[:, o3:o4],
                                   w_in[:, o4:o5], w_in[:, o5:o6], w_in[:, o6:])
    ret_swap = np.concatenate([hd * RET_DK + _pair_swap_index(RET_DK) for hd in range(H_RET)])
    kr_blk = jnp.zeros((d, LANES), F32).at[:, MLA_NOPE:MLA_NOPE + MLA_ROPE].set(kr)
    krr_blk = jnp.zeros((d, LANES), F32).at[:, MLA_NOPE:MLA_NOPE + MLA_ROPE].set(kr[:, _pair_swap_index(MLA_ROPE)])
    w_ext = jnp.concatenate([cq, ckv, rq, rk, rv, rg, rq[:, ret_swap], rk[:, ret_swap], kr_blk, krr_blk], axis=1)

    dqh = MLA_NOPE + MLA_ROPE
    uq = w_uq.reshape(-1, H_MLA, dqh)
    uq_rot = jnp.concatenate([jnp.zeros_like(uq[..., :MLA_NOPE]),
                              uq[..., MLA_NOPE:][..., _pair_swap_index(MLA_ROPE)]], axis=-1)
    wuq_ext = jnp.concatenate([_pad_heads(uq.reshape(-1, H_MLA * dqh), H_MLA, dqh),
                               _pad_heads(uq_rot.reshape(-1, H_MLA * dqh), H_MLA, dqh)], axis=1)
    ukv = w_ukv.reshape(-1, H_MLA, MLA_NOPE + MLA_V)
    k_nope = _pad_heads(ukv[..., :MLA_NOPE].reshape(-1, H_MLA * MLA_NOPE), H_MLA, MLA_NOPE)
    v_w = ukv[..., MLA_NOPE:]
    zeros = jnp.zeros_like(v_w)
    even = (np.arange(H_MLA) % 2 == 0)[None, :, None]
    v_cols = jnp.concatenate([jnp.where(even, v_w, zeros), jnp.where(even, zeros, v_w)], axis=-1)
    wukv_ext = jnp.concatenate([k_nope, v_cols.reshape(-1, H_MLA * LANES)], axis=1)
    return w_ext.astype(BF16), wuq_ext.astype(BF16), wukv_ext.astype(BF16)


def _prep_m_weights(w_in, conv_w, conv_b, gate_b):
    o1 = 2 * H_M * M_DK
    o2 = o1 + H_M * M_DV
    o3 = o2 + H_M * M_DV
    g_pad = jnp.pad(w_in[:, o3:], ((0, 0), (0, LANES - 4 * H_M)))
    w_ext = jnp.concatenate([w_in[:, :o3], g_pad], axis=1).astype(BF16)
    gb = jnp.pad(gate_b.reshape(1, 4 * H_M), ((0, 0), (0, LANES - 4 * H_M)))
    return w_ext, conv_w, conv_b[None, :], gb


def kernel(x, c, ctx, c_ctx, mod_w, mod_b, ln_g, ln_b, ffn_w_in, ffn_w_out, ab_w_in, mla_q_norm, mla_w_uq,
           mla_kv_norm, mla_w_ukv, ret_log_decay, ab_w_out, m_w_in, m_conv_w, m_conv_b, m_gate_b, m_norm_g, m_w_out):
    b, s, d = x.shape
    ctx_len = ctx.shape[1]
    depth = mod_w.shape[0]
    assert ctx_len % ROW_TILE == 0 and s % ROW_TILE == 0 and s % GRID_W == 0 and b + 1 <= 8
    nlt = s // ROW_TILE
    alpha = (2 * depth) ** 0.25

    c_rows = jnp.zeros((8, d), F32).at[:b].set(c).at[b].set(c_ctx)
    mods = _modulation(c_rows, mod_w, mod_b).reshape(depth, 8, 6, d)
    tab_mla = _rope_tables(s, ctx_len, MLA_ROPE, MLA_NOPE)
    tab_ret = _rope_tables(s, ctx_len, RET_DK, 0)

    t = s + ctx_len
    x_lat, x_ctx = x, ctx
    for l in range(depth):
        last = l == depth - 1
        j = l // 2
        modsel = jnp.stack([mods[l, :b], jnp.broadcast_to(mods[l, b], (b, 6, d))], axis=1)
        ln = jnp.stack([ln_g[l, 0], ln_b[l, 0], ln_g[l, 1], ln_b[l, 1]])
        w1 = ffn_w_in[l].astype(BF16)
        w2 = ffn_w_out[l].astype(BF16)
        x_parts = (x_lat, x_ctx, 0)
        if l % 2 == 0:
            w_ext, wuq_ext, wukv_ext = _prep_ab_weights(ab_w_in[j], mla_w_uq[j], mla_w_ukv[j])
            q, kt, v, rq, rkt, rv, rg = _ab_in_proj(x_parts, modsel, w_ext, mla_q_norm[j][None, :],
                                                    mla_kv_norm[j][None, :], wuq_ext, wukv_ext,
                                                    tab_mla, tab_ret, nlt, t)
            att_lat = _attention(q, kt, v, s, False)
            att_ctx = None if last else _attention(q, kt, v, s, True)
            ret_f, ret_b = _retention(rq, rkt, rv, ret_log_decay[j], s)
            post_fn, acts, consts, w_out = _ab_post_kernel, [ret_f, ret_b, rg], [], ab_w_out[j]
        else:
            w_ext, cw, cb, gb = _prep_m_weights(m_w_in[j], m_conv_w[j], m_conv_b[j], m_gate_b[j])
            qt, k, vt, og, g1, g2, gc = _m_in_proj(x_parts, modsel, w_ext, cw, cb, gb, nlt, t)
            gt1 = jnp.swapaxes(g1[:, :, :4 * H_M], 1, 2)
            gt2 = jnp.swapaxes(g2[:, :, :4 * H_M], 1, 2)
            hf, hb = _mlstm(k, qt, vt, gc, gt1, gt2, s)
            post_fn, acts, consts, w_out = _m_post_kernel, [hf, hb, og], [m_norm_g[j][None, :]], m_w_out[j]
            att_lat = att_ctx = None

        def post(tm, tile0, n_tiles, for_ctx):
            pair = (lambda a_lat, a_ctx: (a_lat, a_ctx, 0) if for_ctx else (a_lat, a_lat, 0))
            att = [] if att_lat is None else [pair(att_lat, att_ctx)]
            return _post(post_fn, pair(x_lat, x_ctx), modsel, att + acts, consts, w_out.astype(BF16),
                         ln, w1, w2, s, tm, tile0, n_tiles, alpha)

        new_lat = post(POST_TILE, 0, s // POST_TILE, False)
        if not last:
            x_ctx = post(ROW_TILE, nlt, ctx_len // ROW_TILE, True)
        x_lat = new_lat
    return x_lat
```

```python
import functools

import jax
import jax.numpy as jnp
import numpy as np
from jax import lax
from jax.experimental import pallas as pl
from jax.experimental.pallas import tpu as pltpu

F32 = jnp.float32
BF16 = jnp.bfloat16
HIGHEST = lax.Precision.HIGHEST

GRID_W = 64
ROPE_BASE = 10000.0
EPS = 1e-5
NEG = -1e30
LOG2E = 1.4426950408889634
H_MLA, MLA_NOPE, MLA_ROPE, MLA_V = 8, 64, 32, 64
MLA_Q_LORA, MLA_KV_LORA = 384, 256
H_RET, RET_DK, RET_DV = 4, 128, 128
H_M, M_DK, M_DV = 8, 64, 128

LANES = 128
ROW_TILE = 256
POST_TILE = 512
SCAN_BLOCK = 128
ATT_KEY_CHUNK = 512
ATT_Q_TILE = 1024
M_SCAN_CHUNK = SCAN_BLOCK
M_ONES_ROWS = 16
VMEM_LIMIT = 56 * 1024 * 1024


def _cparams(sem, flags=None):
    return pltpu.CompilerParams(dimension_semantics=sem, vmem_limit_bytes=VMEM_LIMIT, flags=flags)


def _const_spec(shape):
    nd = len(shape)
    return pl.BlockSpec(shape, lambda *_: (0,) * nd, pipeline_mode=pl.Buffered(1))


def _split_rows_specs(parts, nlt, tm, tile0=0):
    lat, ctx, ctx_tile0 = parts
    n = lat.shape[-1]
    specs = [pl.BlockSpec((1, tm, n), lambda bi, i: (bi, jnp.minimum(i + tile0, nlt - 1), 0)),
             pl.BlockSpec((1, tm, n), lambda bi, i: (bi, ctx_tile0 + jnp.maximum(i + tile0 - nlt, 0), 0))]
    return specs, [lat, ctx]


def _select_rows(nlt, lat_ref, ctx_ref, tile0=0, rows=slice(None)):
    return jnp.where(pl.program_id(1) + tile0 >= nlt, ctx_ref[0, rows, :], lat_ref[0, rows, :])


def _silu(v):
    return v * jax.nn.sigmoid(v)


def _layer_norm_rows(v, g, b):
    mu = jnp.mean(v, axis=-1, keepdims=True)
    d = v - mu
    var = jnp.mean(d * d, axis=-1, keepdims=True)
    return d * lax.rsqrt(var + EPS) * g + b


def _head_norm_lanes(v):
    outs = []
    for h in range(v.shape[-1] // LANES):
        blk = v[:, h * LANES:(h + 1) * LANES]
        mu = jnp.mean(blk, axis=-1, keepdims=True)
        d = blk - mu
        var = jnp.mean(d * d, axis=-1, keepdims=True)
        outs.append(d * lax.rsqrt(var + EPS))
    return outs


def _mod_kernel(c_ref, w_ref, b_ref, o_ref):
    sc = _silu(c_ref[...])
    o_ref[0] = jnp.dot(sc, w_ref[0], precision=HIGHEST, preferred_element_type=F32) + b_ref[0]


def _modulation(c_rows, mod_w, mod_b):
    depth, d, n = mod_w.shape
    tn = 1536
    return pl.pallas_call(
        _mod_kernel,
        grid=(depth, n // tn),
        in_specs=[pl.BlockSpec((8, d), lambda l, j: (0, 0)),
                  pl.BlockSpec((1, d, tn), lambda l, j: (l, 0, j)),
                  pl.BlockSpec((1, 1, tn), lambda l, j: (l, 0, j))],
        out_specs=pl.BlockSpec((1, 8, tn), lambda l, j: (l, 0, j)),
        out_shape=jax.ShapeDtypeStruct((depth, 8, n), F32),
        compiler_params=_cparams(("arbitrary", "arbitrary")),
        name="modulation",
    )(c_rows, mod_w, mod_b.reshape(depth, 1, n))


_A_CQ, _A_CKV, _A_RQ, _A_RK, _A_RV, _A_RG, _A_KR, _A_END = 0, 384, 640, 1152, 1664, 2176, 2688, 2816


def _swap_pairs(v, width):
    lane = lax.broadcasted_iota(jnp.int32, v.shape, 1)
    return jnp.where((lane // width) % 2 == 0, pltpu.roll(v, LANES - width, axis=1), pltpu.roll(v, width, axis=1))


def _ab_in_kernel(x_ref, xc_ref, mod_ref, w_ref, qn_ref, kvn_ref, wuq_ref, wukv_ref, tm_ref, tr_ref,
                  q_ref, kt_ref, v_ref, rq_ref, rkt_ref, rv_ref, rg_ref, *, nlt):
    x = _select_rows(nlt, x_ref, xc_ref)
    shift = mod_ref[0, 0, 0:1, :]
    scale = mod_ref[0, 0, 1:2, :]
    h = (x * (1.0 + scale) + shift).astype(BF16)
    y = jnp.dot(h, w_ref[...], preferred_element_type=F32)

    cq = y[:, _A_CQ:_A_CKV]
    ckv = y[:, _A_CKV:_A_RQ]
    ncq = (cq * lax.rsqrt(jnp.mean(cq * cq, axis=-1, keepdims=True) + EPS) * qn_ref[...]).astype(BF16)
    nckv = (ckv * lax.rsqrt(jnp.mean(ckv * ckv, axis=-1, keepdims=True) + EPS) * kvn_ref[...]).astype(BF16)
    q2 = jnp.dot(ncq, wuq_ref[...], preferred_element_type=F32)
    kv = jnp.dot(nckv, wukv_ref[...], preferred_element_type=F32)

    cos_m = tm_ref[0]
    sin_m = tm_ref[1]
    q_scale = (MLA_NOPE + MLA_ROPE) ** -0.5 * LOG2E
    kr = y[:, _A_KR:_A_END]
    k_rope = kr * cos_m + _swap_pairs(kr, MLA_ROPE // 4) * sin_m
    nq = H_MLA * LANES
    for hd in range(H_MLA):
        sl = slice(hd * LANES, (hd + 1) * LANES)
        qh = q2[:, sl] * cos_m + _swap_pairs(q2[:, sl], MLA_ROPE // 4) * sin_m
        q_ref[0, :, sl] = (qh * q_scale).astype(BF16)
        kt_ref[0, sl, :] = (kv[:, sl] + k_rope).T.astype(BF16)
    lane = lax.broadcasted_iota(jnp.int32, (1, nq), 1)
    ones_half = jnp.where(((lane // MLA_V) % 2) != ((lane // LANES) % 2), 1.0, 0.0)
    v_ref[0] = (kv[:, nq:] + ones_half).astype(BF16)

    cos_r = tr_ref[0]
    sin_r = tr_ref[1]
    k_scale = RET_DK ** -0.5
    for hd in range(H_RET):
        sl = slice(hd * LANES, (hd + 1) * LANES)
        rq = y[:, _A_RQ + hd * LANES:_A_RQ + (hd + 1) * LANES]
        rk = y[:, _A_RK + hd * LANES:_A_RK + (hd + 1) * LANES]
        rq = rq * cos_r + _swap_pairs(rq, RET_DK // 4) * sin_r
        rk = rk * cos_r + _swap_pairs(rk, RET_DK // 4) * sin_r
        rq_ref[0, :, sl] = rq.astype(BF16)
        rkt_ref[0, sl, :] = (rk * k_scale).T.astype(BF16)
    rv_ref[0] = y[:, _A_RV:_A_RG].astype(BF16)
    rg_ref[0] = y[:, _A_RG:_A_KR].astype(BF16)


def _ab_in_proj(x_parts, modsel, w_ext, q_norm, kv_norm, wuq_ext, wukv_ext, tab_mla, tab_ret, nlt, t):
    b, d = modsel.shape[0], modsel.shape[-1]
    tm = ROW_TILE
    row = lambda bi, i: (bi, i, 0)
    col = lambda bi, i: (bi, 0, i)
    n_ret = H_RET * RET_DK
    widths = (H_MLA * LANES, H_MLA * LANES, H_MLA * LANES, n_ret, n_ret, n_ret, n_ret)
    transposed = (False, True, False, False, True, False, False)
    out_shapes = [jax.ShapeDtypeStruct((b, w, t) if tr else (b, t, w), BF16) for w, tr in zip(widths, transposed)]
    out_specs = [pl.BlockSpec((1, w, tm), col) if tr else pl.BlockSpec((1, tm, w), row)
                 for w, tr in zip(widths, transposed)]
    x_specs, x_args = _split_rows_specs(x_parts, nlt, tm)
    return pl.pallas_call(
        functools.partial(_ab_in_kernel, nlt=nlt),
        grid=(b, t // tm),
        in_specs=x_specs + [
                  pl.BlockSpec((1, 1, 6, d), lambda bi, i: (bi, (i >= nlt).astype(jnp.int32), 0, 0)),
                  _const_spec(w_ext.shape), _const_spec(q_norm.shape), _const_spec(kv_norm.shape),
                  _const_spec(wuq_ext.shape), _const_spec(wukv_ext.shape),
                  pl.BlockSpec((2, tm, LANES), lambda bi, i: (0, i, 0)),
                  pl.BlockSpec((2, tm, LANES), lambda bi, i: (0, i, 0))],
        out_specs=out_specs,
        out_shape=out_shapes,
        compiler_params=_cparams(("parallel", "arbitrary")),
        name="ab_in_proj",
    )(*x_args, modsel, w_ext, q_norm, kv_norm, wuq_ext, wukv_ext, tab_mla, tab_ret)


def _attn_kernel(q_ref, kt_ref, v_ref, o_ref, s_scr, m_scr, acc_scr, *, tk, n_lat, ctx_len, n_lat_chunks):
    tq = q_ref.shape[1]
    m_scr[...] = jnp.full(m_scr.shape, NEG, F32)
    acc_scr[...] = jnp.zeros_like(acc_scr)

    def scores(slot, r0, size):
        for hh in range(2):
            sl = slice(hh * LANES, (hh + 1) * LANES)
            s_scr[slot, hh, :, 0:size] = jnp.dot(q_ref[0, :, sl], kt_ref[0, sl, pl.ds(r0, size)],
                                                 preferred_element_type=F32)

    def accumulate(slot, r0, size):
        for hh in range(2):
            sl = slice(hh * LANES, (hh + 1) * LANES)
            s = s_scr[slot, hh, :, 0:size]
            m = m_scr[hh]
            m_new = jnp.maximum(m, jnp.max(s, axis=-1, keepdims=True))
            m_scr[hh] = m_new
            p = jnp.exp2((s - m_new[:, 0:1]).astype(BF16))
            acc_scr[hh] = jnp.exp2(m - m_new) * acc_scr[hh] \
                + jnp.dot(p, v_ref[0, pl.ds(r0, size), sl], preferred_element_type=F32)

    def lat_row(c):
        return pl.multiple_of(jnp.minimum(c, n_lat_chunks - 1) * tk, tk)

    scores(1, n_lat, ctx_len)
    if n_lat_chunks:
        scores(0, 0, tk)
    accumulate(1, n_lat, ctx_len)
    if n_lat_chunks:

        def body(j, carry):
            scores(1, lat_row(2 * j + 1), tk)
            accumulate(0, lat_row(2 * j), tk)
            scores(0, lat_row(2 * j + 2), tk)
            accumulate(1, lat_row(2 * j + 1), tk)
            return carry

        lax.fori_loop(0, n_lat_chunks // 2, body, 0)
    outs = [acc_scr[hh] / pltpu.roll(acc_scr[hh], MLA_V, axis=1) for hh in range(2)]
    lane = lax.broadcasted_iota(jnp.int32, (tq, LANES), 1)
    o_ref[0] = jnp.where(lane < MLA_V, outs[0], outs[1]).astype(BF16)


def _attention(q, kt, v, n_lat, for_ctx):
    b, t, _ = q.shape
    ctx_len = t - n_lat
    tk = ATT_KEY_CHUNK
    assert n_lat % (2 * tk) == 0 and n_lat % ATT_Q_TILE == 0 and ctx_len % ROW_TILE == 0
    tq, n_rows, row0 = (ROW_TILE, ctx_len, n_lat // ROW_TILE) if for_ctx else (ATT_Q_TILE, n_lat, 0)
    kern = functools.partial(_attn_kernel, tk=tk, n_lat=n_lat, ctx_len=ctx_len,
                             n_lat_chunks=0 if for_ctx else n_lat // tk)
    pair = 2 * LANES
    return pl.pallas_call(
        kern,
        grid=(b, H_MLA // 2, n_rows // tq),
        in_specs=[pl.BlockSpec((1, tq, pair), lambda bi, hp, i: (bi, i + row0, hp)),
                  pl.BlockSpec((1, pair, t), lambda bi, hp, i: (bi, hp, 0)),
                  pl.BlockSpec((1, t, pair), lambda bi, hp, i: (bi, 0, hp))],
        out_specs=pl.BlockSpec((1, tq, LANES), lambda bi, hp, i: (bi, i, hp)),
        out_shape=jax.ShapeDtypeStruct((b, n_rows, H_MLA * MLA_V), BF16),
        scratch_shapes=[pltpu.VMEM((2, 2, tq, max(tk, ctx_len)), F32), pltpu.VMEM((2, tq, LANES), F32),
                        pltpu.VMEM((2, tq, LANES), F32)],
        compiler_params=_cparams(("parallel", "parallel", "arbitrary")),
        name="mla_attention_ctx" if for_ctx else "mla_attention",
    )(q, kt, v)


def _ret_kernel(ld_ref, qf_ref, ktf_ref, vf_ref, qb_ref, ktb_ref, vb_ref, of_ref, ob_ref,
                state, dmat, dq, wk, gl, s_scr):
    c = pl.program_id(1)
    L = SCAN_BLOCK

    @pl.when(c == 0)
    def _():
        state[...] = jnp.zeros_like(state)
        row = lax.broadcasted_iota(jnp.int32, (L, L), 0).astype(F32)
        col = lax.broadcasted_iota(jnp.int32, (L, L), 1).astype(F32)
        for d in range(2):
            for hd in range(H_RET):
                lg = ld_ref[d, hd]
                if d == 0:
                    rel = row - col
                    dq[d, hd] = jnp.exp(lg * (row + 1.0))
                    wk[d, hd] = jnp.exp(lg * (L - 1.0 - col))
                else:
                    rel = col - row
                    dq[d, hd] = jnp.exp(lg * (L - row))
                    wk[d, hd] = jnp.exp(lg * col)
                dmat[d, hd] = jnp.where(rel >= 0, jnp.exp(lg * jnp.maximum(rel, 0.0)), 0.0)
                gl[d, hd] = jnp.exp(jnp.zeros((L, L), F32) + lg * L)

    streams = ((qf_ref, ktf_ref, vf_ref, of_ref), (qb_ref, ktb_ref, vb_ref, ob_ref))
    chains = [(d, hd, slice(hd * LANES, (hd + 1) * LANES)) + streams[d]
              for d in range(2) for hd in range(H_RET)]
    for d, hd, sl, q_ref, kt_ref, _, _ in chains:
        s = jnp.dot(q_ref[0, :, sl], kt_ref[0, sl, :], preferred_element_type=F32) * dmat[d, hd]
        s_scr[d, hd] = s.astype(BF16)
    for d, hd, sl, q_ref, _, v_ref, o_ref in chains:
        qd = (q_ref[0, :, sl].astype(F32) * dq[d, hd]).astype(BF16)
        o_ref[0, :, sl] = jnp.dot(s_scr[d, hd], v_ref[0, :, sl], preferred_element_type=F32) \
            + jnp.dot(qd, state[d, hd].astype(BF16), preferred_element_type=F32)
    for d, hd, sl, _, kt_ref, v_ref, _ in chains:
        kw = (kt_ref[0, sl, :].astype(F32) * wk[d, hd]).astype(BF16)
        state[d, hd] = state[d, hd] * gl[d, hd] + jnp.dot(kw, v_ref[0, :, sl], preferred_element_type=F32)


def _scan_maps(n_lat_blocks, n_ctx_blocks):
    nlb, ncb = n_lat_blocks, n_ctx_blocks
    fwd_blk = lambda c: jnp.where(c < ncb, nlb + c, c - ncb)
    bwd_blk = lambda c: nlb + ncb - 1 - c
    fwd = lambda bi, c: (bi, fwd_blk(c), 0)
    bwd = lambda bi, c: (bi, bwd_blk(c), 0)
    fwd_t = lambda bi, c: (bi, 0, fwd_blk(c))
    bwd_t = lambda bi, c: (bi, 0, bwd_blk(c))
    return fwd, bwd, fwd_t, bwd_t


def _retention(rq, rkt, rv, log_decay, n_lat):
    b, t, n = rq.shape
    L = SCAN_BLOCK
    nb = t // L
    fwd, bwd, fwd_t, bwd_t = _scan_maps(n_lat // L, nb - n_lat // L)
    tile = pl.BlockSpec((1, L, n), fwd)
    tile_b = pl.BlockSpec((1, L, n), bwd)
    out = jax.ShapeDtypeStruct((b, t, n), F32)
    sq = pltpu.VMEM((2, H_RET, L, L), F32)
    return pl.pallas_call(
        _ret_kernel,
        grid=(b, nb),
        in_specs=[pl.BlockSpec(memory_space=pltpu.SMEM),
                  tile, pl.BlockSpec((1, n, L), fwd_t), tile,
                  tile_b, pl.BlockSpec((1, n, L), bwd_t), tile_b],
        out_specs=[tile, tile_b],
        out_shape=[out, out],
        scratch_shapes=[sq, sq, sq, sq, sq, pltpu.VMEM((2, H_RET, L, L), BF16)],
        compiler_params=_cparams(("parallel", "arbitrary")),
        name="retention_scan",
    )(log_decay, rq, rkt, rv, rq, rkt, rv)


def _post_stages(xs, mixes, mod_ref, wo_ref, ln_ref, w1_ref, w2_ref, alpha):
    g1 = mod_ref[0, 0, 2:3, :]
    sh2 = mod_ref[0, 0, 3:4, :]
    sc2 = mod_ref[0, 0, 4:5, :]
    g2 = mod_ref[0, 0, 5:6, :]
    ys = [jnp.dot(m, wo_ref[...], preferred_element_type=F32) for m in mixes]
    x1s = [_layer_norm_rows(alpha * x + g1 * y, ln_ref[0:1, :], ln_ref[1:2, :]) for x, y in zip(xs, ys)]
    us = [jnp.dot((x1 * (1.0 + sc2) + sh2).astype(BF16), w1_ref[...], preferred_element_type=F32) for x1 in x1s]
    dff = w1_ref.shape[-1] // 2
    acts = [(_silu(u[:, dff:]) * u[:, :dff]).astype(BF16) for u in us]
    y2s = [jnp.dot(a, w2_ref[...], preferred_element_type=F32) for a in acts]
    return [_layer_norm_rows(alpha * x1 + g2 * y2, ln_ref[2:3, :], ln_ref[3:4, :]) for x1, y2 in zip(x1s, y2s)]


def _sub_tiles(o_ref):
    return [slice(r, r + ROW_TILE) for r in range(0, o_ref.shape[1], ROW_TILE)]


def _ab_post_kernel(x_ref, xc_ref, mod_ref, att_ref, attc_ref, rf_ref, rb_ref, rg_ref,
                    wo_ref, ln_ref, w1_ref, w2_ref, o_ref, *, alpha, nlt, tile0):
    subs = _sub_tiles(o_ref)
    mixes = []
    for rows in subs:
        ret = rf_ref[0, rows, :] + rb_ref[0, rows, :]
        gate = _silu(rg_ref[0, rows, :].astype(F32))
        parts = [_select_rows(nlt, att_ref, attc_ref, tile0, rows)]
        for hd, nh in enumerate(_head_norm_lanes(ret)):
            parts.append((gate[:, hd * LANES:(hd + 1) * LANES] * nh).astype(BF16))
        mixes.append(jnp.concatenate(parts, axis=-1))
    xs = [_select_rows(nlt, x_ref, xc_ref, tile0, rows) for rows in subs]
    for rows, out in zip(subs, _post_stages(xs, mixes, mod_ref, wo_ref, ln_ref, w1_ref, w2_ref, alpha)):
        o_ref[0, rows, :] = out


def _m_post_kernel(x_ref, xc_ref, mod_ref, hf_ref, hb_ref, og_ref, ng_ref,
                   wo_ref, ln_ref, w1_ref, w2_ref, o_ref, *, alpha, nlt, tile0):
    subs = _sub_tiles(o_ref)
    mixes = []
    for rows in subs:
        hs = hf_ref[0, rows, :] + hb_ref[0, rows, :]
        og = jax.nn.sigmoid(og_ref[0, rows, :].astype(F32))
        parts = []
        for hd, nh in enumerate(_head_norm_lanes(hs)):
            sl = slice(hd * LANES, (hd + 1) * LANES)
            parts.append((og[:, sl] * (nh * ng_ref[:, sl])).astype(BF16))
        mixes.append(jnp.concatenate(parts, axis=-1))
    xs = [_select_rows(nlt, x_ref, xc_ref, tile0, rows) for rows in subs]
    for rows, out in zip(subs, _post_stages(xs, mixes, mod_ref, wo_ref, ln_ref, w1_ref, w2_ref, alpha)):
        o_ref[0, rows, :] = out


def _post(kernel_fn, x_parts, modsel, acts, consts, w_out, ln, w1, w2, n_lat, tm, tile0, n_tiles, alpha):
    b, d = modsel.shape[0], modsel.shape[-1]
    assert n_lat % tm == 0 and tm % ROW_TILE == 0
    nlt = n_lat // tm
    row = lambda bi, i: (bi, i + tile0, 0)
    in_specs, args = _split_rows_specs(x_parts, nlt, tm, tile0)
    in_specs.append(pl.BlockSpec((1, 1, 6, d), lambda bi, i: (bi, (i + tile0 >= nlt).astype(jnp.int32), 0, 0)))
    args.append(modsel)
    for a in acts:
        if isinstance(a, tuple):
            sp, ar = _split_rows_specs(a, nlt, tm, tile0)
            in_specs += sp
            args += ar
        else:
            in_specs.append(pl.BlockSpec((1, tm, a.shape[-1]), row))
            args.append(a)
    weights = [*consts, w_out, ln, w1, w2]
    in_specs += [_const_spec(w.shape) for w in weights]
    return pl.pallas_call(
        functools.partial(kernel_fn, alpha=alpha, nlt=nlt, tile0=tile0),
        grid=(b, n_tiles),
        in_specs=in_specs,
        out_specs=pl.BlockSpec((1, tm, d), lambda bi, i: (bi, i, 0)),
        out_shape=jax.ShapeDtypeStruct((b, n_tiles * tm, d), F32),
        compiler_params=_cparams(("parallel", "arbitrary")),
        name="post_" + kernel_fn.__name__ + ("_ctx" if tile0 else ""),
    )(*args, *weights)


_M_QK, _M_V, _M_OG, _M_G, _M_END = 0, 1024, 2048, 3072, 3200


def _m_in_kernel(x_ref, xc_ref, xp_ref, xpc_ref, xn_ref, xnc_ref, mod_ref, w_ref, cw_ref, cb_ref, gb_ref,
                 qt_ref, k_ref, vt_ref, og_ref, gt1_ref, gt2_ref, gc_ref, *, nlt, seg_starts, seg_ends):
    i = pl.program_id(1)
    tm = x_ref.shape[1]
    shift = mod_ref[0, 0, 0:1, :]
    scale = mod_ref[0, 0, 1:2, :]
    h = (_select_rows(nlt, x_ref, xc_ref) * (1.0 + scale) + shift).astype(BF16)
    y = jnp.dot(h, w_ref[...], preferred_element_type=F32)
    hp = (_select_rows(nlt, xp_ref, xpc_ref) * (1.0 + scale) + shift).astype(BF16)
    hn = (_select_rows(nlt, xn_ref, xnc_ref) * (1.0 + scale) + shift).astype(BF16)
    wqk = w_ref[:, _M_QK:_M_V]
    up = jnp.dot(hp, wqk, preferred_element_type=F32)[7:8, :]
    un = jnp.dot(hn, wqk, preferred_element_type=F32)[0:1, :]
    is_start = functools.reduce(jnp.logical_or, [i == s for s in seg_starts])
    is_end = functools.reduce(jnp.logical_or, [i == e for e in seg_ends])
    up = jnp.where(is_start, 0.0, up)
    un = jnp.where(is_end, 0.0, un)

    u = y[:, _M_QK:_M_V]
    rows = lax.broadcasted_iota(jnp.int32, u.shape, 0)
    u_prev = jnp.where(rows == 0, up, pltpu.roll(u, 1, axis=0))
    u_next = jnp.where(rows == tm - 1, un, pltpu.roll(u, tm - 1, axis=0))
    qk = _silu(cw_ref[0:1, :] * u_prev + cw_ref[1:2, :] * u + cw_ref[2:3, :] * u_next + cb_ref[...])
    nq = H_M * M_DK
    for grp in range(nq // LANES):
        sl = slice(grp * LANES, (grp + 1) * LANES)
        qt_ref[0, sl, :] = qk[:, sl].T.astype(BF16)
    for hd in range(H_M):
        sl = slice(hd * LANES, (hd + 1) * LANES)
        vt_ref[0, sl, :] = y[:, _M_V + hd * LANES:_M_V + (hd + 1) * LANES].T.astype(BF16)
    k_ref[0] = (qk[:, nq:] * (M_DK ** -0.5)).astype(BF16)
    og_ref[0] = y[:, _M_OG:_M_G].astype(BF16)

    g = y[:, _M_G:_M_END] + gb_ref[...]
    lsg = jnp.minimum(g, 0.0) - jnp.log1p(jnp.exp(-jnp.abs(g)))
    r = lax.broadcasted_iota(jnp.int32, (tm, tm), 0)
    cidx = lax.broadcasted_iota(jnp.int32, (tm, tm), 1)
    same = (r // M_SCAN_CHUNK) == (cidx // M_SCAN_CHUNK)
    tril = jnp.where(jnp.logical_and(same, cidx <= r), 1.0, 0.0)
    triu = jnp.where(jnp.logical_and(same, cidx >= r), 1.0, 0.0)
    pre = jnp.dot(tril, lsg, precision=HIGHEST, preferred_element_type=F32)
    suf = jnp.dot(triu, lsg, precision=HIGHEST, preferred_element_type=F32)
    lane = lax.broadcasted_iota(jnp.int32, g.shape, 1)
    grp = lane // H_M
    gt1_ref[0] = jnp.where(grp == 1, pre, jnp.where(grp == 3, suf, g)).T
    bsum = jnp.where(grp == 1, pre, jnp.where(grp == 3, suf, 0.0))
    b_at_i = pltpu.roll(bsum, LANES - H_M, axis=1)
    r_gate = g - b_at_i
    pos = lax.broadcasted_iota(jnp.int32, g.shape, 0) % M_SCAN_CHUNK
    pmax = r_gate
    smax = r_gate
    step = 1
    while step < M_SCAN_CHUNK:
        pmax = jnp.where(pos >= step, jnp.maximum(pmax, pltpu.roll(pmax, step, axis=0)), pmax)
        smax = jnp.where(pos < M_SCAN_CHUNK - step, jnp.maximum(smax, pltpu.roll(smax, tm - step, axis=0)), smax)
        step *= 2
    gc_ref[0] = r_gate
    gt2_ref[0] = (b_at_i + jnp.where(grp == 0, pmax, smax)).T


def _m_in_proj(x_parts, modsel, w_ext, conv_w, conv_b, gate_b, nlt, t):
    b, d = modsel.shape[0], modsel.shape[-1]
    tm = ROW_TILE
    nt = t // tm
    r8 = tm // 8
    lat, ctx, ctx_tile0 = x_parts
    c0, n8_lat, n8_ctx = ctx_tile0 * r8, nlt * r8, (nt - nlt) * r8
    halo = lambda off, lo, n8: (lambda bi, i: (bi, lo + jnp.clip(i * r8 + off, 0, n8 - 1), 0))
    halo_ctx = lambda off: (lambda bi, i: (bi, c0 + jnp.clip((i - nlt) * r8 + off, 0, n8_ctx - 1), 0))
    x_specs, x_args = _split_rows_specs(x_parts, nlt, tm)
    x_specs += [pl.BlockSpec((1, 8, d), halo(-1, 0, n8_lat)), pl.BlockSpec((1, 8, d), halo_ctx(-1)),
                pl.BlockSpec((1, 8, d), halo(r8, 0, n8_lat)), pl.BlockSpec((1, 8, d), halo_ctx(r8))]
    x_args += [lat, ctx, lat, ctx]
    row = lambda bi, i: (bi, i, 0)
    col = lambda bi, i: (bi, 0, i)
    nk, n = H_M * M_DK, H_M * M_DV
    out_shapes = [jax.ShapeDtypeStruct((b, nk, t), BF16), jax.ShapeDtypeStruct((b, t, nk), BF16),
                  jax.ShapeDtypeStruct((b, n, t), BF16), jax.ShapeDtypeStruct((b, t, n), BF16)] + \
                 [jax.ShapeDtypeStruct((b, LANES, t), F32)] * 2 + [jax.ShapeDtypeStruct((b, t, LANES), F32)]
    out_specs = [pl.BlockSpec((1, nk, tm), col), pl.BlockSpec((1, tm, nk), row),
                 pl.BlockSpec((1, n, tm), col), pl.BlockSpec((1, tm, n), row)] + \
                [pl.BlockSpec((1, LANES, tm), col)] * 2 + [pl.BlockSpec((1, tm, LANES), row)]
    kern = functools.partial(_m_in_kernel, nlt=nlt, seg_starts=(0, nlt), seg_ends=(nlt - 1, nt - 1))
    return pl.pallas_call(
        kern,
        grid=(b, nt),
        in_specs=x_specs + [
                  pl.BlockSpec((1, 1, 6, d), lambda bi, i: (bi, (i >= nlt).astype(jnp.int32), 0, 0)),
                  _const_spec(w_ext.shape), _const_spec(conv_w.shape), _const_spec(conv_b.shape),
                  _const_spec(gate_b.shape)],
        out_specs=out_specs,
        out_shape=out_shapes,
        compiler_params=_cparams(("parallel", "arbitrary")),
        name="m_in_proj",
    )(*x_args, modsel, w_ext, conv_w, conv_b, gate_b)


def _mlstm_gates(gc, gt1, gt2, hd, backward, m_prev):
    L = M_SCAN_CHUNK
    gi = 2 * H_M if backward else 0
    i_row = gt1[gi + hd:gi + hd + 1, :]
    b_row = gt1[gi + H_M + hd:gi + H_M + hd + 1, :]
    mloc_row = gt2[gi + hd:gi + hd + 1, :]
    r_col = gc[:, gi + hd:gi + hd + 1]
    inter = b_row + m_prev
    m_t = jnp.maximum(inter, mloc_row)
    e = 0 if backward else L - 1
    b_end = b_row[:, e:e + 1]
    m_new = jnp.maximum(b_end + m_prev, mloc_row[:, e:e + 1])
    return i_row, b_row, r_col, inter, m_t, b_end, m_new


def _mlstm_kernel(*refs):
    ins, (hf_ref, hb_ref, cstate, mstate, sc_scr) = refs[:12], refs[12:]
    c = pl.program_id(1)
    L = M_SCAN_CHUNK

    @pl.when(c == 0)
    def _():
        cstate[...] = jnp.zeros_like(cstate)
        mstate[...] = jnp.zeros_like(mstate)

    chains = [(d, hd) for d in range(2) for hd in range(H_M)]

    def operands(d, hd):
        k_ref, qt_ref, vt_ref, gc_ref, gt1_ref, gt2_ref = ins[6 * d:6 * d + 6]
        pair = slice((hd // 2) * LANES, (hd // 2 + 1) * LANES)
        k_pair = k_ref[0, :, pair]
        lane = lax.broadcasted_iota(jnp.int32, k_pair.shape, 1)
        k_own = jnp.where((lane // M_DK) == (hd % 2), k_pair, jnp.zeros_like(k_pair))
        gates = _mlstm_gates(gc_ref[0], gt1_ref[0], gt2_ref[0], hd, d == 1, mstate[d * H_M + hd][0:1, 0:1])
        return k_own, qt_ref[0, pair, :], vt_ref[0, hd * LANES:(hd + 1) * LANES, :], gates

    ones_rows = jnp.ones((M_ONES_ROWS, L), BF16)
    s_idx = lax.broadcasted_iota(jnp.int32, (L, L), 0)
    t_idx = lax.broadcasted_iota(jnp.int32, (L, L), 1)

    for ch, (d, hd) in enumerate(chains):
        k_own, qt, _, (_, b_row, r_col, _, m_t, _, _) = operands(d, hd)
        visible = (s_idx >= t_idx) if d == 1 else (s_idx <= t_idx)
        dt = jnp.exp(jnp.where(visible, r_col + (b_row - m_t), NEG))
        sc_scr[ch] = (jnp.dot(k_own, qt, preferred_element_type=F32) * dt).astype(BF16)

    for ch, (d, hd) in enumerate(chains):
        _, qt, vt, (_, _, _, inter, m_t, _, _) = operands(d, hd)
        vext_t = jnp.concatenate([vt, ones_rows], axis=0)
        nd_t = jnp.dot(vext_t, sc_scr[ch], preferred_element_type=F32) \
            + jnp.dot(cstate[ch].astype(BF16), qt, preferred_element_type=F32) * jnp.exp(inter - m_t)
        den = jnp.broadcast_to(nd_t[LANES:LANES + 1, :], (LANES, L))
        h_t = nd_t[:LANES] / jnp.maximum(jnp.abs(den), jnp.exp(-m_t))
        (hb_ref if d else hf_ref)[0, :, hd * LANES:(hd + 1) * LANES] = h_t.T

    for ch, (d, hd) in enumerate(chains):
        k_own, _, vt, (i_row, b_row, _, _, _, b_end, m_new) = operands(d, hd)
        m_prev = mstate[ch][0:1, 0:1]
        vext_t = jnp.concatenate([vt, ones_rows], axis=0)
        w_row = jnp.exp(b_end - b_row + i_row - m_new)
        cstate[ch] = jnp.exp(b_end + m_prev - m_new) * cstate[ch] \
            + jnp.dot((vext_t.astype(F32) * w_row).astype(BF16), k_own, preferred_element_type=F32)
        mstate[ch] = jnp.broadcast_to(m_new, mstate.shape[1:])


def _mlstm(k, qt, vt, gc, gt1, gt2, n_lat):
    b, t, nk = k.shape
    n = vt.shape[1]
    L = SCAN_BLOCK
    nb = t // L
    fwd, bwd, fwd_t, bwd_t = _scan_maps(n_lat // L, nb - n_lat // L)

    def specs(rmap, tmap):
        return [pl.BlockSpec((1, L, nk), rmap), pl.BlockSpec((1, nk, L), tmap), pl.BlockSpec((1, n, L), tmap),
                pl.BlockSpec((1, L, LANES), rmap), pl.BlockSpec((1, 4 * H_M, L), tmap),
                pl.BlockSpec((1, 4 * H_M, L), tmap)]

    out = jax.ShapeDtypeStruct((b, t, H_M * M_DV), F32)
    args = (k, qt, vt, gc, gt1, gt2)
    return pl.pallas_call(
        _mlstm_kernel,
        grid=(b, nb),
        in_specs=specs(fwd, fwd_t) + specs(bwd, bwd_t),
        out_specs=[pl.BlockSpec((1, L, H_M * M_DV), fwd), pl.BlockSpec((1, L, H_M * M_DV), bwd)],
        out_shape=[out, out],
        scratch_shapes=[pltpu.VMEM((2 * H_M, LANES + M_ONES_ROWS, LANES), F32),
                        pltpu.VMEM((2 * H_M, 8, LANES), F32),
                        pltpu.VMEM((2 * H_M, L, L), BF16)],
        compiler_params=_cparams(("parallel", "arbitrary")),
        name="mlstm_scan",
    )(*args, *args)


def _rope_tables(n_lat, ctx_len, d, lane0):
    rows = n_lat // GRID_W
    row = np.repeat(np.arange(rows, dtype=np.float64), GRID_W)
    col = np.tile(np.arange(GRID_W, dtype=np.float64), rows)
    da = d // 2
    inv = ROPE_BASE ** (-np.arange(0, da, 2, dtype=np.float64) / da)
    ar = row[:, None] * inv
    ac = col[:, None] * inv
    tab = np.zeros((2, n_lat + ctx_len, LANES), np.float32)
    tab[0] = 1.0
    tab[0, :n_lat, lane0:lane0 + d] = np.concatenate([np.cos(ar), np.cos(ar), np.cos(ac), np.cos(ac)], axis=-1)
    tab[1, :n_lat, lane0:lane0 + d] = np.concatenate([-np.sin(ar), np.sin(ar), -np.sin(ac), np.sin(ac)], axis=-1)
    return jnp.asarray(tab)


def _pad_heads(w, n_heads, width):
    k = w.shape[0]
    w = w.reshape(k, n_heads, width)
    return jnp.pad(w, ((0, 0), (0, 0), (0, LANES - width))).reshape(k, n_heads * LANES)


def _prep_ab_weights(w_in, w_uq, w_ukv):
    o1 = MLA_Q_LORA
    o2 = o1 + MLA_KV_LORA
    o3 = o2 + MLA_ROPE
    o4 = o3 + H_RET * RET_DK
    o5 = o4 + H_RET * RET_DK
    o6 = o5 + H_RET * RET_DV
    cq, ckv, kr, rq, rk, rv, rg = (w_in[:, :o1], w_in[:, o1:o2], w_in[:, o2:o3], w_in[:, o3:o4],
                                   w_in[:, o4:o5], w_in[:, o5:o6], w_in[:, o6:])
    kr_blk = jnp.pad(kr, ((0, 0), (MLA_NOPE, LANES - MLA_NOPE - MLA_ROPE)))
    w_ext = jnp.concatenate([cq, ckv, rq, rk, rv, rg, kr_blk], axis=1)

    wuq_ext = _pad_heads(w_uq, H_MLA, MLA_NOPE + MLA_ROPE)
    ukv = w_ukv.reshape(-1, H_MLA, MLA_NOPE + MLA_V)
    k_nope = _pad_heads(ukv[..., :MLA_NOPE].reshape(-1, H_MLA * MLA_NOPE), H_MLA, MLA_NOPE)
    v_w = ukv[..., MLA_NOPE:]
    zeros = jnp.zeros_like(v_w)
    even = (np.arange(H_MLA) % 2 == 0)[None, :, None]
    v_cols = jnp.concatenate([jnp.where(even, v_w, zeros), jnp.where(even, zeros, v_w)], axis=-1)
    wukv_ext = jnp.concatenate([k_nope, v_cols.reshape(-1, H_MLA * LANES)], axis=1)
    return w_ext.astype(BF16), wuq_ext.astype(BF16), wukv_ext.astype(BF16)


def _prep_m_weights(w_in, conv_w, conv_b, gate_b):
    o1 = 2 * H_M * M_DK
    o2 = o1 + H_M * M_DV
    o3 = o2 + H_M * M_DV
    g_pad = jnp.pad(w_in[:, o3:], ((0, 0), (0, LANES - 4 * H_M)))
    w_ext = jnp.concatenate([w_in[:, :o3], g_pad], axis=1).astype(BF16)
    gb = jnp.pad(gate_b.reshape(1, 4 * H_M), ((0, 0), (0, LANES - 4 * H_M)))
    return w_ext, conv_w, conv_b[None, :], gb


def kernel(x, c, ctx, c_ctx, mod_w, mod_b, ln_g, ln_b, ffn_w_in, ffn_w_out, ab_w_in, mla_q_norm, mla_w_uq,
           mla_kv_norm, mla_w_ukv, ret_log_decay, ab_w_out, m_w_in, m_conv_w, m_conv_b, m_gate_b, m_norm_g, m_w_out):
    b, s, d = x.shape
    ctx_len = ctx.shape[1]
    depth = mod_w.shape[0]
    assert ctx_len % ROW_TILE == 0 and s % ROW_TILE == 0 and s % GRID_W == 0 and b + 1 <= 8
    nlt = s // ROW_TILE
    alpha = (2 * depth) ** 0.25

    c_rows = jnp.zeros((8, d), F32).at[:b].set(c).at[b].set(c_ctx)
    mods = _modulation(c_rows, mod_w, mod_b).reshape(depth, 8, 6, d)
    tab_mla = _rope_tables(s, ctx_len, MLA_ROPE, MLA_NOPE)
    tab_ret = _rope_tables(s, ctx_len, RET_DK, 0)

    t = s + ctx_len
    x_lat, x_ctx = x, ctx
    for l in range(depth):
        last = l == depth - 1
        j = l // 2
        modsel = jnp.stack([mods[l, :b], jnp.broadcast_to(mods[l, b], (b, 6, d))], axis=1)
        ln = jnp.stack([ln_g[l, 0], ln_b[l, 0], ln_g[l, 1], ln_b[l, 1]])
        w1 = ffn_w_in[l].astype(BF16)
        w2 = ffn_w_out[l].astype(BF16)
        x_parts = (x_lat, x_ctx, 0)
        if l % 2 == 0:
            w_ext, wuq_ext, wukv_ext = _prep_ab_weights(ab_w_in[j], mla_w_uq[j], mla_w_ukv[j])
            q, kt, v, rq, rkt, rv, rg = _ab_in_proj(x_parts, modsel, w_ext, mla_q_norm[j][None, :],
                                                    mla_kv_norm[j][None, :], wuq_ext, wukv_ext,
                                                    tab_mla, tab_ret, nlt, t)
            att_lat = _attention(q, kt, v, s, False)
            att_ctx = None if last else _attention(q, kt, v, s, True)
            ret_f, ret_b = _retention(rq, rkt, rv, ret_log_decay[j], s)
            post_fn, acts, consts, w_out = _ab_post_kernel, [ret_f, ret_b, rg], [], ab_w_out[j]
        else:
            w_ext, cw, cb, gb = _prep_m_weights(m_w_in[j], m_conv_w[j], m_conv_b[j], m_gate_b[j])
            qt, k, vt, og, gt1, gt2, gc = _m_in_proj(x_parts, modsel, w_ext, cw, cb, gb, nlt, t)
            hf, hb = _mlstm(k, qt, vt, gc, gt1, gt2, s)
            post_fn, acts, consts, w_out = _m_post_kernel, [hf, hb, og], [m_norm_g[j][None, :]], m_w_out[j]
            att_lat = att_ctx = None

        def post(tm, tile0, n_tiles, for_ctx):
            pair = (lambda a_lat, a_ctx: (a_lat, a_ctx, 0) if for_ctx else (a_lat, a_lat, 0))
            att = [] if att_lat is None else [pair(att_lat, att_ctx)]
            return _post(post_fn, pair(x_lat, x_ctx), modsel, att + acts, consts, w_out.astype(BF16),
                         ln, w1, w2, s, tm, tile0, n_tiles, alpha)

        new_lat = post(POST_TILE, 0, s // POST_TILE, False)
        if not last:
            x_ctx = post(ROW_TILE, nlt, ctx_len // ROW_TILE, True)
        x_lat = new_lat
    return x_lat
```

```python
import functools

import jax
import jax.numpy as jnp
import numpy as np
from jax import lax
from jax.experimental import pallas as pl
from jax.experimental.pallas import tpu as pltpu

F32 = jnp.float32
BF16 = jnp.bfloat16
HIGHEST = lax.Precision.HIGHEST

GRID_W = 64
ROPE_BASE = 10000.0
EPS = 1e-5
NEG = -1e30
LOG2E = 1.4426950408889634
H_MLA, MLA_NOPE, MLA_ROPE, MLA_V = 8, 64, 32, 64
MLA_Q_LORA, MLA_KV_LORA = 384, 256
H_RET, RET_DK, RET_DV = 4, 128, 128
H_M, M_DK, M_DV = 8, 64, 128

LANES = 128
ROW_TILE = 256
POST_TILE = 512
RET_BLOCK = 128
ATT_KEY_CHUNK = 512
ATT_Q_TILE = 1024
M_SCAN_CHUNK = 256
M_ONES_ROWS = 16
VMEM_LIMIT = 56 * 1024 * 1024


def _cparams(sem, flags=None):
    return pltpu.CompilerParams(dimension_semantics=sem, vmem_limit_bytes=VMEM_LIMIT, flags=flags)


def _const_spec(shape):
    nd = len(shape)
    return pl.BlockSpec(shape, lambda *_: (0,) * nd, pipeline_mode=pl.Buffered(1))


def _split_rows_specs(parts, nlt, tm, tile0=0):
    lat, ctx, ctx_tile0 = parts
    n = lat.shape[-1]
    specs = [pl.BlockSpec((1, tm, n), lambda bi, i: (bi, jnp.minimum(i + tile0, nlt - 1), 0)),
             pl.BlockSpec((1, tm, n), lambda bi, i: (bi, ctx_tile0 + jnp.maximum(i + tile0 - nlt, 0), 0))]
    return specs, [lat, ctx]


def _select_rows(nlt, lat_ref, ctx_ref, tile0=0, rows=slice(None)):
    return jnp.where(pl.program_id(1) + tile0 >= nlt, ctx_ref[0, rows, :], lat_ref[0, rows, :])


def _silu(v):
    return v * jax.nn.sigmoid(v)


def _layer_norm_rows(v, g, b):
    mu = jnp.mean(v, axis=-1, keepdims=True)
    d = v - mu
    var = jnp.mean(d * d, axis=-1, keepdims=True)
    return d * lax.rsqrt(var + EPS) * g + b


def _head_norm_lanes(v):
    outs = []
    for h in range(v.shape[-1] // LANES):
        blk = v[:, h * LANES:(h + 1) * LANES]
        mu = jnp.mean(blk, axis=-1, keepdims=True)
        d = blk - mu
        var = jnp.mean(d * d, axis=-1, keepdims=True)
        outs.append(d * lax.rsqrt(var + EPS))
    return outs


def _mod_kernel(c_ref, w_ref, b_ref, o_ref):
    sc = _silu(c_ref[...])
    o_ref[0] = jnp.dot(sc, w_ref[0], precision=HIGHEST, preferred_element_type=F32) + b_ref[0]


def _modulation(c_rows, mod_w, mod_b):
    depth, d, n = mod_w.shape
    tn = 1536
    return pl.pallas_call(
        _mod_kernel,
        grid=(depth, n // tn),
        in_specs=[pl.BlockSpec((8, d), lambda l, j: (0, 0)),
                  pl.BlockSpec((1, d, tn), lambda l, j: (l, 0, j)),
                  pl.BlockSpec((1, 1, tn), lambda l, j: (l, 0, j))],
        out_specs=pl.BlockSpec((1, 8, tn), lambda l, j: (l, 0, j)),
        out_shape=jax.ShapeDtypeStruct((depth, 8, n), F32),
        compiler_params=_cparams(("arbitrary", "arbitrary")),
        name="modulation",
    )(c_rows, mod_w, mod_b.reshape(depth, 1, n))


_A_CQ, _A_CKV, _A_RQ, _A_RK, _A_RV, _A_RG, _A_KR, _A_END = 0, 384, 640, 1152, 1664, 2176, 2688, 2816


def _swap_pairs(v, width):
    lane = lax.broadcasted_iota(jnp.int32, v.shape, 1)
    return jnp.where((lane // width) % 2 == 0, pltpu.roll(v, LANES - width, axis=1), pltpu.roll(v, width, axis=1))


def _ab_in_kernel(x_ref, xc_ref, mod_ref, w_ref, qn_ref, kvn_ref, wuq_ref, wukv_ref, tm_ref, tr_ref,
                  q_ref, kt_ref, v_ref, rq_ref, rkt_ref, rv_ref, rg_ref, *, nlt):
    x = _select_rows(nlt, x_ref, xc_ref)
    shift = mod_ref[0, 0, 0:1, :]
    scale = mod_ref[0, 0, 1:2, :]
    h = (x * (1.0 + scale) + shift).astype(BF16)
    y = jnp.dot(h, w_ref[...], preferred_element_type=F32)

    cq = y[:, _A_CQ:_A_CKV]
    ckv = y[:, _A_CKV:_A_RQ]
    ncq = (cq * lax.rsqrt(jnp.mean(cq * cq, axis=-1, keepdims=True) + EPS) * qn_ref[...]).astype(BF16)
    nckv = (ckv * lax.rsqrt(jnp.mean(ckv * ckv, axis=-1, keepdims=True) + EPS) * kvn_ref[...]).astype(BF16)
    q2 = jnp.dot(ncq, wuq_ref[...], preferred_element_type=F32)
    kv = jnp.dot(nckv, wukv_ref[...], preferred_element_type=F32)

    cos_m = tm_ref[0]
    sin_m = tm_ref[1]
    q_scale = (MLA_NOPE + MLA_ROPE) ** -0.5 * LOG2E
    kr = y[:, _A_KR:_A_END]
    k_rope = kr * cos_m + _swap_pairs(kr, MLA_ROPE // 4) * sin_m
    nq = H_MLA * LANES
    for hd in range(H_MLA):
        sl = slice(hd * LANES, (hd + 1) * LANES)
        qh = q2[:, sl] * cos_m + _swap_pairs(q2[:, sl], MLA_ROPE // 4) * sin_m
        q_ref[0, :, sl] = (qh * q_scale).astype(BF16)
        kt_ref[0, sl, :] = (kv[:, sl] + k_rope).T.astype(BF16)
    lane = lax.broadcasted_iota(jnp.int32, (1, nq), 1)
    ones_half = jnp.where(((lane // MLA_V) % 2) != ((lane // LANES) % 2), 1.0, 0.0)
    v_ref[0] = (kv[:, nq:] + ones_half).astype(BF16)

    cos_r = tr_ref[0]
    sin_r = tr_ref[1]
    k_scale = RET_DK ** -0.5
    for hd in range(H_RET):
        sl = slice(hd * LANES, (hd + 1) * LANES)
        rq = y[:, _A_RQ + hd * LANES:_A_RQ + (hd + 1) * LANES]
        rk = y[:, _A_RK + hd * LANES:_A_RK + (hd + 1) * LANES]
        rq = rq * cos_r + _swap_pairs(rq, RET_DK // 4) * sin_r
        rk = rk * cos_r + _swap_pairs(rk, RET_DK // 4) * sin_r
        rq_ref[0, :, sl] = rq.astype(BF16)
        rkt_ref[0, sl, :] = (rk * k_scale).T.astype(BF16)
    rv_ref[0] = y[:, _A_RV:_A_RG].astype(BF16)
    rg_ref[0] = y[:, _A_RG:_A_KR].astype(BF16)


def _ab_in_proj(x_parts, modsel, w_ext, q_norm, kv_norm, wuq_ext, wukv_ext, tab_mla, tab_ret, nlt, t):
    b, d = modsel.shape[0], modsel.shape[-1]
    tm = ROW_TILE
    row = lambda bi, i: (bi, i, 0)
    col = lambda bi, i: (bi, 0, i)
    n_ret = H_RET * RET_DK
    widths = (H_MLA * LANES, H_MLA * LANES, H_MLA * LANES, n_ret, n_ret, n_ret, n_ret)
    transposed = (False, True, False, False, True, False, False)
    out_shapes = [jax.ShapeDtypeStruct((b, w, t) if tr else (b, t, w), BF16) for w, tr in zip(widths, transposed)]
    out_specs = [pl.BlockSpec((1, w, tm), col) if tr else pl.BlockSpec((1, tm, w), row)
                 for w, tr in zip(widths, transposed)]
    x_specs, x_args = _split_rows_specs(x_parts, nlt, tm)
    return pl.pallas_call(
        functools.partial(_ab_in_kernel, nlt=nlt),
        grid=(b, t // tm),
        in_specs=x_specs + [
                  pl.BlockSpec((1, 1, 6, d), lambda bi, i: (bi, (i >= nlt).astype(jnp.int32), 0, 0)),
                  _const_spec(w_ext.shape), _const_spec(q_norm.shape), _const_spec(kv_norm.shape),
                  _const_spec(wuq_ext.shape), _const_spec(wukv_ext.shape),
                  pl.BlockSpec((2, tm, LANES), lambda bi, i: (0, i, 0)),
                  pl.BlockSpec((2, tm, LANES), lambda bi, i: (0, i, 0))],
        out_specs=out_specs,
        out_shape=out_shapes,
        compiler_params=_cparams(("parallel", "arbitrary")),
        name="ab_in_proj",
    )(*x_args, modsel, w_ext, q_norm, kv_norm, wuq_ext, wukv_ext, tab_mla, tab_ret)


def _attn_kernel(q_ref, kt_ref, v_ref, o_ref, s_scr, m_scr, acc_scr, *, tk, ctx_key0, ctx_len, n_lat_chunks):
    tq = q_ref.shape[1]
    m_scr[...] = jnp.full(m_scr.shape, NEG, F32)
    acc_scr[...] = jnp.zeros_like(acc_scr)

    def scores(slot, r0, size):
        for hh in range(2):
            sl = slice(hh * LANES, (hh + 1) * LANES)
            s_scr[slot, hh, :, 0:size] = jnp.dot(q_ref[0, :, sl], kt_ref[0, sl, pl.ds(r0, size)],
                                                 preferred_element_type=F32)

    def accumulate(slot, r0, size):
        for hh in range(2):
            sl = slice(hh * LANES, (hh + 1) * LANES)
            s = s_scr[slot, hh, :, 0:size]
            m = m_scr[hh]
            m_new = jnp.maximum(m, jnp.max(s, axis=-1, keepdims=True))
            m_scr[hh] = m_new
            p = jnp.exp2(s - m_new[:, 0:1]).astype(BF16)
            acc_scr[hh] = jnp.exp2(m - m_new) * acc_scr[hh] \
                + jnp.dot(p, v_ref[0, pl.ds(r0, size), sl], preferred_element_type=F32)

    def lat_row(c):
        return pl.multiple_of(jnp.minimum(c, n_lat_chunks - 1) * tk, tk)

    scores(1, ctx_key0, ctx_len)
    if n_lat_chunks:
        scores(0, 0, tk)
    accumulate(1, ctx_key0, ctx_len)
    if n_lat_chunks:

        def body(j, carry):
            scores(1, lat_row(2 * j + 1), tk)
            accumulate(0, lat_row(2 * j), tk)
            scores(0, lat_row(2 * j + 2), tk)
            accumulate(1, lat_row(2 * j + 1), tk)
            return carry

        lax.fori_loop(0, n_lat_chunks // 2, body, 0)
    outs = [acc_scr[hh] / pltpu.roll(acc_scr[hh], MLA_V, axis=1) for hh in range(2)]
    lane = lax.broadcasted_iota(jnp.int32, (tq, LANES), 1)
    o_ref[0] = jnp.where(lane < MLA_V, outs[0], outs[1]).astype(BF16)


def _attention(q, kt, v, n_lat, for_ctx):
    b, t, _ = q.shape
    ctx_len = t - n_lat
    tk = ATT_KEY_CHUNK
    assert n_lat % (2 * tk) == 0 and n_lat % ATT_Q_TILE == 0 and ctx_len % ROW_TILE == 0
    tq, n_rows, row0 = (ROW_TILE, ctx_len, n_lat // ROW_TILE) if for_ctx else (ATT_Q_TILE, n_lat, 0)
    pair = 2 * LANES
    if for_ctx:
        assert n_lat % ctx_len == 0
        key_blk, n_keys, key0 = n_lat // ctx_len, ctx_len, 0
    else:
        key_blk, n_keys, key0 = 0, t, n_lat
    kern = functools.partial(_attn_kernel, tk=tk, ctx_key0=key0, ctx_len=ctx_len,
                             n_lat_chunks=0 if for_ctx else n_lat // tk)
    return pl.pallas_call(
        kern,
        grid=(b, H_MLA // 2, n_rows // tq),
        in_specs=[pl.BlockSpec((1, tq, pair), lambda bi, hp, i: (bi, i + row0, hp)),
                  pl.BlockSpec((1, pair, n_keys), lambda bi, hp, i: (bi, hp, key_blk)),
                  pl.BlockSpec((1, n_keys, pair), lambda bi, hp, i: (bi, key_blk, hp))],
        out_specs=pl.BlockSpec((1, tq, LANES), lambda bi, hp, i: (bi, i, hp)),
        out_shape=jax.ShapeDtypeStruct((b, n_rows, H_MLA * MLA_V), BF16),
        scratch_shapes=[pltpu.VMEM((2, 2, tq, max(tk, ctx_len)), F32), pltpu.VMEM((2, tq, LANES), F32),
                        pltpu.VMEM((2, tq, LANES), F32)],
        compiler_params=_cparams(("parallel", "parallel", "arbitrary")),
        name="mla_attention_ctx" if for_ctx else "mla_attention",
    )(q, kt, v)


def _ret_kernel(ld_ref, qf_ref, ktf_ref, vf_ref, qb_ref, ktb_ref, vb_ref, of_ref, ob_ref,
                state, dmat, dq, wk, gl, s_scr):
    c = pl.program_id(1)
    L = RET_BLOCK

    @pl.when(c == 0)
    def _():
        state[...] = jnp.zeros_like(state)
        row = lax.broadcasted_iota(jnp.int32, (L, L), 0).astype(F32)
        col = lax.broadcasted_iota(jnp.int32, (L, L), 1).astype(F32)
        qpos = lax.broadcasted_iota(jnp.int32, (L, RET_DK), 0).astype(F32)
        kpos = lax.broadcasted_iota(jnp.int32, (RET_DK, L), 1).astype(F32)
        for d in range(2):
            for hd in range(H_RET):
                lg = ld_ref[d, hd]
                if d == 0:
                    rel = row - col
                    dq[d, hd] = jnp.exp(lg * (qpos + 1.0))
                    wk[d, hd] = jnp.exp(lg * (L - 1.0 - kpos))
                else:
                    rel = col - row
                    dq[d, hd] = jnp.exp(lg * (L - qpos))
                    wk[d, hd] = jnp.exp(lg * kpos)
                dmat[d, hd] = jnp.where(rel >= 0, jnp.exp(lg * jnp.maximum(rel, 0.0)), 0.0)
                gl[d, hd] = jnp.exp(jnp.zeros((RET_DK, RET_DV), F32) + lg * L)

    streams = ((qf_ref, ktf_ref, vf_ref, of_ref), (qb_ref, ktb_ref, vb_ref, ob_ref))
    chains = [(d, hd, slice(hd * LANES, (hd + 1) * LANES)) + streams[d]
              for d in range(2) for hd in range(H_RET)]
    for d, hd, sl, q_ref, kt_ref, _, _ in chains:
        s = jnp.dot(q_ref[0, :, sl], kt_ref[0, sl, :], preferred_element_type=F32) * dmat[d, hd]
        s_scr[d, hd] = s.astype(BF16)
    for d, hd, sl, q_ref, _, v_ref, o_ref in chains:
        qd = (q_ref[0, :, sl].astype(F32) * dq[d, hd]).astype(BF16)
        o_ref[0, :, sl] = jnp.dot(s_scr[d, hd], v_ref[0, :, sl], preferred_element_type=F32) \
            + jnp.dot(qd, state[d, hd].astype(BF16), preferred_element_type=F32)
    for d, hd, sl, _, kt_ref, v_ref, _ in chains:
        kw = (kt_ref[0, sl, :].astype(F32) * wk[d, hd]).astype(BF16)
        state[d, hd] = state[d, hd] * gl[d, hd] + jnp.dot(kw, v_ref[0, :, sl], preferred_element_type=F32)


def _scan_maps(n_lat_blocks, n_ctx_blocks):
    nlb, ncb = n_lat_blocks, n_ctx_blocks
    fwd_blk = lambda c: jnp.where(c < ncb, nlb + c, c - ncb)
    bwd_blk = lambda c: nlb + ncb - 1 - c
    fwd = lambda bi, c: (bi, fwd_blk(c), 0)
    bwd = lambda bi, c: (bi, bwd_blk(c), 0)
    fwd_t = lambda bi, c: (bi, 0, fwd_blk(c))
    bwd_t = lambda bi, c: (bi, 0, bwd_blk(c))
    return fwd, bwd, fwd_t, bwd_t


def _retention(rq, rkt, rv, log_decay, n_lat):
    b, t, n = rq.shape
    L = RET_BLOCK
    nb = t // L
    fwd, bwd, fwd_t, bwd_t = _scan_maps(n_lat // L, nb - n_lat // L)
    tile = pl.BlockSpec((1, L, n), fwd)
    tile_b = pl.BlockSpec((1, L, n), bwd)
    out = jax.ShapeDtypeStruct((b, t, n), F32)
    per_head = lambda rows, cols, dtype=F32: pltpu.VMEM((2, H_RET, rows, cols), dtype)
    scratch = [per_head(RET_DK, RET_DV),
               per_head(L, L),
               per_head(L, RET_DK),
               per_head(RET_DK, L),
               per_head(RET_DK, RET_DV),
               per_head(L, L, BF16)]
    return pl.pallas_call(
        _ret_kernel,
        grid=(b, nb),
        in_specs=[pl.BlockSpec(memory_space=pltpu.SMEM),
                  tile, pl.BlockSpec((1, n, L), fwd_t), tile,
                  tile_b, pl.BlockSpec((1, n, L), bwd_t), tile_b],
        out_specs=[tile, tile_b],
        out_shape=[out, out],
        scratch_shapes=scratch,
        compiler_params=_cparams(("parallel", "arbitrary")),
        name="retention_scan",
    )(log_decay, rq, rkt, rv, rq, rkt, rv)


def _post_stages(xs, mixes, mod_ref, wo_ref, ln_ref, w1_ref, w2_ref, alpha):
    g1 = mod_ref[0, 0, 2:3, :]
    sh2 = mod_ref[0, 0, 3:4, :]
    sc2 = mod_ref[0, 0, 4:5, :]
    g2 = mod_ref[0, 0, 5:6, :]
    ys = [jnp.dot(m, wo_ref[...], preferred_element_type=F32) for m in mixes]
    x1s = [_layer_norm_rows(alpha * x + g1 * y, ln_ref[0:1, :], ln_ref[1:2, :]) for x, y in zip(xs, ys)]
    us = [jnp.dot((x1 * (1.0 + sc2) + sh2).astype(BF16), w1_ref[...], preferred_element_type=F32) for x1 in x1s]
    dff = w1_ref.shape[-1] // 2
    acts = [(_silu(u[:, dff:]) * u[:, :dff]).astype(BF16) for u in us]
    y2s = [jnp.dot(a, w2_ref[...], preferred_element_type=F32) for a in acts]
    return [_layer_norm_rows(alpha * x1 + g2 * y2, ln_ref[2:3, :], ln_ref[3:4, :]) for x1, y2 in zip(x1s, y2s)]


def _sub_tiles(o_ref):
    return [slice(r, r + ROW_TILE) for r in range(0, o_ref.shape[1], ROW_TILE)]


def _ab_post_kernel(x_ref, xc_ref, mod_ref, att_ref, attc_ref, rf_ref, rb_ref, rg_ref,
                    wo_ref, ln_ref, w1_ref, w2_ref, o_ref, *, alpha, nlt, tile0):
    subs = _sub_tiles(o_ref)
    mixes = []
    for rows in subs:
        ret = rf_ref[0, rows, :] + rb_ref[0, rows, :]
        gate = _silu(rg_ref[0, rows, :].astype(F32))
        parts = [_select_rows(nlt, att_ref, attc_ref, tile0, rows)]
        for hd, nh in enumerate(_head_norm_lanes(ret)):
            parts.append((gate[:, hd * LANES:(hd + 1) * LANES] * nh).astype(BF16))
        mixes.append(jnp.concatenate(parts, axis=-1))
    xs = [_select_rows(nlt, x_ref, xc_ref, tile0, rows) for rows in subs]
    for rows, out in zip(subs, _post_stages(xs, mixes, mod_ref, wo_ref, ln_ref, w1_ref, w2_ref, alpha)):
        o_ref[0, rows, :] = out


def _m_post_kernel(x_ref, xc_ref, mod_ref, hf_ref, hb_ref, og_ref, ng_ref,
                   wo_ref, ln_ref, w1_ref, w2_ref, o_ref, *, alpha, nlt, tile0):
    subs = _sub_tiles(o_ref)
    mixes = []
    for rows in subs:
        hs = hf_ref[0, rows, :] + hb_ref[0, rows, :]
        og = jax.nn.sigmoid(og_ref[0, rows, :].astype(F32))
        parts = []
        for hd, nh in enumerate(_head_norm_lanes(hs)):
            sl = slice(hd * LANES, (hd + 1) * LANES)
            parts.append((og[:, sl] * (nh * ng_ref[:, sl])).astype(BF16))
        mixes.append(jnp.concatenate(parts, axis=-1))
    xs = [_select_rows(nlt, x_ref, xc_ref, tile0, rows) for rows in subs]
    for rows, out in zip(subs, _post_stages(xs, mixes, mod_ref, wo_ref, ln_ref, w1_ref, w2_ref, alpha)):
        o_ref[0, rows, :] = out


def _post(kernel_fn, x_parts, modsel, acts, consts, w_out, ln, w1, w2, n_lat, tm, tile0, n_tiles, alpha):
    b, d = modsel.shape[0], modsel.shape[-1]
    assert n_lat % tm == 0 and tm % ROW_TILE == 0
    nlt = n_lat // tm
    row = lambda bi, i: (bi, i + tile0, 0)
    in_specs, args = _split_rows_specs(x_parts, nlt, tm, tile0)
    in_specs.append(pl.BlockSpec((1, 1, 6, d), lambda bi, i: (bi, (i + tile0 >= nlt).astype(jnp.int32), 0, 0)))
    args.append(modsel)
    for a in acts:
        if isinstance(a, tuple):
            sp, ar = _split_rows_specs(a, nlt, tm, tile0)
            in_specs += sp
            args += ar
        else:
            in_specs.append(pl.BlockSpec((1, tm, a.shape[-1]), row))
            args.append(a)
    weights = [*consts, w_out, ln, w1, w2]
    in_specs += [_const_spec(w.shape) for w in weights]
    return pl.pallas_call(
        functools.partial(kernel_fn, alpha=alpha, nlt=nlt, tile0=tile0),
        grid=(b, n_tiles),
        in_specs=in_specs,
        out_specs=pl.BlockSpec((1, tm, d), lambda bi, i: (bi, i, 0)),
        out_shape=jax.ShapeDtypeStruct((b, n_tiles * tm, d), F32),
        compiler_params=_cparams(("parallel", "arbitrary")),
        name="post_" + kernel_fn.__name__ + ("_ctx" if tile0 else ""),
    )(*args, *weights)


_M_G, _M_QK, _M_V, _M_OG, _M_END = 0, 128, 1152, 2176, 3200


def _m_in_kernel(x_ref, xc_ref, xp_ref, xpc_ref, xn_ref, xnc_ref, mod_ref, w_ref, cw_ref, cb_ref, gb_ref,
                 qt_ref, k_ref, vt_ref, og_ref, gt1_ref, gt2_ref, gc_ref, *, nlt, seg_starts, seg_ends):
    i = pl.program_id(1)
    tm = x_ref.shape[1]
    shift = mod_ref[0, 0, 0:1, :]
    scale = mod_ref[0, 0, 1:2, :]
    h = (_select_rows(nlt, x_ref, xc_ref) * (1.0 + scale) + shift).astype(BF16)
    g = jnp.dot(h, w_ref[:, _M_G:_M_QK], preferred_element_type=F32) + gb_ref[...]
    lsg = jnp.minimum(g, 0.0) - jnp.log1p(jnp.exp(-jnp.abs(g)))
    r = lax.broadcasted_iota(jnp.int32, (tm, tm), 0)
    cidx = lax.broadcasted_iota(jnp.int32, (tm, tm), 1)
    same = (r // M_SCAN_CHUNK) == (cidx // M_SCAN_CHUNK)
    tril = jnp.where(jnp.logical_and(same, cidx <= r), 1.0, 0.0)
    triu = jnp.where(jnp.logical_and(same, cidx >= r), 1.0, 0.0)
    pre = jnp.dot(tril, lsg, precision=HIGHEST, preferred_element_type=F32)
    suf = jnp.dot(triu, lsg, precision=HIGHEST, preferred_element_type=F32)
    lane = lax.broadcasted_iota(jnp.int32, g.shape, 1)
    grp = lane // H_M
    gt1_ref[0] = jnp.where(grp == 1, pre, jnp.where(grp == 3, suf, g)).T
    bsum = jnp.where(grp == 1, pre, jnp.where(grp == 3, suf, 0.0))
    b_at_i = pltpu.roll(bsum, LANES - H_M, axis=1)
    r_gate = g - b_at_i
    pos = lax.broadcasted_iota(jnp.int32, g.shape, 0) % M_SCAN_CHUNK
    pmax = r_gate
    smax = r_gate
    step = 1
    while step < M_SCAN_CHUNK:
        pmax = jnp.where(pos >= step, jnp.maximum(pmax, pltpu.roll(pmax, step, axis=0)), pmax)
        smax = jnp.where(pos < M_SCAN_CHUNK - step, jnp.maximum(smax, pltpu.roll(smax, tm - step, axis=0)), smax)
        step *= 2
    gc_ref[0] = r_gate * LOG2E
    gt2_ref[0] = (b_at_i + jnp.where(grp == 0, pmax, smax)).T

    y = jnp.dot(h, w_ref[:, _M_QK:_M_END], preferred_element_type=F32)
    hp = (_select_rows(nlt, xp_ref, xpc_ref) * (1.0 + scale) + shift).astype(BF16)
    hn = (_select_rows(nlt, xn_ref, xnc_ref) * (1.0 + scale) + shift).astype(BF16)
    wqk = w_ref[:, _M_QK:_M_V]
    up = jnp.dot(hp, wqk, preferred_element_type=F32)[7:8, :]
    un = jnp.dot(hn, wqk, preferred_element_type=F32)[0:1, :]
    is_start = functools.reduce(jnp.logical_or, [i == s for s in seg_starts])
    is_end = functools.reduce(jnp.logical_or, [i == e for e in seg_ends])
    up = jnp.where(is_start, 0.0, up)
    un = jnp.where(is_end, 0.0, un)

    u = y[:, 0:_M_V - _M_QK]
    rows = lax.broadcasted_iota(jnp.int32, u.shape, 0)
    u_prev = jnp.where(rows == 0, up, pltpu.roll(u, 1, axis=0))
    u_next = jnp.where(rows == tm - 1, un, pltpu.roll(u, tm - 1, axis=0))
    qk = _silu(cw_ref[0:1, :] * u_prev + cw_ref[1:2, :] * u + cw_ref[2:3, :] * u_next + cb_ref[...])
    nq = H_M * M_DK
    for grp in range(nq // LANES):
        sl = slice(grp * LANES, (grp + 1) * LANES)
        qt_ref[0, sl, :] = qk[:, sl].T.astype(BF16)
    for hd in range(H_M):
        sl = slice(hd * LANES, (hd + 1) * LANES)
        vt_ref[0, sl, :] = y[:, _M_V - _M_QK + hd * LANES:_M_V - _M_QK + (hd + 1) * LANES].T.astype(BF16)
    k_ref[0] = (qk[:, nq:] * (M_DK ** -0.5)).astype(BF16)
    og_ref[0] = y[:, _M_OG - _M_QK:].astype(BF16)


def _m_in_proj(x_parts, modsel, w_ext, conv_w, conv_b, gate_b, nlt, t):
    b, d = modsel.shape[0], modsel.shape[-1]
    tm = ROW_TILE
    nt = t // tm
    r8 = tm // 8
    lat, ctx, ctx_tile0 = x_parts
    c0, n8_lat, n8_ctx = ctx_tile0 * r8, nlt * r8, (nt - nlt) * r8
    halo = lambda off, lo, n8: (lambda bi, i: (bi, lo + jnp.clip(i * r8 + off, 0, n8 - 1), 0))
    halo_ctx = lambda off: (lambda bi, i: (bi, c0 + jnp.clip((i - nlt) * r8 + off, 0, n8_ctx - 1), 0))
    x_specs, x_args = _split_rows_specs(x_parts, nlt, tm)
    x_specs += [pl.BlockSpec((1, 8, d), halo(-1, 0, n8_lat)), pl.BlockSpec((1, 8, d), halo_ctx(-1)),
                pl.BlockSpec((1, 8, d), halo(r8, 0, n8_lat)), pl.BlockSpec((1, 8, d), halo_ctx(r8))]
    x_args += [lat, ctx, lat, ctx]
    row = lambda bi, i: (bi, i, 0)
    col = lambda bi, i: (bi, 0, i)
    nk, n = H_M * M_DK, H_M * M_DV
    out_shapes = [jax.ShapeDtypeStruct((b, nk, t), BF16), jax.ShapeDtypeStruct((b, t, nk), BF16),
                  jax.ShapeDtypeStruct((b, n, t), BF16), jax.ShapeDtypeStruct((b, t, n), BF16)] + \
                 [jax.ShapeDtypeStruct((b, LANES, t), F32)] * 2 + [jax.ShapeDtypeStruct((b, t, LANES), F32)]
    out_specs = [pl.BlockSpec((1, nk, tm), col), pl.BlockSpec((1, tm, nk), row),
                 pl.BlockSpec((1, n, tm), col), pl.BlockSpec((1, tm, n), row)] + \
                [pl.BlockSpec((1, LANES, tm), col)] * 2 + [pl.BlockSpec((1, tm, LANES), row)]
    kern = functools.partial(_m_in_kernel, nlt=nlt, seg_starts=(0, nlt), seg_ends=(nlt - 1, nt - 1))
    return pl.pallas_call(
        kern,
        grid=(b, nt),
        in_specs=x_specs + [
                  pl.BlockSpec((1, 1, 6, d), lambda bi, i: (bi, (i >= nlt).astype(jnp.int32), 0, 0)),
                  _const_spec(w_ext.shape), _const_spec(conv_w.shape), _const_spec(conv_b.shape),
                  _const_spec(gate_b.shape)],
        out_specs=out_specs,
        out_shape=out_shapes,
        compiler_params=_cparams(("parallel", "arbitrary")),
        name="m_in_proj",
    )(*x_args, modsel, w_ext, conv_w, conv_b, gate_b)


def _mlstm_gates(gc, gt1, gt2, hd, backward, m_prev):
    L = M_SCAN_CHUNK
    gi = 2 * H_M if backward else 0
    i_row = gt1[gi + hd:gi + hd + 1, :]
    b_row = gt1[gi + H_M + hd:gi + H_M + hd + 1, :]
    mloc_row = gt2[gi + hd:gi + hd + 1, :]
    r_col = gc[:, gi + hd:gi + hd + 1]
    inter = b_row + m_prev
    m_t = jnp.maximum(inter, mloc_row)
    e = 0 if backward else L - 1
    b_end = b_row[:, e:e + 1]
    m_new = jnp.maximum(b_end + m_prev, mloc_row[:, e:e + 1])
    return i_row, b_row, r_col, inter, m_t, b_end, m_new


def _mlstm_kernel(*refs):
    ins, (hf_ref, hb_ref, cstate, mstate, sc_scr) = refs[:12], refs[12:]
    c = pl.program_id(1)
    L = M_SCAN_CHUNK

    @pl.when(c == 0)
    def _():
        cstate[...] = jnp.zeros_like(cstate)
        mstate[...] = jnp.zeros_like(mstate)

    chains = [(d, hd) for d in range(2) for hd in range(H_M)]

    def operands(d, hd):
        k_ref, qt_ref, vt_ref, gc_ref, gt1_ref, gt2_ref = ins[6 * d:6 * d + 6]
        pair = slice((hd // 2) * LANES, (hd // 2 + 1) * LANES)
        k_pair = k_ref[0, :, pair]
        lane = lax.broadcasted_iota(jnp.int32, k_pair.shape, 1)
        k_own = jnp.where((lane // M_DK) == (hd % 2), k_pair, jnp.zeros_like(k_pair))
        gates = _mlstm_gates(gc_ref[0], gt1_ref[0], gt2_ref[0], hd, d == 1, mstate[d * H_M + hd][0:1, 0:1])
        return k_own, qt_ref[0, pair, :], vt_ref[0, hd * LANES:(hd + 1) * LANES, :], gates

    ones_rows = jnp.ones((M_ONES_ROWS, L), BF16)
    s_idx = lax.broadcasted_iota(jnp.int32, (L, L), 0)
    t_idx = lax.broadcasted_iota(jnp.int32, (L, L), 1)

    for ch, (d, hd) in enumerate(chains):
        k_own, qt, _, (_, b_row, r_col, _, m_t, _, _) = operands(d, hd)
        visible = (s_idx >= t_idx) if d == 1 else (s_idx <= t_idx)
        dt = jnp.exp2(jnp.where(visible, r_col + (b_row - m_t) * LOG2E, NEG))
        sc_scr[ch] = (jnp.dot(k_own, qt, preferred_element_type=F32) * dt).astype(BF16)

    for ch, (d, hd) in enumerate(chains):
        _, qt, vt, (_, _, _, inter, m_t, _, _) = operands(d, hd)
        vext_t = jnp.concatenate([vt, ones_rows], axis=0)
        nd_t = jnp.dot(vext_t, sc_scr[ch], preferred_element_type=F32) \
            + jnp.dot(cstate[ch].astype(BF16), qt, preferred_element_type=F32) * jnp.exp(inter - m_t)
        den = jnp.broadcast_to(nd_t[LANES:LANES + 1, :], (LANES, L))
        h_t = nd_t[:LANES] / jnp.maximum(jnp.abs(den), jnp.exp(-m_t))
        (hb_ref if d else hf_ref)[0, :, hd * LANES:(hd + 1) * LANES] = h_t.T

    for ch, (d, hd) in enumerate(chains):
        k_own, _, vt, (i_row, b_row, _, _, _, b_end, m_new) = operands(d, hd)
        m_prev = mstate[ch][0:1, 0:1]
        vext_t = jnp.concatenate([vt, ones_rows], axis=0)
        w_row = jnp.exp(b_end - b_row + i_row - m_new)
        cstate[ch] = jnp.exp(b_end + m_prev - m_new) * cstate[ch] \
            + jnp.dot((vext_t.astype(F32) * w_row).astype(BF16), k_own, preferred_element_type=F32)
        mstate[ch] = jnp.broadcast_to(m_new, mstate.shape[1:])


def _mlstm(k, qt, vt, gc, gt1, gt2, n_lat):
    b, t, nk = k.shape
    n = vt.shape[1]
    L = M_SCAN_CHUNK
    nb = t // L
    fwd, bwd, fwd_t, bwd_t = _scan_maps(n_lat // L, nb - n_lat // L)

    def specs(rmap, tmap):
        return [pl.BlockSpec((1, L, nk), rmap), pl.BlockSpec((1, nk, L), tmap), pl.BlockSpec((1, n, L), tmap),
                pl.BlockSpec((1, L, LANES), rmap), pl.BlockSpec((1, 4 * H_M, L), tmap),
                pl.BlockSpec((1, 4 * H_M, L), tmap)]

    out = jax.ShapeDtypeStruct((b, t, H_M * M_DV), F32)
    args = (k, qt, vt, gc, gt1, gt2)
    return pl.pallas_call(
        _mlstm_kernel,
        grid=(b, nb),
        in_specs=specs(fwd, fwd_t) + specs(bwd, bwd_t),
        out_specs=[pl.BlockSpec((1, L, H_M * M_DV), fwd), pl.BlockSpec((1, L, H_M * M_DV), bwd)],
        out_shape=[out, out],
        scratch_shapes=[pltpu.VMEM((2 * H_M, LANES + M_ONES_ROWS, LANES), F32),
                        pltpu.VMEM((2 * H_M, 8, LANES), F32),
                        pltpu.VMEM((2 * H_M, L, L), BF16)],
        compiler_params=_cparams(("parallel", "arbitrary")),
        name="mlstm_scan",
    )(*args, *args)


def _rope_tables(n_lat, ctx_len, d, lane0):
    rows = n_lat // GRID_W
    row = np.repeat(np.arange(rows, dtype=np.float64), GRID_W)
    col = np.tile(np.arange(GRID_W, dtype=np.float64), rows)
    da = d // 2
    inv = ROPE_BASE ** (-np.arange(0, da, 2, dtype=np.float64) / da)
    ar = row[:, None] * inv
    ac = col[:, None] * inv
    tab = np.zeros((2, n_lat + ctx_len, LANES), np.float32)
    tab[0] = 1.0
    tab[0, :n_lat, lane0:lane0 + d] = np.concatenate([np.cos(ar), np.cos(ar), np.cos(ac), np.cos(ac)], axis=-1)
    tab[1, :n_lat, lane0:lane0 + d] = np.concatenate([-np.sin(ar), np.sin(ar), -np.sin(ac), np.sin(ac)], axis=-1)
    return jnp.asarray(tab)


def _pad_heads(w, n_heads, width):
    k = w.shape[0]
    w = w.reshape(k, n_heads, width)
    return jnp.pad(w, ((0, 0), (0, 0), (0, LANES - width))).reshape(k, n_heads * LANES)


def _prep_ab_weights(w_in, w_uq, w_ukv):
    o1 = MLA_Q_LORA
    o2 = o1 + MLA_KV_LORA
    o3 = o2 + MLA_ROPE
    o4 = o3 + H_RET * RET_DK
    o5 = o4 + H_RET * RET_DK
    o6 = o5 + H_RET * RET_DV
    cq, ckv, kr, rq, rk, rv, rg = (w_in[:, :o1], w_in[:, o1:o2], w_in[:, o2:o3], w_in[:, o3:o4],
                                   w_in[:, o4:o5], w_in[:, o5:o6], w_in[:, o6:])
    kr_blk = jnp.pad(kr, ((0, 0), (MLA_NOPE, LANES - MLA_NOPE - MLA_ROPE)))
    w_ext = jnp.concatenate([cq, ckv, rq, rk, rv, rg, kr_blk], axis=1)

    wuq_ext = _pad_heads(w_uq, H_MLA, MLA_NOPE + MLA_ROPE)
    ukv = w_ukv.reshape(-1, H_MLA, MLA_NOPE + MLA_V)
    k_nope = _pad_heads(ukv[..., :MLA_NOPE].reshape(-1, H_MLA * MLA_NOPE), H_MLA, MLA_NOPE)
    v_w = ukv[..., MLA_NOPE:]
    zeros = jnp.zeros_like(v_w)
    even = (np.arange(H_MLA) % 2 == 0)[None, :, None]
    v_cols = jnp.concatenate([jnp.where(even, v_w, zeros), jnp.where(even, zeros, v_w)], axis=-1)
    wukv_ext = jnp.concatenate([k_nope, v_cols.reshape(-1, H_MLA * LANES)], axis=1)
    return w_ext.astype(BF16), wuq_ext.astype(BF16), wukv_ext.astype(BF16)


def _prep_m_weights(w_in, conv_w, conv_b, gate_b):
    o1 = 2 * H_M * M_DK
    o2 = o1 + H_M * M_DV
    o3 = o2 + H_M * M_DV
    g_pad = jnp.pad(w_in[:, o3:], ((0, 0), (0, LANES - 4 * H_M)))
    w_ext = jnp.concatenate([g_pad, w_in[:, :o3]], axis=1).astype(BF16)
    gb = jnp.pad(gate_b.reshape(1, 4 * H_M), ((0, 0), (0, LANES - 4 * H_M)))
    return w_ext, conv_w, conv_b[None, :], gb


def kernel(x, c, ctx, c_ctx, mod_w, mod_b, ln_g, ln_b, ffn_w_in, ffn_w_out, ab_w_in, mla_q_norm, mla_w_uq,
           mla_kv_norm, mla_w_ukv, ret_log_decay, ab_w_out, m_w_in, m_conv_w, m_conv_b, m_gate_b, m_norm_g, m_w_out):
    b, s, d = x.shape
    ctx_len = ctx.shape[1]
    depth = mod_w.shape[0]
    assert ctx_len % ROW_TILE == 0 and s % ROW_TILE == 0 and s % GRID_W == 0 and b + 1 <= 8
    nlt = s // ROW_TILE
    alpha = (2 * depth) ** 0.25

    c_rows = jnp.zeros((8, d), F32).at[:b].set(c).at[b].set(c_ctx)
    mods = _modulation(c_rows, mod_w, mod_b).reshape(depth, 8, 6, d)
    tab_mla = _rope_tables(s, ctx_len, MLA_ROPE, MLA_NOPE)
    tab_ret = _rope_tables(s, ctx_len, RET_DK, 0)

    t = s + ctx_len
    x_lat, x_ctx = x, ctx
    for l in range(depth):
        last = l == depth - 1
        j = l // 2
        modsel = jnp.stack([mods[l, :b], jnp.broadcast_to(mods[l, b], (b, 6, d))], axis=1)
        ln = jnp.stack([ln_g[l, 0], ln_b[l, 0], ln_g[l, 1], ln_b[l, 1]])
        w1 = ffn_w_in[l].astype(BF16)
        w2 = ffn_w_out[l].astype(BF16)
        x_parts = (x_lat, x_ctx, 0)
        if l % 2 == 0:
            w_ext, wuq_ext, wukv_ext = _prep_ab_weights(ab_w_in[j], mla_w_uq[j], mla_w_ukv[j])
            q, kt, v, rq, rkt, rv, rg = _ab_in_proj(x_parts, modsel, w_ext, mla_q_norm[j][None, :],
                                                    mla_kv_norm[j][None, :], wuq_ext, wukv_ext,
                                                    tab_mla, tab_ret, nlt, t)
            att_lat = _attention(q, kt, v, s, False)
            att_ctx = None if last else _attention(q, kt, v, s, True)
            ret_f, ret_b = _retention(rq, rkt, rv, ret_log_decay[j], s)
            post_fn, acts, consts, w_out = _ab_post_kernel, [ret_f, ret_b, rg], [], ab_w_out[j]
        else:
            w_ext, cw, cb, gb = _prep_m_weights(m_w_in[j], m_conv_w[j], m_conv_b[j], m_gate_b[j])
            qt, k, vt, og, gt1, gt2, gc = _m_in_proj(x_parts, modsel, w_ext, cw, cb, gb, nlt, t)
            hf, hb = _mlstm(k, qt, vt, gc, gt1, gt2, s)
            post_fn, acts, consts, w_out = _m_post_kernel, [hf, hb, og], [m_norm_g[j][None, :]], m_w_out[j]
            att_lat = att_ctx = None

        def post(tm, tile0, n_tiles, for_ctx):
            pair = (lambda a_lat, a_ctx: (a_lat, a_ctx, 0) if for_ctx else (a_lat, a_lat, 0))
            att = [] if att_lat is None else [pair(att_lat, att_ctx)]
            return _post(post_fn, pair(x_lat, x_ctx), modsel, att + acts, consts, w_out.astype(BF16),
                         ln, w1, w2, s, tm, tile0, n_tiles, alpha)

        new_lat = post(POST_TILE, 0, s // POST_TILE, False)
        if not last:
            x_ctx = post(ROW_TILE, nlt, ctx_len // ROW_TILE, True)
        x_lat = new_lat
    return x_lat
```

```python
import functools

import jax
import jax.numpy as jnp
import numpy as np
from jax import lax
from jax.experimental import pallas as pl
from jax.experimental.pallas import tpu as pltpu

F32 = jnp.float32
BF16 = jnp.bfloat16
HIGHEST = lax.Precision.HIGHEST

GRID_W = 64
ROPE_BASE = 10000.0
EPS = 1e-5
NEG = -1e30
LOG2E = 1.4426950408889634
H_MLA, MLA_NOPE, MLA_ROPE, MLA_V = 8, 64, 32, 64
MLA_Q_LORA, MLA_KV_LORA = 384, 256
H_RET, RET_DK, RET_DV = 4, 128, 128
H_M, M_DK, M_DV = 8, 64, 128

LANES = 128
ROW_TILE = 256
POST_TILE = 512
RET_CHUNK = 128
RET_BLOCK = 256
ATT_KEY_CHUNK = 512
ATT_Q_TILE = 1024
M_SCAN_CHUNK = 256
M_ONES_ROWS = 16
VMEM_LIMIT = 56 * 1024 * 1024


def _cparams(sem, flags=None):
    return pltpu.CompilerParams(dimension_semantics=sem, vmem_limit_bytes=VMEM_LIMIT, flags=flags)


def _const_spec(shape):
    nd = len(shape)
    return pl.BlockSpec(shape, lambda *_: (0,) * nd, pipeline_mode=pl.Buffered(1))


def _split_rows_specs(parts, nlt, tm, tile0=0, step=lambda i: i, mode=None):
    lat, ctx, ctx_tile0 = parts
    n = lat.shape[-1]
    specs = [pl.BlockSpec((1, tm, n), lambda bi, i: (bi, jnp.minimum(step(i) + tile0, nlt - 1), 0),
                          pipeline_mode=mode)]
    if ctx is None:
        return specs, [lat]
    specs.append(pl.BlockSpec((1, tm, n), lambda bi, i: (bi, ctx_tile0 + jnp.maximum(step(i) + tile0 - nlt, 0), 0),
                              pipeline_mode=mode))
    return specs, [lat, ctx]


def _select_rows(nlt, lat_ref, ctx_ref, tile0=0, rows=slice(None)):
    if ctx_ref is None:
        return lat_ref[0, rows, :]
    return jnp.where(pl.program_id(1) + tile0 >= nlt, ctx_ref[0, rows, :], lat_ref[0, rows, :])


def _silu(v):
    return v * jax.nn.sigmoid(v)


def _layer_norm_rows(v, g, b):
    mu = jnp.mean(v, axis=-1, keepdims=True)
    d = v - mu
    var = jnp.mean(d * d, axis=-1, keepdims=True)
    return d * lax.rsqrt(var + EPS) * g + b


def _head_norm_lanes(v):
    outs = []
    for h in range(v.shape[-1] // LANES):
        blk = v[:, h * LANES:(h + 1) * LANES]
        mu = jnp.mean(blk, axis=-1, keepdims=True)
        d = blk - mu
        var = jnp.mean(d * d, axis=-1, keepdims=True)
        outs.append(d * lax.rsqrt(var + EPS))
    return outs


def _mod_kernel(c_ref, w_ref, b_ref, o_ref):
    sc = _silu(c_ref[...])
    o_ref[0] = jnp.dot(sc, w_ref[0], precision=HIGHEST, preferred_element_type=F32) + b_ref[0]


def _modulation(c_rows, mod_w, mod_b):
    depth, d, n = mod_w.shape
    tn = 1536
    return pl.pallas_call(
        _mod_kernel,
        grid=(depth, n // tn),
        in_specs=[pl.BlockSpec((8, d), lambda l, j: (0, 0)),
                  pl.BlockSpec((1, d, tn), lambda l, j: (l, 0, j)),
                  pl.BlockSpec((1, 1, tn), lambda l, j: (l, 0, j))],
        out_specs=pl.BlockSpec((1, 8, tn), lambda l, j: (l, 0, j)),
        out_shape=jax.ShapeDtypeStruct((depth, 8, n), F32),
        compiler_params=_cparams(("arbitrary", "arbitrary")),
        name="modulation",
    )(c_rows, mod_w, mod_b.reshape(depth, 1, n))


_A_CQ, _A_CKV, _A_RQ, _A_RK, _A_RV, _A_RG, _A_KR, _A_END = 0, 384, 640, 1152, 1664, 2176, 2688, 2816


def _swap_pairs(v, width):
    lane = lax.broadcasted_iota(jnp.int32, v.shape, 1)
    return jnp.where((lane // width) % 2 == 0, pltpu.roll(v, LANES - width, axis=1), pltpu.roll(v, width, axis=1))


def _ab_in_kernel(x_ref, xc_ref, mod_ref, w_ref, qn_ref, kvn_ref, wuq_ref, wukv_ref, tm_ref, tr_ref,
                  q_ref, kt_ref, v_ref, rq_ref, rkt_ref, rv_ref, rg_ref, *, nlt):
    x = _select_rows(nlt, x_ref, xc_ref)
    shift = mod_ref[0, 0, 0:1, :]
    scale = mod_ref[0, 0, 1:2, :]
    h = (x * (1.0 + scale) + shift).astype(BF16)
    y = jnp.dot(h, w_ref[...], preferred_element_type=F32)

    cq = y[:, _A_CQ:_A_CKV]
    ckv = y[:, _A_CKV:_A_RQ]
    ncq = (cq * lax.rsqrt(jnp.mean(cq * cq, axis=-1, keepdims=True) + EPS) * qn_ref[...]).astype(BF16)
    nckv = (ckv * lax.rsqrt(jnp.mean(ckv * ckv, axis=-1, keepdims=True) + EPS) * kvn_ref[...]).astype(BF16)
    q2 = jnp.dot(ncq, wuq_ref[...], preferred_element_type=F32)
    kv = jnp.dot(nckv, wukv_ref[...], preferred_element_type=F32)

    cos_m = tm_ref[0]
    sin_m = tm_ref[1]
    q_scale = (MLA_NOPE + MLA_ROPE) ** -0.5 * LOG2E
    kr = y[:, _A_KR:_A_END]
    k_rope = kr * cos_m + _swap_pairs(kr, MLA_ROPE // 4) * sin_m
    nq = H_MLA * LANES
    for hd in range(H_MLA):
        sl = slice(hd * LANES, (hd + 1) * LANES)
        qh = q2[:, sl] * cos_m + _swap_pairs(q2[:, sl], MLA_ROPE // 4) * sin_m
        q_ref[0, :, sl] = (qh * q_scale).astype(BF16)
        kt_ref[0, sl, :] = (kv[:, sl] + k_rope).T.astype(BF16)
    lane = lax.broadcasted_iota(jnp.int32, (1, nq), 1)
    ones_half = jnp.where(((lane // MLA_V) % 2) != ((lane // LANES) % 2), 1.0, 0.0)
    v_ref[0] = (kv[:, nq:] + ones_half).astype(BF16)

    cos_r = tr_ref[0]
    sin_r = tr_ref[1]
    k_scale = RET_DK ** -0.5
    for hd in range(H_RET):
        sl = slice(hd * LANES, (hd + 1) * LANES)
        rq = y[:, _A_RQ + hd * LANES:_A_RQ + (hd + 1) * LANES]
        rk = y[:, _A_RK + hd * LANES:_A_RK + (hd + 1) * LANES]
        rq = rq * cos_r + _swap_pairs(rq, RET_DK // 4) * sin_r
        rk = rk * cos_r + _swap_pairs(rk, RET_DK // 4) * sin_r
        rq_ref[0, :, sl] = rq.astype(BF16)
        rkt_ref[0, sl, :] = (rk * k_scale).T.astype(BF16)
    rv_ref[0] = y[:, _A_RV:_A_RG].astype(BF16)
    rg_ref[0] = y[:, _A_RG:_A_KR].astype(BF16)


def _ab_in_proj(x_parts, modsel, w_ext, q_norm, kv_norm, wuq_ext, wukv_ext, tab_mla, tab_ret, nlt, t):
    b, d = modsel.shape[0], modsel.shape[-1]
    tm = ROW_TILE
    row = lambda bi, i: (bi, i, 0)
    col = lambda bi, i: (bi, 0, i)
    n_ret = H_RET * RET_DK
    widths = (H_MLA * LANES, H_MLA * LANES, H_MLA * LANES, n_ret, n_ret, n_ret, n_ret)
    transposed = (False, True, False, False, True, False, False)
    out_shapes = [jax.ShapeDtypeStruct((b, w, t) if tr else (b, t, w), BF16) for w, tr in zip(widths, transposed)]
    out_specs = [pl.BlockSpec((1, w, tm), col) if tr else pl.BlockSpec((1, tm, w), row)
                 for w, tr in zip(widths, transposed)]
    x_specs, x_args = _split_rows_specs(x_parts, nlt, tm)
    return pl.pallas_call(
        functools.partial(_ab_in_kernel, nlt=nlt),
        grid=(b, t // tm),
        in_specs=x_specs + [
                  pl.BlockSpec((1, 1, 6, d), lambda bi, i: (bi, (i >= nlt).astype(jnp.int32), 0, 0)),
                  _const_spec(w_ext.shape), _const_spec(q_norm.shape), _const_spec(kv_norm.shape),
                  _const_spec(wuq_ext.shape), _const_spec(wukv_ext.shape),
                  pl.BlockSpec((2, tm, LANES), lambda bi, i: (0, i, 0)),
                  pl.BlockSpec((2, tm, LANES), lambda bi, i: (0, i, 0))],
        out_specs=out_specs,
        out_shape=out_shapes,
        compiler_params=_cparams(("parallel", "arbitrary")),
        name="ab_in_proj",
    )(*x_args, modsel, w_ext, q_norm, kv_norm, wuq_ext, wukv_ext, tab_mla, tab_ret)


def _attn_kernel(q_ref, kt_ref, v_ref, o_ref, s_scr, m_scr, acc_scr, *, tk, ctx_key0, ctx_len, n_lat_chunks):
    tq = q_ref.shape[1]
    m_scr[...] = jnp.full(m_scr.shape, NEG, F32)
    acc_scr[...] = jnp.zeros_like(acc_scr)

    def scores(slot, r0, size):
        for hh in range(2):
            sl = slice(hh * LANES, (hh + 1) * LANES)
            s_scr[slot, hh, :, 0:size] = jnp.dot(q_ref[0, :, sl], kt_ref[0, sl, pl.ds(r0, size)],
                                                 preferred_element_type=F32)

    def accumulate(slot, r0, size):
        for hh in range(2):
            sl = slice(hh * LANES, (hh + 1) * LANES)
            s = s_scr[slot, hh, :, 0:size]
            m = m_scr[hh]
            m_new = jnp.maximum(m, jnp.max(s, axis=-1, keepdims=True))
            m_scr[hh] = m_new
            p = jnp.exp2(s - m_new[:, 0:1]).astype(BF16)
            acc_scr[hh] = jnp.exp2(m - m_new) * acc_scr[hh] \
                + jnp.dot(p, v_ref[0, pl.ds(r0, size), sl], preferred_element_type=F32)

    def lat_row(c):
        return pl.multiple_of(jnp.minimum(c, n_lat_chunks - 1) * tk, tk)

    scores(1, ctx_key0, ctx_len)
    if n_lat_chunks:
        scores(0, 0, tk)
    accumulate(1, ctx_key0, ctx_len)
    if n_lat_chunks:

        def body(j, carry):
            scores(1, lat_row(2 * j + 1), tk)
            accumulate(0, lat_row(2 * j), tk)
            scores(0, lat_row(2 * j + 2), tk)
            accumulate(1, lat_row(2 * j + 1), tk)
            return carry

        lax.fori_loop(0, n_lat_chunks // 2, body, 0)
    outs = [acc_scr[hh] / pltpu.roll(acc_scr[hh], MLA_V, axis=1) for hh in range(2)]
    lane = lax.broadcasted_iota(jnp.int32, (tq, LANES), 1)
    o_ref[0] = jnp.where(lane < MLA_V, outs[0], outs[1]).astype(BF16)


def _attention(q, kt, v, n_lat, for_ctx):
    b, t, _ = q.shape
    ctx_len = t - n_lat
    tk = ATT_KEY_CHUNK
    assert n_lat % (2 * tk) == 0 and n_lat % ATT_Q_TILE == 0 and ctx_len % ROW_TILE == 0
    tq, n_rows, row0 = (ROW_TILE, ctx_len, n_lat // ROW_TILE) if for_ctx else (ATT_Q_TILE, n_lat, 0)
    pair = 2 * LANES
    if for_ctx:
        assert n_lat % ctx_len == 0
        key_blk, n_keys, key0 = n_lat // ctx_len, ctx_len, 0
    else:
        key_blk, n_keys, key0 = 0, t, n_lat
    kern = functools.partial(_attn_kernel, tk=tk, ctx_key0=key0, ctx_len=ctx_len,
                             n_lat_chunks=0 if for_ctx else n_lat // tk)
    return pl.pallas_call(
        kern,
        grid=(b, H_MLA // 2, n_rows // tq),
        in_specs=[pl.BlockSpec((1, tq, pair), lambda bi, hp, i: (bi, i + row0, hp)),
                  pl.BlockSpec((1, pair, n_keys), lambda bi, hp, i: (bi, hp, key_blk)),
                  pl.BlockSpec((1, n_keys, pair), lambda bi, hp, i: (bi, key_blk, hp))],
        out_specs=pl.BlockSpec((1, tq, LANES), lambda bi, hp, i: (bi, i, hp)),
        out_shape=jax.ShapeDtypeStruct((b, n_rows, H_MLA * MLA_V), BF16),
        scratch_shapes=[pltpu.VMEM((2, 2, tq, max(tk, ctx_len)), F32), pltpu.VMEM((2, tq, LANES), F32),
                        pltpu.VMEM((2, tq, LANES), F32)],
        compiler_params=_cparams(("parallel", "parallel", "arbitrary")),
        name="mla_attention_ctx" if for_ctx else "mla_attention",
    )(q, kt, v)


def _ret_kernel(ld_ref, qf_ref, ktf_ref, vf_ref, qb_ref, ktb_ref, vb_ref, of_ref, ob_ref,
                state, dmat, dq, wk, gl, s_scr):
    c = pl.program_id(1)
    L = RET_CHUNK

    @pl.when(c == 0)
    def _():
        state[...] = jnp.zeros_like(state)
        row = lax.broadcasted_iota(jnp.int32, (L, L), 0).astype(F32)
        col = lax.broadcasted_iota(jnp.int32, (L, L), 1).astype(F32)
        qpos = lax.broadcasted_iota(jnp.int32, (L, RET_DK), 0).astype(F32)
        kpos = lax.broadcasted_iota(jnp.int32, (RET_DK, L), 1).astype(F32)
        for d in range(2):
            for hd in range(H_RET):
                lg = ld_ref[d, hd]
                if d == 0:
                    rel = row - col
                    dq[d, hd] = jnp.exp(lg * (qpos + 1.0))
                    wk[d, hd] = jnp.exp(lg * (L - 1.0 - kpos))
                else:
                    rel = col - row
                    dq[d, hd] = jnp.exp(lg * (L - qpos))
                    wk[d, hd] = jnp.exp(lg * kpos)
                dmat[d, hd] = jnp.where(rel >= 0, jnp.exp(lg * jnp.maximum(rel, 0.0)), 0.0)
                gl[d, hd] = jnp.exp(jnp.zeros((RET_DK, RET_DV), F32) + lg * L)

    streams = ((qf_ref, ktf_ref, vf_ref, of_ref), (qb_ref, ktb_ref, vb_ref, ob_ref))
    n_sub = qf_ref.shape[1] // L
    for k in range(n_sub):
        rows = [slice(k * L, (k + 1) * L), slice((n_sub - 1 - k) * L, (n_sub - k) * L)]
        chains = [(d, hd, slice(hd * LANES, (hd + 1) * LANES), rows[d]) + streams[d]
                  for d in range(2) for hd in range(H_RET)]
        for d, hd, sl, r, q_ref, kt_ref, _, _ in chains:
            s = jnp.dot(q_ref[0, r, sl], kt_ref[0, sl, r], preferred_element_type=F32) * dmat[d, hd]
            s_scr[d, hd] = s.astype(BF16)
        for d, hd, sl, r, q_ref, _, v_ref, o_ref in chains:
            qd = (q_ref[0, r, sl].astype(F32) * dq[d, hd]).astype(BF16)
            o_ref[0, r, sl] = jnp.dot(s_scr[d, hd], v_ref[0, r, sl], preferred_element_type=F32) \
                + jnp.dot(qd, state[d, hd].astype(BF16), preferred_element_type=F32)
        for d, hd, sl, r, _, kt_ref, v_ref, _ in chains:
            kw = (kt_ref[0, sl, r].astype(F32) * wk[d, hd]).astype(BF16)
            state[d, hd] = state[d, hd] * gl[d, hd] + jnp.dot(kw, v_ref[0, r, sl], preferred_element_type=F32)


def _scan_maps(n_lat_blocks, n_ctx_blocks):
    nlb, ncb = n_lat_blocks, n_ctx_blocks
    fwd_blk = lambda c: jnp.where(c < ncb, nlb + c, c - ncb)
    bwd_blk = lambda c: nlb + ncb - 1 - c
    fwd = lambda bi, c: (bi, fwd_blk(c), 0)
    bwd = lambda bi, c: (bi, bwd_blk(c), 0)
    fwd_t = lambda bi, c: (bi, 0, fwd_blk(c))
    bwd_t = lambda bi, c: (bi, 0, bwd_blk(c))
    return fwd, bwd, fwd_t, bwd_t


def _retention(rq, rkt, rv, log_decay, n_lat):
    b, t, n = rq.shape
    L, blk = RET_CHUNK, RET_BLOCK
    nb = t // blk
    fwd, bwd, fwd_t, bwd_t = _scan_maps(n_lat // blk, nb - n_lat // blk)
    tile = pl.BlockSpec((1, blk, n), fwd)
    tile_b = pl.BlockSpec((1, blk, n), bwd)
    out = jax.ShapeDtypeStruct((b, t, n), F32)
    per_head = lambda rows, cols, dtype=F32: pltpu.VMEM((2, H_RET, rows, cols), dtype)
    scratch = [per_head(RET_DK, RET_DV),
               per_head(L, L),
               per_head(L, RET_DK),
               per_head(RET_DK, L),
               per_head(RET_DK, RET_DV),
               per_head(L, L, BF16)]
    return pl.pallas_call(
        _ret_kernel,
        grid=(b, nb),
        in_specs=[pl.BlockSpec(memory_space=pltpu.SMEM),
                  tile, pl.BlockSpec((1, n, blk), fwd_t), tile,
                  tile_b, pl.BlockSpec((1, n, blk), bwd_t), tile_b],
        out_specs=[tile, tile_b],
        out_shape=[out, out],
        scratch_shapes=scratch,
        compiler_params=_cparams(("parallel", "arbitrary")),
        name="retention_scan",
    )(log_decay, rq, rkt, rv, rq, rkt, rv)


def _post_stages(xs, mixes, mod_ref, wo_ref, ln_ref, w1_ref, w2_ref, alpha, mid_fn):
    g1 = mod_ref[0, 0, 2:3, :]
    sh2 = mod_ref[0, 0, 3:4, :]
    sc2 = mod_ref[0, 0, 4:5, :]
    g2 = mod_ref[0, 0, 5:6, :]
    ys = [jnp.dot(m, wo_ref[...], preferred_element_type=F32) for m in mixes]
    x1s = [_layer_norm_rows(alpha * x + g1 * y, ln_ref[0:1, :], ln_ref[1:2, :]) for x, y in zip(xs, ys)]
    us = [jnp.dot((x1 * (1.0 + sc2) + sh2).astype(BF16), w1_ref[...], preferred_element_type=F32) for x1 in x1s]
    mid_fn()
    dff = w1_ref.shape[-1] // 2
    acts = [(_silu(u[:, dff:]) * u[:, :dff]).astype(BF16) for u in us]
    y2s = [jnp.dot(a, w2_ref[...], preferred_element_type=F32) for a in acts]
    return [_layer_norm_rows(alpha * x1 + g2 * y2, ln_ref[2:3, :], ln_ref[3:4, :]) for x1, y2 in zip(x1s, y2s)]


def _sub_tiles(o_ref):
    return [slice(r, r + ROW_TILE) for r in range(0, o_ref.shape[1], ROW_TILE)]


def _ab_mix(act_refs, const_refs, nlt, tile0, rows):
    att_refs, (rf_ref,), (rb_ref,), (rg_ref,) = act_refs
    ret = rf_ref[0, rows, :] + rb_ref[0, rows, :]
    gate = _silu(rg_ref[0, rows, :].astype(F32))
    parts = [_select_rows(nlt, *_as_pair(att_refs), tile0, rows)]
    for hd, nh in enumerate(_head_norm_lanes(ret)):
        parts.append((gate[:, hd * LANES:(hd + 1) * LANES] * nh).astype(BF16))
    return jnp.concatenate(parts, axis=-1)


def _m_mix(act_refs, const_refs, nlt, tile0, rows):
    (hf_ref,), (hb_ref,), (og_ref,) = act_refs
    ng_ref, = const_refs
    hs = hf_ref[0, rows, :] + hb_ref[0, rows, :]
    og = jax.nn.sigmoid(og_ref[0, rows, :].astype(F32))
    parts = []
    for hd, nh in enumerate(_head_norm_lanes(hs)):
        sl = slice(hd * LANES, (hd + 1) * LANES)
        parts.append((og[:, sl] * (nh * ng_ref[:, sl])).astype(BF16))
    return jnp.concatenate(parts, axis=-1)


def _as_pair(refs):
    return tuple(refs) + (None,) * (2 - len(refs))


def _post_kernel(*refs, mix_fn, arity, n_const, alpha, nlt, tile0):
    refs = list(refs)
    take = lambda n: tuple(refs.pop(0) for _ in range(n))
    x_refs = take(arity[0])
    mod_ref, = take(1)
    first_refs = [take(n) for n in arity[1:]]
    next_refs = [take(n) for n in arity[1:]]
    const_refs = take(n_const)
    wo_ref, ln_ref, w1_ref, w2_ref, o_ref, mix_scr = refs
    i = pl.program_id(1)
    subs = _sub_tiles(o_ref)

    @pl.when(i == 0)
    def _():
        for rows in subs:
            mix_scr[0, rows, :] = mix_fn(first_refs, const_refs, nlt, tile0, rows)

    slot = i % 2
    mixes = [mix_scr[slot, rows, :] for rows in subs]
    xs = [_select_rows(nlt, *_as_pair(x_refs), tile0, rows) for rows in subs]

    def prefetch_next_mix():
        for rows in subs:
            mix_scr[1 - slot, rows, :] = mix_fn(next_refs, const_refs, nlt, tile0, rows)

    outs = _post_stages(xs, mixes, mod_ref, wo_ref, ln_ref, w1_ref, w2_ref, alpha, prefetch_next_mix)
    for rows, out in zip(subs, outs):
        o_ref[0, rows, :] = out


def _post(mix_fn, x_parts, modsel, acts, consts, w_out, ln, w1, w2, n_lat, tm, tile0, n_tiles, alpha):
    b, d = modsel.shape[0], modsel.shape[-1]
    assert n_lat % tm == 0 and tm % ROW_TILE == 0
    nlt = n_lat // tm
    in_specs, args = _split_rows_specs(x_parts, nlt, tm, tile0)
    arity = [len(args)]
    in_specs.append(pl.BlockSpec((1, 1, 6, d), lambda bi, i: (bi, (i + tile0 >= nlt).astype(jnp.int32), 0, 0)))
    args.append(modsel)
    for first, step, mode in ((True, lambda i: 0 * i, pl.Buffered(1)),
                              (False, lambda i: jnp.minimum(i + 1, n_tiles - 1), None)):
        for a in acts:
            if isinstance(a, tuple):
                sp, ar = _split_rows_specs(a, nlt, tm, tile0, step, mode)
            else:
                sp = [pl.BlockSpec((1, tm, a.shape[-1]), lambda bi, i, step=step: (bi, step(i) + tile0, 0),
                                   pipeline_mode=mode)]
                ar = [a]
            in_specs += sp
            args += ar
            if first:
                arity.append(len(ar))
    weights = [*consts, w_out, ln, w1, w2]
    in_specs += [_const_spec(w.shape) for w in weights]
    kern = functools.partial(_post_kernel, mix_fn=mix_fn, arity=tuple(arity), n_const=len(consts),
                             alpha=alpha, nlt=nlt, tile0=tile0)
    return pl.pallas_call(
        kern,
        grid=(b, n_tiles),
        in_specs=in_specs,
        out_specs=pl.BlockSpec((1, tm, d), lambda bi, i: (bi, i, 0)),
        out_shape=jax.ShapeDtypeStruct((b, n_tiles * tm, d), F32),
        scratch_shapes=[pltpu.VMEM((2, tm, w_out.shape[0]), BF16)],
        compiler_params=_cparams(("parallel", "arbitrary")),
        name="post" + mix_fn.__name__ + ("_ctx" if tile0 else ""),
    )(*args, *weights)


_M_G, _M_QK, _M_V, _M_OG, _M_END = 0, 128, 1152, 2176, 3200


def _m_in_kernel(x_ref, xc_ref, xp_ref, xpc_ref, xn_ref, xnc_ref, mod_ref, w_ref, cw_ref, cb_ref, gb_ref,
                 qt_ref, k_ref, vt_ref, og_ref, gt1_ref, gt2_ref, gc_ref, *, nlt, seg_starts, seg_ends):
    i = pl.program_id(1)
    tm = x_ref.shape[1]
    shift = mod_ref[0, 0, 0:1, :]
    scale = mod_ref[0, 0, 1:2, :]
    h = (_select_rows(nlt, x_ref, xc_ref) * (1.0 + scale) + shift).astype(BF16)
    g = jnp.dot(h, w_ref[:, _M_G:_M_QK], preferred_element_type=F32) + gb_ref[...]
    lsg = jnp.minimum(g, 0.0) - jnp.log1p(jnp.exp(-jnp.abs(g)))
    r = lax.broadcasted_iota(jnp.int32, (tm, tm), 0)
    cidx = lax.broadcasted_iota(jnp.int32, (tm, tm), 1)
    same = (r // M_SCAN_CHUNK) == (cidx // M_SCAN_CHUNK)
    tril = jnp.where(jnp.logical_and(same, cidx <= r), 1.0, 0.0)
    triu = jnp.where(jnp.logical_and(same, cidx >= r), 1.0, 0.0)
    pre = jnp.dot(tril, lsg, precision=HIGHEST, preferred_element_type=F32)
    suf = jnp.dot(triu, lsg, precision=HIGHEST, preferred_element_type=F32)
    lane = lax.broadcasted_iota(jnp.int32, g.shape, 1)
    grp = lane // H_M
    gt1_ref[0] = jnp.where(grp == 1, pre, jnp.where(grp == 3, suf, g)).T
    bsum = jnp.where(grp == 1, pre, jnp.where(grp == 3, suf, 0.0))
    b_at_i = pltpu.roll(bsum, LANES - H_M, axis=1)
    r_gate = g - b_at_i
    pos = lax.broadcasted_iota(jnp.int32, g.shape, 0) % M_SCAN_CHUNK
    pmax = r_gate
    smax = r_gate
    step = 1
    while step < M_SCAN_CHUNK:
        pmax = jnp.where(pos >= step, jnp.maximum(pmax, pltpu.roll(pmax, step, axis=0)), pmax)
        smax = jnp.where(pos < M_SCAN_CHUNK - step, jnp.maximum(smax, pltpu.roll(smax, tm - step, axis=0)), smax)
        step *= 2
    gc_ref[0] = r_gate * LOG2E
    gt2_ref[0] = (b_at_i + jnp.where(grp == 0, pmax, smax)).T

    y = jnp.dot(h, w_ref[:, _M_QK:_M_END], preferred_element_type=F32)
    hp = (_select_rows(nlt, xp_ref, xpc_ref) * (1.0 + scale) + shift).astype(BF16)
    hn = (_select_rows(nlt, xn_ref, xnc_ref) * (1.0 + scale) + shift).astype(BF16)
    wqk = w_ref[:, _M_QK:_M_V]
    up = jnp.dot(hp, wqk, preferred_element_type=F32)[7:8, :]
    un = jnp.dot(hn, wqk, preferred_element_type=F32)[0:1, :]
    is_start = functools.reduce(jnp.logical_or, [i == s for s in seg_starts])
    is_end = functools.reduce(jnp.logical_or, [i == e for e in seg_ends])
    up = jnp.where(is_start, 0.0, up)
    un = jnp.where(is_end, 0.0, un)

    u = y[:, 0:_M_V - _M_QK]
    rows = lax.broadcasted_iota(jnp.int32, u.shape, 0)
    u_prev = jnp.where(rows == 0, up, pltpu.roll(u, 1, axis=0))
    u_next = jnp.where(rows == tm - 1, un, pltpu.roll(u, tm - 1, axis=0))
    qk = _silu(cw_ref[0:1, :] * u_prev + cw_ref[1:2, :] * u + cw_ref[2:3, :] * u_next + cb_ref[...])
    nq = H_M * M_DK
    for grp in range(nq // LANES):
        sl = slice(grp * LANES, (grp + 1) * LANES)
        qt_ref[0, sl, :] = qk[:, sl].T.astype(BF16)
    for hd in range(H_M):
        sl = slice(hd * LANES, (hd + 1) * LANES)
        vt_ref[0, sl, :] = y[:, _M_V - _M_QK + hd * LANES:_M_V - _M_QK + (hd + 1) * LANES].T.astype(BF16)
    k_ref[0] = (qk[:, nq:] * (M_DK ** -0.5)).astype(BF16)
    og_ref[0] = y[:, _M_OG - _M_QK:].astype(BF16)


def _m_in_proj(x_parts, modsel, w_ext, conv_w, conv_b, gate_b, nlt, t):
    b, d = modsel.shape[0], modsel.shape[-1]
    tm = ROW_TILE
    nt = t // tm
    r8 = tm // 8
    lat, ctx, ctx_tile0 = x_parts
    c0, n8_lat, n8_ctx = ctx_tile0 * r8, nlt * r8, (nt - nlt) * r8
    halo = lambda off, lo, n8: (lambda bi, i: (bi, lo + jnp.clip(i * r8 + off, 0, n8 - 1), 0))
    halo_ctx = lambda off: (lambda bi, i: (bi, c0 + jnp.clip((i - nlt) * r8 + off, 0, n8_ctx - 1), 0))
    x_specs, x_args = _split_rows_specs(x_parts, nlt, tm)
    x_specs += [pl.BlockSpec((1, 8, d), halo(-1, 0, n8_lat)), pl.BlockSpec((1, 8, d), halo_ctx(-1)),
                pl.BlockSpec((1, 8, d), halo(r8, 0, n8_lat)), pl.BlockSpec((1, 8, d), halo_ctx(r8))]
    x_args += [lat, ctx, lat, ctx]
    row = lambda bi, i: (bi, i, 0)
    col = lambda bi, i: (bi, 0, i)
    nk, n = H_M * M_DK, H_M * M_DV
    out_shapes = [jax.ShapeDtypeStruct((b, nk, t), BF16), jax.ShapeDtypeStruct((b, t, nk), BF16),
                  jax.ShapeDtypeStruct((b, n, t), BF16), jax.ShapeDtypeStruct((b, t, n), BF16)] + \
                 [jax.ShapeDtypeStruct((b, LANES, t), F32)] * 2 + [jax.ShapeDtypeStruct((b, t, LANES), F32)]
    out_specs = [pl.BlockSpec((1, nk, tm), col), pl.BlockSpec((1, tm, nk), row),
                 pl.BlockSpec((1, n, tm), col), pl.BlockSpec((1, tm, n), row)] + \
                [pl.BlockSpec((1, LANES, tm), col)] * 2 + [pl.BlockSpec((1, tm, LANES), row)]
    kern = functools.partial(_m_in_kernel, nlt=nlt, seg_starts=(0, nlt), seg_ends=(nlt - 1, nt - 1))
    return pl.pallas_call(
        kern,
        grid=(b, nt),
        in_specs=x_specs + [
                  pl.BlockSpec((1, 1, 6, d), lambda bi, i: (bi, (i >= nlt).astype(jnp.int32), 0, 0)),
                  _const_spec(w_ext.shape), _const_spec(conv_w.shape), _const_spec(conv_b.shape),
                  _const_spec(gate_b.shape)],
        out_specs=out_specs,
        out_shape=out_shapes,
        compiler_params=_cparams(("parallel", "arbitrary")),
        name="m_in_proj",
    )(*x_args, modsel, w_ext, conv_w, conv_b, gate_b)


def _mlstm_gates(gc, gt1, gt2, hd, backward, m_prev):
    L = M_SCAN_CHUNK
    gi = 2 * H_M if backward else 0
    i_row = gt1[gi + hd:gi + hd + 1, :]
    b_row = gt1[gi + H_M + hd:gi + H_M + hd + 1, :]
    mloc_row = gt2[gi + hd:gi + hd + 1, :]
    r_col = gc[:, gi + hd:gi + hd + 1]
    inter = b_row + m_prev
    m_t = jnp.maximum(inter, mloc_row)
    e = 0 if backward else L - 1
    b_end = b_row[:, e:e + 1]
    m_new = jnp.maximum(b_end + m_prev, mloc_row[:, e:e + 1])
    return i_row, b_row, r_col, inter, m_t, b_end, m_new


def _mlstm_kernel(*refs):
    ins, (hf_ref, hb_ref, cstate, mstate, sc_scr) = refs[:12], refs[12:]
    c = pl.program_id(1)
    L = M_SCAN_CHUNK

    @pl.when(c == 0)
    def _():
        cstate[...] = jnp.zeros_like(cstate)
        mstate[...] = jnp.zeros_like(mstate)

    chains = [(d, hd) for d in range(2) for hd in range(H_M)]

    def operands(d, hd):
        k_ref, qt_ref, vt_ref, gc_ref, gt1_ref, gt2_ref = ins[6 * d:6 * d + 6]
        pair = slice((hd // 2) * LANES, (hd // 2 + 1) * LANES)
        k_pair = k_ref[0, :, pair]
        lane = lax.broadcasted_iota(jnp.int32, k_pair.shape, 1)
        k_own = jnp.where((lane // M_DK) == (hd % 2), k_pair, jnp.zeros_like(k_pair))
        gates = _mlstm_gates(gc_ref[0], gt1_ref[0], gt2_ref[0], hd, d == 1, mstate[d * H_M + hd][0:1, 0:1])
        return k_own, qt_ref[0, pair, :], vt_ref[0, hd * LANES:(hd + 1) * LANES, :], gates

    ones_rows = jnp.ones((M_ONES_ROWS, L), BF16)
    s_idx = lax.broadcasted_iota(jnp.int32, (L, L), 0)
    t_idx = lax.broadcasted_iota(jnp.int32, (L, L), 1)

    for ch, (d, hd) in enumerate(chains):
        k_own, qt, _, (_, b_row, r_col, _, m_t, _, _) = operands(d, hd)
        visible = (s_idx >= t_idx) if d == 1 else (s_idx <= t_idx)
        dt = jnp.exp2(jnp.where(visible, r_col + (b_row - m_t) * LOG2E, NEG))
        sc_scr[ch] = (jnp.dot(k_own, qt, preferred_element_type=F32) * dt).astype(BF16)

    for ch, (d, hd) in enumerate(chains):
        _, qt, vt, (_, _, _, inter, m_t, _, _) = operands(d, hd)
        vext_t = jnp.concatenate([vt, ones_rows], axis=0)
        nd_t = jnp.dot(vext_t, sc_scr[ch], preferred_element_type=F32) \
            + jnp.dot(cstate[ch].astype(BF16), qt, preferred_element_type=F32) * jnp.exp(inter - m_t)
        den = jnp.broadcast_to(nd_t[LANES:LANES + 1, :], (LANES, L))
        h_t = nd_t[:LANES] / jnp.maximum(jnp.abs(den), jnp.exp(-m_t))
        (hb_ref if d else hf_ref)[0, :, hd * LANES:(hd + 1) * LANES] = h_t.T

    for ch, (d, hd) in enumerate(chains):
        k_own, _, vt, (i_row, b_row, _, _, _, b_end, m_new) = operands(d, hd)
        m_prev = mstate[ch][0:1, 0:1]
        vext_t = jnp.concatenate([vt, ones_rows], axis=0)
        w_row = jnp.exp(b_end - b_row + i_row - m_new)
        cstate[ch] = jnp.exp(b_end + m_prev - m_new) * cstate[ch] \
            + jnp.dot((vext_t.astype(F32) * w_row).astype(BF16), k_own, preferred_element_type=F32)
        mstate[ch] = jnp.broadcast_to(m_new, mstate.shape[1:])


def _mlstm(k, qt, vt, gc, gt1, gt2, n_lat):
    b, t, nk = k.shape
    n = vt.shape[1]
    L = M_SCAN_CHUNK
    nb = t // L
    fwd, bwd, fwd_t, bwd_t = _scan_maps(n_lat // L, nb - n_lat // L)

    def specs(rmap, tmap):
        return [pl.BlockSpec((1, L, nk), rmap), pl.BlockSpec((1, nk, L), tmap), pl.BlockSpec((1, n, L), tmap),
                pl.BlockSpec((1, L, LANES), rmap), pl.BlockSpec((1, 4 * H_M, L), tmap),
                pl.BlockSpec((1, 4 * H_M, L), tmap)]

    out = jax.ShapeDtypeStruct((b, t, H_M * M_DV), F32)
    args = (k, qt, vt, gc, gt1, gt2)
    return pl.pallas_call(
        _mlstm_kernel,
        grid=(b, nb),
        in_specs=specs(fwd, fwd_t) + specs(bwd, bwd_t),
        out_specs=[pl.BlockSpec((1, L, H_M * M_DV), fwd), pl.BlockSpec((1, L, H_M * M_DV), bwd)],
        out_shape=[out, out],
        scratch_shapes=[pltpu.VMEM((2 * H_M, LANES + M_ONES_ROWS, LANES), F32),
                        pltpu.VMEM((2 * H_M, 8, LANES), F32),
                        pltpu.VMEM((2 * H_M, L, L), BF16)],
        compiler_params=_cparams(("parallel", "arbitrary")),
        name="mlstm_scan",
    )(*args, *args)


def _rope_tables(n_lat, ctx_len, d, lane0):
    rows = n_lat // GRID_W
    row = np.repeat(np.arange(rows, dtype=np.float64), GRID_W)
    col = np.tile(np.arange(GRID_W, dtype=np.float64), rows)
    da = d // 2
    inv = ROPE_BASE ** (-np.arange(0, da, 2, dtype=np.float64) / da)
    ar = row[:, None] * inv
    ac = col[:, None] * inv
    tab = np.zeros((2, n_lat + ctx_len, LANES), np.float32)
    tab[0] = 1.0
    tab[0, :n_lat, lane0:lane0 + d] = np.concatenate([np.cos(ar), np.cos(ar), np.cos(ac), np.cos(ac)], axis=-1)
    tab[1, :n_lat, lane0:lane0 + d] = np.concatenate([-np.sin(ar), np.sin(ar), -np.sin(ac), np.sin(ac)], axis=-1)
    return jnp.asarray(tab)


def _pad_heads(w, n_heads, width):
    k = w.shape[0]
    w = w.reshape(k, n_heads, width)
    return jnp.pad(w, ((0, 0), (0, 0), (0, LANES - width))).reshape(k, n_heads * LANES)


def _prep_ab_weights(w_in, w_uq, w_ukv):
    o1 = MLA_Q_LORA
    o2 = o1 + MLA_KV_LORA
    o3 = o2 + MLA_ROPE
    o4 = o3 + H_RET * RET_DK
    o5 = o4 + H_RET * RET_DK
    o6 = o5 + H_RET * RET_DV
    cq, ckv, kr, rq, rk, rv, rg = (w_in[:, :o1], w_in[:, o1:o2], w_in[:, o2:o3], w_in[:, o3:o4],
                                   w_in[:, o4:o5], w_in[:, o5:o6], w_in[:, o6:])
    kr_blk = jnp.pad(kr, ((0, 0), (MLA_NOPE, LANES - MLA_NOPE - MLA_ROPE)))
    w_ext = jnp.concatenate([cq, ckv, rq, rk, rv, rg, kr_blk], axis=1)

    wuq_ext = _pad_heads(w_uq, H_MLA, MLA_NOPE + MLA_ROPE)
    ukv = w_ukv.reshape(-1, H_MLA, MLA_NOPE + MLA_V)
    k_nope = _pad_heads(ukv[..., :MLA_NOPE].reshape(-1, H_MLA * MLA_NOPE), H_MLA, MLA_NOPE)
    v_w = ukv[..., MLA_NOPE:]
    zeros = jnp.zeros_like(v_w)
    even = (np.arange(H_MLA) % 2 == 0)[None, :, None]
    v_cols = jnp.concatenate([jnp.where(even, v_w, zeros), jnp.where(even, zeros, v_w)], axis=-1)
    wukv_ext = jnp.concatenate([k_nope, v_cols.reshape(-1, H_MLA * LANES)], axis=1)
    return w_ext.astype(BF16), wuq_ext.astype(BF16), wukv_ext.astype(BF16)


def _prep_m_weights(w_in, conv_w, conv_b, gate_b):
    o1 = 2 * H_M * M_DK
    o2 = o1 + H_M * M_DV
    o3 = o2 + H_M * M_DV
    g_pad = jnp.pad(w_in[:, o3:], ((0, 0), (0, LANES - 4 * H_M)))
    w_ext = jnp.concatenate([g_pad, w_in[:, :o3]], axis=1).astype(BF16)
    gb = jnp.pad(gate_b.reshape(1, 4 * H_M), ((0, 0), (0, LANES - 4 * H_M)))
    return w_ext, conv_w, conv_b[None, :], gb


def kernel(x, c, ctx, c_ctx, mod_w, mod_b, ln_g, ln_b, ffn_w_in, ffn_w_out, ab_w_in, mla_q_norm, mla_w_uq,
           mla_kv_norm, mla_w_ukv, ret_log_decay, ab_w_out, m_w_in, m_conv_w, m_conv_b, m_gate_b, m_norm_g, m_w_out):
    b, s, d = x.shape
    ctx_len = ctx.shape[1]
    depth = mod_w.shape[0]
    assert ctx_len % ROW_TILE == 0 and s % ROW_TILE == 0 and s % GRID_W == 0 and b + 1 <= 8
    nlt = s // ROW_TILE
    alpha = (2 * depth) ** 0.25

    c_rows = jnp.zeros((8, d), F32).at[:b].set(c).at[b].set(c_ctx)
    mods = _modulation(c_rows, mod_w, mod_b).reshape(depth, 8, 6, d)
    tab_mla = _rope_tables(s, ctx_len, MLA_ROPE, MLA_NOPE)
    tab_ret = _rope_tables(s, ctx_len, RET_DK, 0)

    t = s + ctx_len
    x_lat, x_ctx = x, ctx
    for l in range(depth):
        last = l == depth - 1
        j = l // 2
        modsel = jnp.stack([mods[l, :b], jnp.broadcast_to(mods[l, b], (b, 6, d))], axis=1)
        ln = jnp.stack([ln_g[l, 0], ln_b[l, 0], ln_g[l, 1], ln_b[l, 1]])
        w1 = ffn_w_in[l].astype(BF16)
        w2 = ffn_w_out[l].astype(BF16)
        x_parts = (x_lat, x_ctx, 0)
        if l % 2 == 0:
            w_ext, wuq_ext, wukv_ext = _prep_ab_weights(ab_w_in[j], mla_w_uq[j], mla_w_ukv[j])
            q, kt, v, rq, rkt, rv, rg = _ab_in_proj(x_parts, modsel, w_ext, mla_q_norm[j][None, :],
                                                    mla_kv_norm[j][None, :], wuq_ext, wukv_ext,
                                                    tab_mla, tab_ret, nlt, t)
            att_lat = _attention(q, kt, v, s, False)
            att_ctx = None if last else _attention(q, kt, v, s, True)
            ret_f, ret_b = _retention(rq, rkt, rv, ret_log_decay[j], s)
            post_fn, acts, consts, w_out = _ab_mix, [ret_f, ret_b, rg], [], ab_w_out[j]
        else:
            w_ext, cw, cb, gb = _prep_m_weights(m_w_in[j], m_conv_w[j], m_conv_b[j], m_gate_b[j])
            qt, k, vt, og, gt1, gt2, gc = _m_in_proj(x_parts, modsel, w_ext, cw, cb, gb, nlt, t)
            hf, hb = _mlstm(k, qt, vt, gc, gt1, gt2, s)
            post_fn, acts, consts, w_out = _m_mix, [hf, hb, og], [m_norm_g[j][None, :]], m_w_out[j]
            att_lat = att_ctx = None

        def post(tm, tile0, n_tiles, for_ctx):
            pair = (lambda a_lat, a_ctx: (a_lat, a_ctx, 0) if for_ctx else (a_lat, None, 0))
            att = [] if att_lat is None else [pair(att_lat, att_ctx)]
            return _post(post_fn, pair(x_lat, x_ctx), modsel, att + acts, consts, w_out.astype(BF16),
                         ln, w1, w2, s, tm, tile0, n_tiles, alpha)

        new_lat = post(POST_TILE, 0, s // POST_TILE, False)
        if not last:
            x_ctx = post(ROW_TILE, nlt, ctx_len // ROW_TILE, True)
        x_lat = new_lat
    return x_lat
```

```python
import functools

import jax
import jax.numpy as jnp
import numpy as np
from jax import lax
from jax.experimental import pallas as pl
from jax.experimental.pallas import tpu as pltpu

F32 = jnp.float32
BF16 = jnp.bfloat16
HIGHEST = lax.Precision.HIGHEST

GRID_W = 64
ROPE_BASE = 10000.0
EPS = 1e-5
NEG = -1e30
LOG2E = 1.4426950408889634
H_MLA, MLA_NOPE, MLA_ROPE, MLA_V = 8, 64, 32, 64
MLA_Q_LORA, MLA_KV_LORA = 384, 256
H_RET, RET_DK, RET_DV = 4, 128, 128
H_M, M_DK, M_DV = 8, 64, 128

LANES = 128
ROW_TILE = 256
POST_TILE = 512
RET_CHUNK = 128
RET_BLOCK = 256
ATT_KEY_CHUNK = 512
ATT_Q_TILE = 1024
M_SCAN_CHUNK = 256
M_ONES_ROWS = 16
VMEM_LIMIT = 56 * 1024 * 1024


def _cparams(sem, flags=None):
    return pltpu.CompilerParams(dimension_semantics=sem, vmem_limit_bytes=VMEM_LIMIT, flags=flags)


def _const_spec(shape):
    nd = len(shape)
    return pl.BlockSpec(shape, lambda *_: (0,) * nd, pipeline_mode=pl.Buffered(1))


def _split_rows_specs(parts, nlt, tm, tile0=0):
    lat, ctx, ctx_tile0 = parts
    n = lat.shape[-1]
    specs = [pl.BlockSpec((1, tm, n), lambda bi, i: (bi, jnp.minimum(i + tile0, nlt - 1), 0)),
             pl.BlockSpec((1, tm, n), lambda bi, i: (bi, ctx_tile0 + jnp.maximum(i + tile0 - nlt, 0), 0))]
    return specs, [lat, ctx]


def _select_rows(nlt, lat_ref, ctx_ref, tile0=0, rows=slice(None)):
    return jnp.where(pl.program_id(1) + tile0 >= nlt, ctx_ref[0, rows, :], lat_ref[0, rows, :])


def _silu(v):
    return v * jax.nn.sigmoid(v)


def _layer_norm_rows(v, g, b):
    mu = jnp.mean(v, axis=-1, keepdims=True)
    d = v - mu
    var = jnp.mean(d * d, axis=-1, keepdims=True)
    return d * lax.rsqrt(var + EPS) * g + b


def _head_norm_lanes(v):
    outs = []
    for h in range(v.shape[-1] // LANES):
        blk = v[:, h * LANES:(h + 1) * LANES]
        mu = jnp.mean(blk, axis=-1, keepdims=True)
        d = blk - mu
        var = jnp.mean(d * d, axis=-1, keepdims=True)
        outs.append(d * lax.rsqrt(var + EPS))
    return outs


def _mod_kernel(c_ref, w_ref, b_ref, o_ref):
    sc = _silu(c_ref[...])
    o_ref[0] = jnp.dot(sc, w_ref[0], precision=HIGHEST, preferred_element_type=F32) + b_ref[0]


def _modulation(c_rows, mod_w, mod_b):
    depth, d, n = mod_w.shape
    tn = 1536
    return pl.pallas_call(
        _mod_kernel,
        grid=(depth, n // tn),
        in_specs=[pl.BlockSpec((8, d), lambda l, j: (0, 0)),
                  pl.BlockSpec((1, d, tn), lambda l, j: (l, 0, j)),
                  pl.BlockSpec((1, 1, tn), lambda l, j: (l, 0, j))],
        out_specs=pl.BlockSpec((1, 8, tn), lambda l, j: (l, 0, j)),
        out_shape=jax.ShapeDtypeStruct((depth, 8, n), F32),
        compiler_params=_cparams(("arbitrary", "arbitrary")),
        name="modulation",
    )(c_rows, mod_w, mod_b.reshape(depth, 1, n))


_A_CQ, _A_CKV, _A_RQ, _A_RK, _A_RV, _A_RG, _A_KR, _A_END = 0, 384, 640, 1152, 1664, 2176, 2688, 2816


def _swap_pairs(v, width):
    lane = lax.broadcasted_iota(jnp.int32, v.shape, 1)
    return jnp.where((lane // width) % 2 == 0, pltpu.roll(v, LANES - width, axis=1), pltpu.roll(v, width, axis=1))


def _ab_in_kernel(x_ref, xc_ref, mod_ref, w_ref, qn_ref, kvn_ref, wuq_ref, wukv_ref, tm_ref, tr_ref,
                  q_ref, kt_ref, v_ref, rq_ref, rkt_ref, rv_ref, rg_ref, *, nlt):
    x = _select_rows(nlt, x_ref, xc_ref)
    shift = mod_ref[0, 0, 0:1, :]
    scale = mod_ref[0, 0, 1:2, :]
    h = (x * (1.0 + scale) + shift).astype(BF16)
    y = jnp.dot(h, w_ref[...], preferred_element_type=F32)

    cq = y[:, _A_CQ:_A_CKV]
    ckv = y[:, _A_CKV:_A_RQ]
    ncq = (cq * lax.rsqrt(jnp.mean(cq * cq, axis=-1, keepdims=True) + EPS) * qn_ref[...]).astype(BF16)
    nckv = (ckv * lax.rsqrt(jnp.mean(ckv * ckv, axis=-1, keepdims=True) + EPS) * kvn_ref[...]).astype(BF16)
    q2 = jnp.dot(ncq, wuq_ref[...], preferred_element_type=F32)
    kv = jnp.dot(nckv, wukv_ref[...], preferred_element_type=F32)

    cos_m = tm_ref[0]
    sin_m = tm_ref[1]
    q_scale = (MLA_NOPE + MLA_ROPE) ** -0.5 * LOG2E
    kr = y[:, _A_KR:_A_END]
    k_rope = kr * cos_m + _swap_pairs(kr, MLA_ROPE // 4) * sin_m
    nq = H_MLA * LANES
    for hd in range(H_MLA):
        sl = slice(hd * LANES, (hd + 1) * LANES)
        qh = q2[:, sl] * cos_m + _swap_pairs(q2[:, sl], MLA_ROPE // 4) * sin_m
        q_ref[0, :, sl] = (qh * q_scale).astype(BF16)
        kt_ref[0, sl, :] = (kv[:, sl] + k_rope).T.astype(BF16)
    lane = lax.broadcasted_iota(jnp.int32, (1, nq), 1)
    ones_half = jnp.where(((lane // MLA_V) % 2) != ((lane // LANES) % 2), 1.0, 0.0)
    v_ref[0] = (kv[:, nq:] + ones_half).astype(BF16)

    cos_r = tr_ref[0]
    sin_r = tr_ref[1]
    k_scale = RET_DK ** -0.5
    for hd in range(H_RET):
        sl = slice(hd * LANES, (hd + 1) * LANES)
        rq = y[:, _A_RQ + hd * LANES:_A_RQ + (hd + 1) * LANES]
        rk = y[:, _A_RK + hd * LANES:_A_RK + (hd + 1) * LANES]
        rq = rq * cos_r + _swap_pairs(rq, RET_DK // 4) * sin_r
        rk = rk * cos_r + _swap_pairs(rk, RET_DK // 4) * sin_r
        rq_ref[0, :, sl] = rq.astype(BF16)
        rkt_ref[0, sl, :] = (rk * k_scale).T.astype(BF16)
    rv_ref[0] = y[:, _A_RV:_A_RG].astype(BF16)
    rg_ref[0] = y[:, _A_RG:_A_KR].astype(BF16)


def _ab_in_proj(x_parts, modsel, w_ext, q_norm, kv_norm, wuq_ext, wukv_ext, tab_mla, tab_ret, nlt, t):
    b, d = modsel.shape[0], modsel.shape[-1]
    tm = ROW_TILE
    row = lambda bi, i: (bi, i, 0)
    col = lambda bi, i: (bi, 0, i)
    n_ret = H_RET * RET_DK
    widths = (H_MLA * LANES, H_MLA * LANES, H_MLA * LANES, n_ret, n_ret, n_ret, n_ret)
    transposed = (False, True, False, False, True, False, False)
    out_shapes = [jax.ShapeDtypeStruct((b, w, t) if tr else (b, t, w), BF16) for w, tr in zip(widths, transposed)]
    out_specs = [pl.BlockSpec((1, w, tm), col) if tr else pl.BlockSpec((1, tm, w), row)
                 for w, tr in zip(widths, transposed)]
    x_specs, x_args = _split_rows_specs(x_parts, nlt, tm)
    return pl.pallas_call(
        functools.partial(_ab_in_kernel, nlt=nlt),
        grid=(b, t // tm),
        in_specs=x_specs + [
                  pl.BlockSpec((1, 1, 6, d), lambda bi, i: (bi, (i >= nlt).astype(jnp.int32), 0, 0)),
                  _const_spec(w_ext.shape), _const_spec(q_norm.shape), _const_spec(kv_norm.shape),
                  _const_spec(wuq_ext.shape), _const_spec(wukv_ext.shape),
                  pl.BlockSpec((2, tm, LANES), lambda bi, i: (0, i, 0)),
                  pl.BlockSpec((2, tm, LANES), lambda bi, i: (0, i, 0))],
        out_specs=out_specs,
        out_shape=out_shapes,
        compiler_params=_cparams(("parallel", "arbitrary")),
        name="ab_in_proj",
    )(*x_args, modsel, w_ext, q_norm, kv_norm, wuq_ext, wukv_ext, tab_mla, tab_ret)


def _attn_kernel(q_ref, kt_ref, v_ref, o_ref, s_scr, m_scr, acc_scr, *, tk, ctx_key0, ctx_len, n_lat_chunks):
    tq = q_ref.shape[1]
    m_scr[...] = jnp.full(m_scr.shape, NEG, F32)
    acc_scr[...] = jnp.zeros_like(acc_scr)

    def scores(slot, r0, size):
        for hh in range(2):
            sl = slice(hh * LANES, (hh + 1) * LANES)
            s_scr[slot, hh, :, 0:size] = jnp.dot(q_ref[0, :, sl], kt_ref[0, sl, pl.ds(r0, size)],
                                                 preferred_element_type=F32)

    def accumulate(slot, r0, size):
        for hh in range(2):
            sl = slice(hh * LANES, (hh + 1) * LANES)
            s = s_scr[slot, hh, :, 0:size]
            m = m_scr[hh]
            m_new = jnp.maximum(m, jnp.max(s, axis=-1, keepdims=True))
            m_scr[hh] = m_new
            p = jnp.exp2(s - m_new[:, 0:1]).astype(BF16)
            acc_scr[hh] = jnp.exp2(m - m_new) * acc_scr[hh] \
                + jnp.dot(p, v_ref[0, pl.ds(r0, size), sl], preferred_element_type=F32)

    def lat_row(c):
        return pl.multiple_of(jnp.minimum(c, n_lat_chunks - 1) * tk, tk)

    scores(1, ctx_key0, ctx_len)
    if n_lat_chunks:
        scores(0, 0, tk)
    accumulate(1, ctx_key0, ctx_len)
    if n_lat_chunks:

        def body(j, carry):
            scores(1, lat_row(2 * j + 1), tk)
            accumulate(0, lat_row(2 * j), tk)
            scores(0, lat_row(2 * j + 2), tk)
            accumulate(1, lat_row(2 * j + 1), tk)
            return carry

        lax.fori_loop(0, n_lat_chunks // 2, body, 0)
    outs = [acc_scr[hh] / pltpu.roll(acc_scr[hh], MLA_V, axis=1) for hh in range(2)]
    lane = lax.broadcasted_iota(jnp.int32, (tq, LANES), 1)
    o_ref[0] = jnp.where(lane < MLA_V, outs[0], outs[1]).astype(BF16)


def _attention(q, kt, v, n_lat, for_ctx):
    b, t, _ = q.shape
    ctx_len = t - n_lat
    tk = ATT_KEY_CHUNK
    assert n_lat % (2 * tk) == 0 and n_lat % ATT_Q_TILE == 0 and ctx_len % ROW_TILE == 0
    tq, n_rows, row0 = (ROW_TILE, ctx_len, n_lat // ROW_TILE) if for_ctx else (ATT_Q_TILE, n_lat, 0)
    pair = 2 * LANES
    if for_ctx:
        assert n_lat % ctx_len == 0
        key_blk, n_keys, key0 = n_lat // ctx_len, ctx_len, 0
    else:
        key_blk, n_keys, key0 = 0, t, n_lat
    kern = functools.partial(_attn_kernel, tk=tk, ctx_key0=key0, ctx_len=ctx_len,
                             n_lat_chunks=0 if for_ctx else n_lat // tk)
    return pl.pallas_call(
        kern,
        grid=(b, H_MLA // 2, n_rows // tq),
        in_specs=[pl.BlockSpec((1, tq, pair), lambda bi, hp, i: (bi, i + row0, hp)),
                  pl.BlockSpec((1, pair, n_keys), lambda bi, hp, i: (bi, hp, key_blk)),
                  pl.BlockSpec((1, n_keys, pair), lambda bi, hp, i: (bi, key_blk, hp))],
        out_specs=pl.BlockSpec((1, tq, LANES), lambda bi, hp, i: (bi, i, hp)),
        out_shape=jax.ShapeDtypeStruct((b, n_rows, H_MLA * MLA_V), BF16),
        scratch_shapes=[pltpu.VMEM((2, 2, tq, max(tk, ctx_len)), F32), pltpu.VMEM((2, tq, LANES), F32),
                        pltpu.VMEM((2, tq, LANES), F32)],
        compiler_params=_cparams(("parallel", "parallel", "arbitrary")),
        name="mla_attention_ctx" if for_ctx else "mla_attention",
    )(q, kt, v)


def _ret_kernel(ld_ref, qf_ref, ktf_ref, vf_ref, qb_ref, ktb_ref, vb_ref, of_ref, ob_ref,
                state, dmat, dq, wk, gl, s_scr):
    c = pl.program_id(1)
    L = RET_CHUNK

    @pl.when(c == 0)
    def _():
        state[...] = jnp.zeros_like(state)
        row = lax.broadcasted_iota(jnp.int32, (L, L), 0).astype(F32)
        col = lax.broadcasted_iota(jnp.int32, (L, L), 1).astype(F32)
        qpos = lax.broadcasted_iota(jnp.int32, (L, RET_DK), 0).astype(F32)
        kpos = lax.broadcasted_iota(jnp.int32, (RET_DK, L), 1).astype(F32)
        for d in range(2):
            for hd in range(H_RET):
                lg = ld_ref[d, hd]
                if d == 0:
                    rel = row - col
                    dq[d, hd] = jnp.exp(lg * (qpos + 1.0))
                    wk[d, hd] = jnp.exp(lg * (L - 1.0 - kpos))
                else:
                    rel = col - row
                    dq[d, hd] = jnp.exp(lg * (L - qpos))
                    wk[d, hd] = jnp.exp(lg * kpos)
                dmat[d, hd] = jnp.where(rel >= 0, jnp.exp(lg * jnp.maximum(rel, 0.0)), 0.0)
                gl[d, hd] = jnp.exp(jnp.zeros((RET_DK, RET_DV), F32) + lg * L)

    streams = ((qf_ref, ktf_ref, vf_ref, of_ref), (qb_ref, ktb_ref, vb_ref, ob_ref))
    n_sub = qf_ref.shape[1] // L
    for k in range(n_sub):
        rows = [slice(k * L, (k + 1) * L), slice((n_sub - 1 - k) * L, (n_sub - k) * L)]
        chains = [(d, hd, slice(hd * LANES, (hd + 1) * LANES), rows[d]) + streams[d]
                  for d in range(2) for hd in range(H_RET)]
        for d, hd, sl, r, q_ref, kt_ref, _, _ in chains:
            s = jnp.dot(q_ref[0, r, sl], kt_ref[0, sl, r], preferred_element_type=F32) * dmat[d, hd]
            s_scr[d, hd] = s.astype(BF16)
        for d, hd, sl, r, q_ref, _, v_ref, o_ref in chains:
            qd = (q_ref[0, r, sl].astype(F32) * dq[d, hd]).astype(BF16)
            o_ref[0, r, sl] = jnp.dot(s_scr[d, hd], v_ref[0, r, sl], preferred_element_type=F32) \
                + jnp.dot(qd, state[d, hd].astype(BF16), preferred_element_type=F32)
        for d, hd, sl, r, _, kt_ref, v_ref, _ in chains:
            kw = (kt_ref[0, sl, r].astype(F32) * wk[d, hd]).astype(BF16)
            state[d, hd] = state[d, hd] * gl[d, hd] + jnp.dot(kw, v_ref[0, r, sl], preferred_element_type=F32)


def _scan_maps(n_lat_blocks, n_ctx_blocks):
    nlb, ncb = n_lat_blocks, n_ctx_blocks
    fwd_blk = lambda c: jnp.where(c < ncb, nlb + c, c - ncb)
    bwd_blk = lambda c: nlb + ncb - 1 - c
    fwd = lambda bi, c: (bi, fwd_blk(c), 0)
    bwd = lambda bi, c: (bi, bwd_blk(c), 0)
    fwd_t = lambda bi, c: (bi, 0, fwd_blk(c))
    bwd_t = lambda bi, c: (bi, 0, bwd_blk(c))
    return fwd, bwd, fwd_t, bwd_t


def _retention(rq, rkt, rv, log_decay, n_lat):
    b, t, n = rq.shape
    L, blk = RET_CHUNK, RET_BLOCK
    nb = t // blk
    fwd, bwd, fwd_t, bwd_t = _scan_maps(n_lat // blk, nb - n_lat // blk)
    tile = pl.BlockSpec((1, blk, n), fwd)
    tile_b = pl.BlockSpec((1, blk, n), bwd)
    out = jax.ShapeDtypeStruct((b, t, n), F32)
    per_head = lambda rows, cols, dtype=F32: pltpu.VMEM((2, H_RET, rows, cols), dtype)
    scratch = [per_head(RET_DK, RET_DV),
               per_head(L, L),
               per_head(L, RET_DK),
               per_head(RET_DK, L),
               per_head(RET_DK, RET_DV),
               per_head(L, L, BF16)]
    return pl.pallas_call(
        _ret_kernel,
        grid=(b, nb),
        in_specs=[pl.BlockSpec(memory_space=pltpu.SMEM),
                  tile, pl.BlockSpec((1, n, blk), fwd_t), tile,
                  tile_b, pl.BlockSpec((1, n, blk), bwd_t), tile_b],
        out_specs=[tile, tile_b],
        out_shape=[out, out],
        scratch_shapes=scratch,
        compiler_params=_cparams(("parallel", "arbitrary")),
        name="retention_scan",
    )(log_decay, rq, rkt, rv, rq, rkt, rv)


def _post_stages(xs, mixes, mod_ref, wo_ref, ln_ref, w1_ref, w2_ref, alpha):
    g1 = mod_ref[0, 0, 2:3, :]
    sh2 = mod_ref[0, 0, 3:4, :]
    sc2 = mod_ref[0, 0, 4:5, :]
    g2 = mod_ref[0, 0, 5:6, :]
    ys = [jnp.dot(m, wo_ref[...], preferred_element_type=F32) for m in mixes]
    x1s = [_layer_norm_rows(alpha * x + g1 * y, ln_ref[0:1, :], ln_ref[1:2, :]) for x, y in zip(xs, ys)]
    us = [jnp.dot((x1 * (1.0 + sc2) + sh2).astype(BF16), w1_ref[...], preferred_element_type=F32) for x1 in x1s]
    dff = w1_ref.shape[-1] // 2
    acts = [(_silu(u[:, dff:]) * u[:, :dff]).astype(BF16) for u in us]
    y2s = [jnp.dot(a, w2_ref[...], preferred_element_type=F32) for a in acts]
    return [_layer_norm_rows(alpha * x1 + g2 * y2, ln_ref[2:3, :], ln_ref[3:4, :]) for x1, y2 in zip(x1s, y2s)]


def _sub_tiles(o_ref):
    return [slice(r, r + ROW_TILE) for r in range(0, o_ref.shape[1], ROW_TILE)]


def _ab_post_kernel(x_ref, xc_ref, mod_ref, att_ref, attc_ref, rf_ref, rb_ref, rg_ref,
                    wo_ref, ln_ref, w1_ref, w2_ref, o_ref, *, alpha, nlt, tile0):
    subs = _sub_tiles(o_ref)
    mixes = []
    for rows in subs:
        ret = rf_ref[0, rows, :] + rb_ref[0, rows, :]
        gate = _silu(rg_ref[0, rows, :].astype(F32))
        parts = [_select_rows(nlt, att_ref, attc_ref, tile0, rows)]
        for hd, nh in enumerate(_head_norm_lanes(ret)):
            parts.append((gate[:, hd * LANES:(hd + 1) * LANES] * nh).astype(BF16))
        mixes.append(jnp.concatenate(parts, axis=-1))
    xs = [_select_rows(nlt, x_ref, xc_ref, tile0, rows) for rows in subs]
    for rows, out in zip(subs, _post_stages(xs, mixes, mod_ref, wo_ref, ln_ref, w1_ref, w2_ref, alpha)):
        o_ref[0, rows, :] = out


def _m_post_kernel(x_ref, xc_ref, mod_ref, hf_ref, hb_ref, og_ref, ng_ref,
                   wo_ref, ln_ref, w1_ref, w2_ref, o_ref, *, alpha, nlt, tile0):
    subs = _sub_tiles(o_ref)
    mixes = []
    for rows in subs:
        hs = hf_ref[0, rows, :] + hb_ref[0, rows, :]
        og = jax.nn.sigmoid(og_ref[0, rows, :].astype(F32))
        parts = []
        for hd, nh in enumerate(_head_norm_lanes(hs)):
            sl = slice(hd * LANES, (hd + 1) * LANES)
            parts.append((og[:, sl] * (nh * ng_ref[:, sl])).astype(BF16))
        mixes.append(jnp.concatenate(parts, axis=-1))
    xs = [_select_rows(nlt, x_ref, xc_ref, tile0, rows) for rows in subs]
    for rows, out in zip(subs, _post_stages(xs, mixes, mod_ref, wo_ref, ln_ref, w1_ref, w2_ref, alpha)):
        o_ref[0, rows, :] = out


def _post(kernel_fn, x_parts, modsel, acts, consts, w_out, ln, w1, w2, n_lat, tm, tile0, n_tiles, alpha):
    b, d = modsel.shape[0], modsel.shape[-1]
    assert n_lat % tm == 0 and tm % ROW_TILE == 0
    nlt = n_lat // tm
    row = lambda bi, i: (bi, i + tile0, 0)
    in_specs, args = _split_rows_specs(x_parts, nlt, tm, tile0)
    in_specs.append(pl.BlockSpec((1, 1, 6, d), lambda bi, i: (bi, (i + tile0 >= nlt).astype(jnp.int32), 0, 0)))
    args.append(modsel)
    for a in acts:
        if isinstance(a, tuple):
            sp, ar = _split_rows_specs(a, nlt, tm, tile0)
            in_specs += sp
            args += ar
        else:
            in_specs.append(pl.BlockSpec((1, tm, a.shape[-1]), row))
            args.append(a)
    weights = [*consts, w_out, ln, w1, w2]
    in_specs += [_const_spec(w.shape) for w in weights]
    return pl.pallas_call(
        functools.partial(kernel_fn, alpha=alpha, nlt=nlt, tile0=tile0),
        grid=(b, n_tiles),
        in_specs=in_specs,
        out_specs=pl.BlockSpec((1, tm, d), lambda bi, i: (bi, i, 0)),
        out_shape=jax.ShapeDtypeStruct((b, n_tiles * tm, d), F32),
        compiler_params=_cparams(("parallel", "arbitrary")),
        name="post_" + kernel_fn.__name__ + ("_ctx" if tile0 else ""),
    )(*args, *weights)


_M_G, _M_QK, _M_V, _M_OG, _M_END = 0, 128, 1152, 2176, 3200


def _m_in_kernel(x_ref, xc_ref, xp_ref, xpc_ref, xn_ref, xnc_ref, mod_ref, w_ref, cw_ref, cb_ref, gb_ref,
                 qt_ref, k_ref, vt_ref, og_ref, gt1_ref, gt2_ref, gc_ref, *, nlt, seg_starts, seg_ends):
    i = pl.program_id(1)
    tm = x_ref.shape[1]
    shift = mod_ref[0, 0, 0:1, :]
    scale = mod_ref[0, 0, 1:2, :]
    h = (_select_rows(nlt, x_ref, xc_ref) * (1.0 + scale) + shift).astype(BF16)
    g = jnp.dot(h, w_ref[:, _M_G:_M_QK], preferred_element_type=F32) + gb_ref[...]
    lsg = jnp.minimum(g, 0.0) - jnp.log1p(jnp.exp(-jnp.abs(g)))
    r = lax.broadcasted_iota(jnp.int32, (tm, tm), 0)
    cidx = lax.broadcasted_iota(jnp.int32, (tm, tm), 1)
    same = (r // M_SCAN_CHUNK) == (cidx // M_SCAN_CHUNK)
    tril = jnp.where(jnp.logical_and(same, cidx <= r), 1.0, 0.0)
    triu = jnp.where(jnp.logical_and(same, cidx >= r), 1.0, 0.0)
    pre = jnp.dot(tril, lsg, precision=HIGHEST, preferred_element_type=F32)
    suf = jnp.dot(triu, lsg, precision=HIGHEST, preferred_element_type=F32)
    lane = lax.broadcasted_iota(jnp.int32, g.shape, 1)
    grp = lane // H_M
    gt1_ref[0] = jnp.where(grp == 1, pre, jnp.where(grp == 3, suf, g)).T
    bsum = jnp.where(grp == 1, pre, jnp.where(grp == 3, suf, 0.0))
    b_at_i = pltpu.roll(bsum, LANES - H_M, axis=1)
    r_gate = g - b_at_i
    pos = lax.broadcasted_iota(jnp.int32, g.shape, 0) % M_SCAN_CHUNK
    pmax = r_gate
    smax = r_gate
    step = 1
    while step < M_SCAN_CHUNK:
        pmax = jnp.where(pos >= step, jnp.maximum(pmax, pltpu.roll(pmax, step, axis=0)), pmax)
        smax = jnp.where(pos < M_SCAN_CHUNK - step, jnp.maximum(smax, pltpu.roll(smax, tm - step, axis=0)), smax)
        step *= 2
    gc_ref[0] = r_gate * LOG2E
    gt2_ref[0] = (b_at_i + jnp.where(grp == 0, pmax, smax)).T

    y = jnp.dot(h, w_ref[:, _M_QK:_M_END], preferred_element_type=F32)
    hp = (_select_rows(nlt, xp_ref, xpc_ref) * (1.0 + scale) + shift).astype(BF16)
    hn = (_select_rows(nlt, xn_ref, xnc_ref) * (1.0 + scale) + shift).astype(BF16)
    wqk = w_ref[:, _M_QK:_M_V]
    up = jnp.dot(hp, wqk, preferred_element_type=F32)[7:8, :]
    un = jnp.dot(hn, wqk, preferred_element_type=F32)[0:1, :]
    is_start = functools.reduce(jnp.logical_or, [i == s for s in seg_starts])
    is_end = functools.reduce(jnp.logical_or, [i == e for e in seg_ends])
    up = jnp.where(is_start, 0.0, up)
    un = jnp.where(is_end, 0.0, un)

    u = y[:, 0:_M_V - _M_QK]
    rows = lax.broadcasted_iota(jnp.int32, u.shape, 0)
    u_prev = jnp.where(rows == 0, up, pltpu.roll(u, 1, axis=0))
    u_next = jnp.where(rows == tm - 1, un, pltpu.roll(u, tm - 1, axis=0))
    qk = _silu(cw_ref[0:1, :] * u_prev + cw_ref[1:2, :] * u + cw_ref[2:3, :] * u_next + cb_ref[...])
    nq = H_M * M_DK
    for grp in range(nq // LANES):
        sl = slice(grp * LANES, (grp + 1) * LANES)
        qt_ref[0, sl, :] = qk[:, sl].T.astype(BF16)
    for hd in range(H_M):
        sl = slice(hd * LANES, (hd + 1) * LANES)
        vt_ref[0, sl, :] = y[:, _M_V - _M_QK + hd * LANES:_M_V - _M_QK + (hd + 1) * LANES].T.astype(BF16)
    k_ref[0] = (qk[:, nq:] * (M_DK ** -0.5)).astype(BF16)
    og_ref[0] = y[:, _M_OG - _M_QK:].astype(BF16)


def _m_in_proj(x_parts, modsel, w_ext, conv_w, conv_b, gate_b, nlt, t):
    b, d = modsel.shape[0], modsel.shape[-1]
    tm = ROW_TILE
    nt = t // tm
    r8 = tm // 8
    lat, ctx, ctx_tile0 = x_parts
    c0, n8_lat, n8_ctx = ctx_tile0 * r8, nlt * r8, (nt - nlt) * r8
    halo = lambda off, lo, n8: (lambda bi, i: (bi, lo + jnp.clip(i * r8 + off, 0, n8 - 1), 0))
    halo_ctx = lambda off: (lambda bi, i: (bi, c0 + jnp.clip((i - nlt) * r8 + off, 0, n8_ctx - 1), 0))
    x_specs, x_args = _split_rows_specs(x_parts, nlt, tm)
    x_specs += [pl.BlockSpec((1, 8, d), halo(-1, 0, n8_lat)), pl.BlockSpec((1, 8, d), halo_ctx(-1)),
                pl.BlockSpec((1, 8, d), halo(r8, 0, n8_lat)), pl.BlockSpec((1, 8, d), halo_ctx(r8))]
    x_args += [lat, ctx, lat, ctx]
    row = lambda bi, i: (bi, i, 0)
    col = lambda bi, i: (bi, 0, i)
    nk, n = H_M * M_DK, H_M * M_DV
    out_shapes = [jax.ShapeDtypeStruct((b, nk, t), BF16), jax.ShapeDtypeStruct((b, t, nk), BF16),
                  jax.ShapeDtypeStruct((b, n, t), BF16), jax.ShapeDtypeStruct((b, t, n), BF16)] + \
                 [jax.ShapeDtypeStruct((b, LANES, t), F32)] * 2 + [jax.ShapeDtypeStruct((b, t, LANES), F32)]
    out_specs = [pl.BlockSpec((1, nk, tm), col), pl.BlockSpec((1, tm, nk), row),
                 pl.BlockSpec((1, n, tm), col), pl.BlockSpec((1, tm, n), row)] + \
                [pl.BlockSpec((1, LANES, tm), col)] * 2 + [pl.BlockSpec((1, tm, LANES), row)]
    kern = functools.partial(_m_in_kernel, nlt=nlt, seg_starts=(0, nlt), seg_ends=(nlt - 1, nt - 1))
    return pl.pallas_call(
        kern,
        grid=(b, nt),
        in_specs=x_specs + [
                  pl.BlockSpec((1, 1, 6, d), lambda bi, i: (bi, (i >= nlt).astype(jnp.int32), 0, 0)),
                  _const_spec(w_ext.shape), _const_spec(conv_w.shape), _const_spec(conv_b.shape),
                  _const_spec(gate_b.shape)],
        out_specs=out_specs,
        out_shape=out_shapes,
        compiler_params=_cparams(("parallel", "arbitrary")),
        name="m_in_proj",
    )(*x_args, modsel, w_ext, conv_w, conv_b, gate_b)


def _mlstm_gates(gc, gt1, gt2, hd, backward, m_prev):
    L = M_SCAN_CHUNK
    gi = 2 * H_M if backward else 0
    i_row = gt1[gi + hd:gi + hd + 1, :]
    b_row = gt1[gi + H_M + hd:gi + H_M + hd + 1, :]
    mloc_row = gt2[gi + hd:gi + hd + 1, :]
    r_col = gc[:, gi + hd:gi + hd + 1]
    inter = b_row + m_prev
    m_t = jnp.maximum(inter, mloc_row)
    e = 0 if backward else L - 1
    b_end = b_row[:, e:e + 1]
    m_new = jnp.maximum(b_end + m_prev, mloc_row[:, e:e + 1])
    return i_row, b_row, r_col, inter, m_t, b_end, m_new


def _mlstm_kernel(*refs):
    ins, (hf_ref, hb_ref, cstate, mstate, sc_scr) = refs[:12], refs[12:]
    c = pl.program_id(1)
    L = M_SCAN_CHUNK

    @pl.when(c == 0)
    def _():
        cstate[...] = jnp.zeros_like(cstate)
        mstate[...] = jnp.zeros_like(mstate)

    chains = [(d, hd) for d in range(2) for hd in range(H_M)]

    def operands(d, hd):
        k_ref, qt_ref, vt_ref, gc_ref, gt1_ref, gt2_ref = ins[6 * d:6 * d + 6]
        pair = slice((hd // 2) * LANES, (hd // 2 + 1) * LANES)
        k_pair = k_ref[0, :, pair]
        lane = lax.broadcasted_iota(jnp.int32, k_pair.shape, 1)
        k_own = jnp.where((lane // M_DK) == (hd % 2), k_pair, jnp.zeros_like(k_pair))
        gates = _mlstm_gates(gc_ref[0], gt1_ref[0], gt2_ref[0], hd, d == 1, mstate[d * H_M + hd][0:1, 0:1])
        return k_own, qt_ref[0, pair, :], vt_ref[0, hd * LANES:(hd + 1) * LANES, :], gates

    ones_rows = jnp.ones((M_ONES_ROWS, L), BF16)
    s_idx = lax.broadcasted_iota(jnp.int32, (L, L), 0)
    t_idx = lax.broadcasted_iota(jnp.int32, (L, L), 1)

    for ch, (d, hd) in enumerate(chains):
        k_own, qt, _, (_, b_row, r_col, _, m_t, _, _) = operands(d, hd)
        visible = (s_idx >= t_idx) if d == 1 else (s_idx <= t_idx)
        dt = jnp.exp2(jnp.where(visible, r_col + (b_row - m_t) * LOG2E, NEG))
        sc_scr[ch] = (jnp.dot(k_own, qt, preferred_element_type=F32) * dt).astype(BF16)

    for ch, (d, hd) in enumerate(chains):
        _, qt, vt, (_, _, _, inter, m_t, _, _) = operands(d, hd)
        vext_t = jnp.concatenate([vt, ones_rows], axis=0)
        nd_t = jnp.dot(vext_t, sc_scr[ch], preferred_element_type=F32) \
            + jnp.dot(cstate[ch].astype(BF16), qt, preferred_element_type=F32) * jnp.exp(inter - m_t)
        den = jnp.broadcast_to(nd_t[LANES:LANES + 1, :], (LANES, L))
        h_t = nd_t[:LANES] / jnp.maximum(jnp.abs(den), jnp.exp(-m_t))
        (hb_ref if d else hf_ref)[0, :, hd * LANES:(hd + 1) * LANES] = h_t.T

    for ch, (d, hd) in enumerate(chains):
        k_own, _, vt, (i_row, b_row, _, _, _, b_end, m_new) = operands(d, hd)
        m_prev = mstate[ch][0:1, 0:1]
        vext_t = jnp.concatenate([vt, ones_rows], axis=0)
        w_row = jnp.exp(b_end - b_row + i_row - m_new)
        cstate[ch] = jnp.exp(b_end + m_prev - m_new) * cstate[ch] \
            + jnp.dot((vext_t.astype(F32) * w_row).astype(BF16), k_own, preferred_element_type=F32)
        mstate[ch] = jnp.broadcast_to(m_new, mstate.shape[1:])


def _mlstm(k, qt, vt, gc, gt1, gt2, n_lat):
    b, t, nk = k.shape
    n = vt.shape[1]
    L = M_SCAN_CHUNK
    nb = t // L
    fwd, bwd, fwd_t, bwd_t = _scan_maps(n_lat // L, nb - n_lat // L)

    def specs(rmap, tmap):
        return [pl.BlockSpec((1, L, nk), rmap), pl.BlockSpec((1, nk, L), tmap), pl.BlockSpec((1, n, L), tmap),
                pl.BlockSpec((1, L, LANES), rmap), pl.BlockSpec((1, 4 * H_M, L), tmap),
                pl.BlockSpec((1, 4 * H_M, L), tmap)]

    out = jax.ShapeDtypeStruct((b, t, H_M * M_DV), F32)
    args = (k, qt, vt, gc, gt1, gt2)
    return pl.pallas_call(
        _mlstm_kernel,
        grid=(b, nb),
        in_specs=specs(fwd, fwd_t) + specs(bwd, bwd_t),
        out_specs=[pl.BlockSpec((1, L, H_M * M_DV), fwd), pl.BlockSpec((1, L, H_M * M_DV), bwd)],
        out_shape=[out, out],
        scratch_shapes=[pltpu.VMEM((2 * H_M, LANES + M_ONES_ROWS, LANES), F32),
                        pltpu.VMEM((2 * H_M, 8, LANES), F32),
                        pltpu.VMEM((2 * H_M, L, L), BF16)],
        compiler_params=_cparams(("parallel", "arbitrary")),
        name="mlstm_scan",
    )(*args, *args)


def _rope_tables(n_lat, ctx_len, d, lane0):
    rows = n_lat // GRID_W
    row = np.repeat(np.arange(rows, dtype=np.float64), GRID_W)
    col = np.tile(np.arange(GRID_W, dtype=np.float64), rows)
    da = d // 2
    inv = ROPE_BASE ** (-np.arange(0, da, 2, dtype=np.float64) / da)
    ar = row[:, None] * inv
    ac = col[:, None] * inv
    tab = np.zeros((2, n_lat + ctx_len, LANES), np.float32)
    tab[0] = 1.0
    tab[0, :n_lat, lane0:lane0 + d] = np.concatenate([np.cos(ar), np.cos(ar), np.cos(ac), np.cos(ac)], axis=-1)
    tab[1, :n_lat, lane0:lane0 + d] = np.concatenate([-np.sin(ar), np.sin(ar), -np.sin(ac), np.sin(ac)], axis=-1)
    return jnp.asarray(tab)


def _pad_heads(w, n_heads, width):
    k = w.shape[0]
    w = w.reshape(k, n_heads, width)
    return jnp.pad(w, ((0, 0), (0, 0), (0, LANES - width))).reshape(k, n_heads * LANES)


def _prep_ab_weights(w_in, w_uq, w_ukv):
    o1 = MLA_Q_LORA
    o2 = o1 + MLA_KV_LORA
    o3 = o2 + MLA_ROPE
    o4 = o3 + H_RET * RET_DK
    o5 = o4 + H_RET * RET_DK
    o6 = o5 + H_RET * RET_DV
    cq, ckv, kr, rq, rk, rv, rg = (w_in[:, :o1], w_in[:, o1:o2], w_in[:, o2:o3], w_in[:, o3:o4],
                                   w_in[:, o4:o5], w_in[:, o5:o6], w_in[:, o6:])
    kr_blk = jnp.pad(kr, ((0, 0), (MLA_NOPE, LANES - MLA_NOPE - MLA_ROPE)))
    w_ext = jnp.concatenate([cq, ckv, rq, rk, rv, rg, kr_blk], axis=1)

    wuq_ext = _pad_heads(w_uq, H_MLA, MLA_NOPE + MLA_ROPE)
    ukv = w_ukv.reshape(-1, H_MLA, MLA_NOPE + MLA_V)
    k_nope = _pad_heads(ukv[..., :MLA_NOPE].reshape(-1, H_MLA * MLA_NOPE), H_MLA, MLA_NOPE)
    v_w = ukv[..., MLA_NOPE:]
    zeros = jnp.zeros_like(v_w)
    even = (np.arange(H_MLA) % 2 == 0)[None, :, None]
    v_cols = jnp.concatenate([jnp.where(even, v_w, zeros), jnp.where(even, zeros, v_w)], axis=-1)
    wukv_ext = jnp.concatenate([k_nope, v_cols.reshape(-1, H_MLA * LANES)], axis=1)
    return w_ext.astype(BF16), wuq_ext.astype(BF16), wukv_ext.astype(BF16)


def _prep_m_weights(w_in, conv_w, conv_b, gate_b):
    o1 = 2 * H_M * M_DK
    o2 = o1 + H_M * M_DV
    o3 = o2 + H_M * M_DV
    g_pad = jnp.pad(w_in[:, o3:], ((0, 0), (0, LANES - 4 * H_M)))
    w_ext = jnp.concatenate([g_pad, w_in[:, :o3]], axis=1).astype(BF16)
    gb = jnp.pad(gate_b.reshape(1, 4 * H_M), ((0, 0), (0, LANES - 4 * H_M)))
    return w_ext, conv_w, conv_b[None, :], gb


def kernel(x, c, ctx, c_ctx, mod_w, mod_b, ln_g, ln_b, ffn_w_in, ffn_w_out, ab_w_in, mla_q_norm, mla_w_uq,
           mla_kv_norm, mla_w_ukv, ret_log_decay, ab_w_out, m_w_in, m_conv_w, m_conv_b, m_gate_b, m_norm_g, m_w_out):
    b, s, d = x.shape
    ctx_len = ctx.shape[1]
    depth = mod_w.shape[0]
    assert ctx_len % ROW_TILE == 0 and s % ROW_TILE == 0 and s % GRID_W == 0 and b + 1 <= 8
    nlt = s // ROW_TILE
    alpha = (2 * depth) ** 0.25

    c_rows = jnp.zeros((8, d), F32).at[:b].set(c).at[b].set(c_ctx)
    mods = _modulation(c_rows, mod_w, mod_b).reshape(depth, 8, 6, d)
    tab_mla = _rope_tables(s, ctx_len, MLA_ROPE, MLA_NOPE)
    tab_ret = _rope_tables(s, ctx_len, RET_DK, 0)

    t = s + ctx_len
    x_lat, x_ctx = x, ctx
    for l in range(depth):
        last = l == depth - 1
        j = l // 2
        modsel = jnp.stack([mods[l, :b], jnp.broadcast_to(mods[l, b], (b, 6, d))], axis=1)
        ln = jnp.stack([ln_g[l, 0], ln_b[l, 0], ln_g[l, 1], ln_b[l, 1]])
        w1 = ffn_w_in[l].astype(BF16)
        w2 = ffn_w_out[l].astype(BF16)
        x_parts = (x_lat, x_ctx, 0)
        if l % 2 == 0:
            w_ext, wuq_ext, wukv_ext = _prep_ab_weights(ab_w_in[j], mla_w_uq[j], mla_w_ukv[j])
            q, kt, v, rq, rkt, rv, rg = _ab_in_proj(x_parts, modsel, w_ext, mla_q_norm[j][None, :],
                                                    mla_kv_norm[j][None, :], wuq_ext, wukv_ext,
                                                    tab_mla, tab_ret, nlt, t)
            att_lat = _attention(q, kt, v, s, False)
            att_ctx = None if last else _attention(q, kt, v, s, True)
            ret_f, ret_b = _retention(rq, rkt, rv, ret_log_decay[j], s)
            post_fn, acts, consts, w_out = _ab_post_kernel, [ret_f, ret_b, rg], [], ab_w_out[j]
        else:
            w_ext, cw, cb, gb = _prep_m_weights(m_w_in[j], m_conv_w[j], m_conv_b[j], m_gate_b[j])
            qt, k, vt, og, gt1, gt2, gc = _m_in_proj(x_parts, modsel, w_ext, cw, cb, gb, nlt, t)
            hf, hb = _mlstm(k, qt, vt, gc, gt1, gt2, s)
            post_fn, acts, consts, w_out = _m_post_kernel, [hf, hb, og], [m_norm_g[j][None, :]], m_w_out[j]
            att_lat = att_ctx = None

        def post(tm, tile0, n_tiles, for_ctx):
            pair = (lambda a_lat, a_ctx: (a_lat, a_ctx, 0) if for_ctx else (a_lat, a_lat, 0))
            att = [] if att_lat is None else [pair(att_lat, att_ctx)]
            return _post(post_fn, pair(x_lat, x_ctx), modsel, att + acts, consts, w_out.astype(BF16),
                         ln, w1, w2, s, tm, tile0, n_tiles, alpha)

        new_lat = post(POST_TILE, 0, s // POST_TILE, False)
        if not last:
            x_ctx = post(ROW_TILE, nlt, ctx_len // ROW_TILE, True)
        x_lat = new_lat
    return x_lat
```

```python
import functools

import jax
import jax.numpy as jnp
import numpy as np
from jax import lax
from jax.experimental import pallas as pl
from jax.experimental.pallas import tpu as pltpu

F32 = jnp.float32
BF16 = jnp.bfloat16
HIGHEST = lax.Precision.HIGHEST

GRID_W = 64
ROPE_BASE = 10000.0
EPS = 1e-5
NEG = -1e30
LOG2E = 1.4426950408889634
H_MLA, MLA_NOPE, MLA_ROPE, MLA_V = 8, 64, 32, 64
MLA_Q_LORA, MLA_KV_LORA = 384, 256
H_RET, RET_DK, RET_DV = 4, 128, 128
H_M, M_DK, M_DV = 8, 64, 128

LANES = 128
ROW_TILE = 256
POST_TILE = 512
RET_CHUNK = 128
RET_BLOCK = 256
ATT_KEY_CHUNK = 512
ATT_Q_TILE = 1024
M_SCAN_CHUNK = 256
M_ONES_ROWS = 16
VMEM_LIMIT = 56 * 1024 * 1024


def _cparams(sem, flags=None):
    return pltpu.CompilerParams(dimension_semantics=sem, vmem_limit_bytes=VMEM_LIMIT, flags=flags)


def _const_spec(shape):
    nd = len(shape)
    return pl.BlockSpec(shape, lambda *_: (0,) * nd, pipeline_mode=pl.Buffered(1))


def _split_rows_specs(parts, nlt, tm, tile0=0):
    lat, ctx, ctx_tile0 = parts
    n = lat.shape[-1]
    specs = [pl.BlockSpec((1, tm, n), lambda bi, i: (bi, jnp.minimum(i + tile0, nlt - 1), 0)),
             pl.BlockSpec((1, tm, n), lambda bi, i: (bi, ctx_tile0 + jnp.maximum(i + tile0 - nlt, 0), 0))]
    return specs, [lat, ctx]


def _select_rows(nlt, lat_ref, ctx_ref, tile0=0, rows=slice(None)):
    return jnp.where(pl.program_id(1) + tile0 >= nlt, ctx_ref[0, rows, :], lat_ref[0, rows, :])


def _silu(v):
    return v * jax.nn.sigmoid(v)


def _layer_norm_rows(v, g, b):
    mu = jnp.mean(v, axis=-1, keepdims=True)
    d = v - mu
    var = jnp.mean(d * d, axis=-1, keepdims=True)
    return d * lax.rsqrt(var + EPS) * g + b


def _head_norm_lanes(v):
    outs = []
    for h in range(v.shape[-1] // LANES):
        blk = v[:, h * LANES:(h + 1) * LANES]
        mu = jnp.mean(blk, axis=-1, keepdims=True)
        d = blk - mu
        var = jnp.mean(d * d, axis=-1, keepdims=True)
        outs.append(d * lax.rsqrt(var + EPS))
    return outs


def _mod_kernel(c_ref, w_ref, b_ref, o_ref):
    sc = _silu(c_ref[...])
    o_ref[0] = jnp.dot(sc, w_ref[0], precision=HIGHEST, preferred_element_type=F32) + b_ref[0]


def _modulation(c_rows, mod_w, mod_b):
    depth, d, n = mod_w.shape
    tn = 1536
    return pl.pallas_call(
        _mod_kernel,
        grid=(depth, n // tn),
        in_specs=[pl.BlockSpec((8, d), lambda l, j: (0, 0)),
                  pl.BlockSpec((1, d, tn), lambda l, j: (l, 0, j)),
                  pl.BlockSpec((1, 1, tn), lambda l, j: (l, 0, j))],
        out_specs=pl.BlockSpec((1, 8, tn), lambda l, j: (l, 0, j)),
        out_shape=jax.ShapeDtypeStruct((depth, 8, n), F32),
        compiler_params=_cparams(("arbitrary", "arbitrary")),
        name="modulation",
    )(c_rows, mod_w, mod_b.reshape(depth, 1, n))


_A_CQ, _A_CKV, _A_RQ, _A_RK, _A_RV, _A_RG, _A_KR, _A_END = 0, 384, 640, 1152, 1664, 2176, 2688, 2816


def _swap_pairs(v, width):
    lane = lax.broadcasted_iota(jnp.int32, v.shape, 1)
    return jnp.where((lane // width) % 2 == 0, pltpu.roll(v, LANES - width, axis=1), pltpu.roll(v, width, axis=1))


def _ab_in_kernel(x_ref, xc_ref, mod_ref, w_ref, qn_ref, kvn_ref, wuq_ref, wukv_ref, tm_ref, tr_ref,
                  q_ref, kt_ref, v_ref, rq_ref, rkt_ref, rv_ref, rg_ref, *, nlt):
    x = _select_rows(nlt, x_ref, xc_ref)
    shift = mod_ref[0, 0, 0:1, :]
    scale = mod_ref[0, 0, 1:2, :]
    h = (x * (1.0 + scale) + shift).astype(BF16)
    y = jnp.dot(h, w_ref[...], preferred_element_type=F32)

    cq = y[:, _A_CQ:_A_CKV]
    ckv = y[:, _A_CKV:_A_RQ]
    ncq = (cq * lax.rsqrt(jnp.mean(cq * cq, axis=-1, keepdims=True) + EPS) * qn_ref[...]).astype(BF16)
    nckv = (ckv * lax.rsqrt(jnp.mean(ckv * ckv, axis=-1, keepdims=True) + EPS) * kvn_ref[...]).astype(BF16)
    q2 = jnp.dot(ncq, wuq_ref[...], preferred_element_type=F32)
    kv = jnp.dot(nckv, wukv_ref[...], preferred_element_type=F32)

    cos_m = tm_ref[0]
    sin_m = tm_ref[1]
    q_scale = (MLA_NOPE + MLA_ROPE) ** -0.5 * LOG2E
    kr = y[:, _A_KR:_A_END]
    k_rope = kr * cos_m + _swap_pairs(kr, MLA_ROPE // 4) * sin_m
    nq = H_MLA * LANES
    for hd in range(H_MLA):
        sl = slice(hd * LANES, (hd + 1) * LANES)
        qh = q2[:, sl] * cos_m + _swap_pairs(q2[:, sl], MLA_ROPE // 4) * sin_m
        q_ref[0, :, sl] = (qh * q_scale).astype(BF16)
        kt_ref[0, sl, :] = (kv[:, sl] + k_rope).T.astype(BF16)
    lane = lax.broadcasted_iota(jnp.int32, (1, nq), 1)
    ones_half = jnp.where(((lane // MLA_V) % 2) != ((lane // LANES) % 2), 1.0, 0.0)
    v_ref[0] = (kv[:, nq:] + ones_half).astype(BF16)

    cos_r = tr_ref[0]
    sin_r = tr_ref[1]
    k_scale = RET_DK ** -0.5
    for hd in range(H_RET):
        sl = slice(hd * LANES, (hd + 1) * LANES)
        rq = y[:, _A_RQ + hd * LANES:_A_RQ + (hd + 1) * LANES]
        rk = y[:, _A_RK + hd * LANES:_A_RK + (hd + 1) * LANES]
        rq = rq * cos_r + _swap_pairs(rq, RET_DK // 4) * sin_r
        rk = rk * cos_r + _swap_pairs(rk, RET_DK // 4) * sin_r
        rq_ref[0, :, sl] = rq.astype(BF16)
        rkt_ref[0, sl, :] = (rk * k_scale).T.astype(BF16)
    rv_ref[0] = y[:, _A_RV:_A_RG].astype(BF16)
    rg_ref[0] = y[:, _A_RG:_A_KR].astype(BF16)


def _ab_in_proj(x_parts, modsel, w_ext, q_norm, kv_norm, wuq_ext, wukv_ext, tab_mla, tab_ret, nlt, t):
    b, d = modsel.shape[0], modsel.shape[-1]
    tm = ROW_TILE
    row = lambda bi, i: (bi, i, 0)
    col = lambda bi, i: (bi, 0, i)
    n_ret = H_RET * RET_DK
    widths = (H_MLA * LANES, H_MLA * LANES, H_MLA * LANES, n_ret, n_ret, n_ret, n_ret)
    transposed = (False, True, False, False, True, False, False)
    out_shapes = [jax.ShapeDtypeStruct((b, w, t) if tr else (b, t, w), BF16) for w, tr in zip(widths, transposed)]
    out_specs = [pl.BlockSpec((1, w, tm), col) if tr else pl.BlockSpec((1, tm, w), row)
                 for w, tr in zip(widths, transposed)]
    x_specs, x_args = _split_rows_specs(x_parts, nlt, tm)
    return pl.pallas_call(
        functools.partial(_ab_in_kernel, nlt=nlt),
        grid=(b, t // tm),
        in_specs=x_specs + [
                  pl.BlockSpec((1, 1, 6, d), lambda bi, i: (bi, (i >= nlt).astype(jnp.int32), 0, 0)),
                  _const_spec(w_ext.shape), _const_spec(q_norm.shape), _const_spec(kv_norm.shape),
                  _const_spec(wuq_ext.shape), _const_spec(wukv_ext.shape),
                  pl.BlockSpec((2, tm, LANES), lambda bi, i: (0, i, 0)),
                  pl.BlockSpec((2, tm, LANES), lambda bi, i: (0, i, 0))],
        out_specs=out_specs,
        out_shape=out_shapes,
        compiler_params=_cparams(("parallel", "arbitrary")),
        name="ab_in_proj",
    )(*x_args, modsel, w_ext, q_norm, kv_norm, wuq_ext, wukv_ext, tab_mla, tab_ret)


def _attn_kernel(q_ref, kt_ref, v_ref, o_ref, s_scr, m_scr, acc_scr, *, tk, ctx_key0, ctx_len, n_lat_chunks):
    tq = q_ref.shape[1]
    m_scr[...] = jnp.full(m_scr.shape, NEG, F32)
    acc_scr[...] = jnp.zeros_like(acc_scr)

    def scores(slot, r0, size):
        for hh in range(2):
            sl = slice(hh * LANES, (hh + 1) * LANES)
            s_scr[slot, hh, :, 0:size] = jnp.dot(q_ref[0, :, sl], kt_ref[0, sl, pl.ds(r0, size)],
                                                 preferred_element_type=F32)

    def accumulate(slot, r0, size):
        for hh in range(2):
            sl = slice(hh * LANES, (hh + 1) * LANES)
            s = s_scr[slot, hh, :, 0:size]
            m = m_scr[hh]
            m_new = jnp.maximum(m, jnp.max(s, axis=-1, keepdims=True))
            m_scr[hh] = m_new
            p = jnp.exp2(s - m_new[:, 0:1]).astype(BF16)
            acc_scr[hh] = jnp.exp2(m - m_new) * acc_scr[hh] \
                + jnp.dot(p, v_ref[0, pl.ds(r0, size), sl], preferred_element_type=F32)

    def lat_row(c):
        return pl.multiple_of(jnp.minimum(c, n_lat_chunks - 1) * tk, tk)

    scores(1, ctx_key0, ctx_len)
    if n_lat_chunks:
        scores(0, 0, tk)
    accumulate(1, ctx_key0, ctx_len)
    if n_lat_chunks:

        def body(j, carry):
            scores(1, lat_row(2 * j + 1), tk)
            accumulate(0, lat_row(2 * j), tk)
            scores(0, lat_row(2 * j + 2), tk)
            accumulate(1, lat_row(2 * j + 1), tk)
            return carry

        lax.fori_loop(0, n_lat_chunks // 2, body, 0)
    outs = [acc_scr[hh] / pltpu.roll(acc_scr[hh], MLA_V, axis=1) for hh in range(2)]
    lane = lax.broadcasted_iota(jnp.int32, (tq, LANES), 1)
    o_ref[0] = jnp.where(lane < MLA_V, outs[0], outs[1]).astype(BF16)


def _attention(q, kt, v, n_lat, for_ctx):
    b, t, _ = q.shape
    ctx_len = t - n_lat
    tk = ATT_KEY_CHUNK
    assert n_lat % (2 * tk) == 0 and n_lat % ATT_Q_TILE == 0 and ctx_len % ROW_TILE == 0
    tq, n_rows, row0 = (ROW_TILE, ctx_len, n_lat // ROW_TILE) if for_ctx else (ATT_Q_TILE, n_lat, 0)
    pair = 2 * LANES
    if for_ctx:
        assert n_lat % ctx_len == 0
        key_blk, n_keys, key0 = n_lat // ctx_len, ctx_len, 0
    else:
        key_blk, n_keys, key0 = 0, t, n_lat
    kern = functools.partial(_attn_kernel, tk=tk, ctx_key0=key0, ctx_len=ctx_len,
                             n_lat_chunks=0 if for_ctx else n_lat // tk)
    return pl.pallas_call(
        kern,
        grid=(b, H_MLA // 2, n_rows // tq),
        in_specs=[pl.BlockSpec((1, tq, pair), lambda bi, hp, i: (bi, i + row0, hp)),
                  pl.BlockSpec((1, pair, n_keys), lambda bi, hp, i: (bi, hp, key_blk)),
                  pl.BlockSpec((1, n_keys, pair), lambda bi, hp, i: (bi, key_blk, hp))],
        out_specs=pl.BlockSpec((1, tq, LANES), lambda bi, hp, i: (bi, i, hp)),
        out_shape=jax.ShapeDtypeStruct((b, n_rows, H_MLA * MLA_V), BF16),
        scratch_shapes=[pltpu.VMEM((2, 2, tq, max(tk, ctx_len)), F32), pltpu.VMEM((2, tq, LANES), F32),
                        pltpu.VMEM((2, tq, LANES), F32)],
        compiler_params=_cparams(("parallel", "parallel", "arbitrary")),
        name="mla_attention_ctx" if for_ctx else "mla_attention",
    )(q, kt, v)


def _ret_kernel(ld_ref, qf_ref, ktf_ref, vf_ref, qb_ref, ktb_ref, vb_ref, of_ref, ob_ref,
                state, dmat, dq, wk, gl, s_scr):
    c = pl.program_id(1)
    L = RET_CHUNK

    @pl.when(c == 0)
    def _():
        state[...] = jnp.zeros_like(state)
        row = lax.broadcasted_iota(jnp.int32, (L, L), 0).astype(F32)
        col = lax.broadcasted_iota(jnp.int32, (L, L), 1).astype(F32)
        qpos = lax.broadcasted_iota(jnp.int32, (L, RET_DK), 0).astype(F32)
        kpos = lax.broadcasted_iota(jnp.int32, (RET_DK, L), 1).astype(F32)
        for d in range(2):
            for hd in range(H_RET):
                lg = ld_ref[d, hd]
                if d == 0:
                    rel = row - col
                    dq[d, hd] = jnp.exp(lg * (qpos + 1.0))
                    wk[d, hd] = jnp.exp(lg * (L - 1.0 - kpos))
                else:
                    rel = col - row
                    dq[d, hd] = jnp.exp(lg * (L - qpos))
                    wk[d, hd] = jnp.exp(lg * kpos)
                dmat[d, hd] = jnp.where(rel >= 0, jnp.exp(lg * jnp.maximum(rel, 0.0)), 0.0)
                gl[d, hd] = jnp.exp(jnp.zeros((RET_DK, RET_DV), F32) + lg * L)

    streams = ((qf_ref, ktf_ref, vf_ref, of_ref), (qb_ref, ktb_ref, vb_ref, ob_ref))
    n_sub = qf_ref.shape[1] // L
    for k in range(n_sub):
        rows = [slice(k * L, (k + 1) * L), slice((n_sub - 1 - k) * L, (n_sub - k) * L)]
        chains = [(d, hd, slice(hd * LANES, (hd + 1) * LANES), rows[d]) + streams[d]
                  for d in range(2) for hd in range(H_RET)]
        for d, hd, sl, r, q_ref, kt_ref, _, _ in chains:
            s = jnp.dot(q_ref[0, r, sl], kt_ref[0, sl, r], preferred_element_type=F32) * dmat[d, hd]
            s_scr[d, hd] = s.astype(BF16)
        for d, hd, sl, r, q_ref, _, v_ref, o_ref in chains:
            qd = (q_ref[0, r, sl].astype(F32) * dq[d, hd]).astype(BF16)
            o = jnp.dot(s_scr[d, hd], v_ref[0, r, sl], preferred_element_type=F32) \
                + jnp.dot(qd, state[d, hd].astype(BF16), preferred_element_type=F32)
            o_ref[0, r, sl] = o.astype(BF16)
        for d, hd, sl, r, _, kt_ref, v_ref, _ in chains:
            kw = (kt_ref[0, sl, r].astype(F32) * wk[d, hd]).astype(BF16)
            state[d, hd] = state[d, hd] * gl[d, hd] + jnp.dot(kw, v_ref[0, r, sl], preferred_element_type=F32)


def _scan_maps(n_lat_blocks, n_ctx_blocks):
    nlb, ncb = n_lat_blocks, n_ctx_blocks
    fwd_blk = lambda c: jnp.where(c < ncb, nlb + c, c - ncb)
    bwd_blk = lambda c: nlb + ncb - 1 - c
    fwd = lambda bi, c: (bi, fwd_blk(c), 0)
    bwd = lambda bi, c: (bi, bwd_blk(c), 0)
    fwd_t = lambda bi, c: (bi, 0, fwd_blk(c))
    bwd_t = lambda bi, c: (bi, 0, bwd_blk(c))
    return fwd, bwd, fwd_t, bwd_t


def _retention(rq, rkt, rv, log_decay, n_lat):
    b, t, n = rq.shape
    L, blk = RET_CHUNK, RET_BLOCK
    nb = t // blk
    fwd, bwd, fwd_t, bwd_t = _scan_maps(n_lat // blk, nb - n_lat // blk)
    tile = pl.BlockSpec((1, blk, n), fwd)
    tile_b = pl.BlockSpec((1, blk, n), bwd)
    out = jax.ShapeDtypeStruct((b, t, n), BF16)
    per_head = lambda rows, cols, dtype=F32: pltpu.VMEM((2, H_RET, rows, cols), dtype)
    scratch = [per_head(RET_DK, RET_DV),
               per_head(L, L),
               per_head(L, RET_DK),
               per_head(RET_DK, L),
               per_head(RET_DK, RET_DV),
               per_head(L, L, BF16)]
    return pl.pallas_call(
        _ret_kernel,
        grid=(b, nb),
        in_specs=[pl.BlockSpec(memory_space=pltpu.SMEM),
                  tile, pl.BlockSpec((1, n, blk), fwd_t), tile,
                  tile_b, pl.BlockSpec((1, n, blk), bwd_t), tile_b],
        out_specs=[tile, tile_b],
        out_shape=[out, out],
        scratch_shapes=scratch,
        compiler_params=_cparams(("parallel", "arbitrary")),
        name="retention_scan",
    )(log_decay, rq, rkt, rv, rq, rkt, rv)


def _post_stages(xs, mixes, mod_ref, wo_ref, ln_ref, w1_ref, w2_ref, alpha):
    g1 = mod_ref[0, 0, 2:3, :]
    sh2 = mod_ref[0, 0, 3:4, :]
    sc2 = mod_ref[0, 0, 4:5, :]
    g2 = mod_ref[0, 0, 5:6, :]
    ys = [jnp.dot(m, wo_ref[...], preferred_element_type=F32) for m in mixes]
    x1s = [_layer_norm_rows(alpha * x + g1 * y, ln_ref[0:1, :], ln_ref[1:2, :]) for x, y in zip(xs, ys)]
    us = [jnp.dot((x1 * (1.0 + sc2) + sh2).astype(BF16), w1_ref[...], preferred_element_type=F32) for x1 in x1s]
    dff = w1_ref.shape[-1] // 2
    acts = [(_silu(u[:, dff:]) * u[:, :dff]).astype(BF16) for u in us]
    y2s = [jnp.dot(a, w2_ref[...], preferred_element_type=F32) for a in acts]
    return [_layer_norm_rows(alpha * x1 + g2 * y2, ln_ref[2:3, :], ln_ref[3:4, :]) for x1, y2 in zip(x1s, y2s)]


def _sub_tiles(o_ref):
    return [slice(r, r + ROW_TILE) for r in range(0, o_ref.shape[1], ROW_TILE)]


def _ab_post_kernel(x_ref, xc_ref, mod_ref, att_ref, attc_ref, rf_ref, rb_ref, rg_ref,
                    wo_ref, ln_ref, w1_ref, w2_ref, o_ref, *, alpha, nlt, tile0):
    subs = _sub_tiles(o_ref)
    mixes = []
    for rows in subs:
        ret = rf_ref[0, rows, :].astype(F32) + rb_ref[0, rows, :].astype(F32)
        gate = _silu(rg_ref[0, rows, :].astype(F32))
        parts = [_select_rows(nlt, att_ref, attc_ref, tile0, rows)]
        for hd, nh in enumerate(_head_norm_lanes(ret)):
            parts.append((gate[:, hd * LANES:(hd + 1) * LANES] * nh).astype(BF16))
        mixes.append(jnp.concatenate(parts, axis=-1))
    xs = [_select_rows(nlt, x_ref, xc_ref, tile0, rows) for rows in subs]
    for rows, out in zip(subs, _post_stages(xs, mixes, mod_ref, wo_ref, ln_ref, w1_ref, w2_ref, alpha)):
        o_ref[0, rows, :] = out


def _m_post_kernel(x_ref, xc_ref, mod_ref, hf_ref, hb_ref, og_ref, ng_ref,
                   wo_ref, ln_ref, w1_ref, w2_ref, o_ref, *, alpha, nlt, tile0):
    subs = _sub_tiles(o_ref)
    mixes = []
    for rows in subs:
        hs = hf_ref[0, rows, :].astype(F32) + hb_ref[0, rows, :].astype(F32)
        og = jax.nn.sigmoid(og_ref[0, rows, :].astype(F32))
        parts = []
        for hd, nh in enumerate(_head_norm_lanes(hs)):
            sl = slice(hd * LANES, (hd + 1) * LANES)
            parts.append((og[:, sl] * (nh * ng_ref[:, sl])).astype(BF16))
        mixes.append(jnp.concatenate(parts, axis=-1))
    xs = [_select_rows(nlt, x_ref, xc_ref, tile0, rows) for rows in subs]
    for rows, out in zip(subs, _post_stages(xs, mixes, mod_ref, wo_ref, ln_ref, w1_ref, w2_ref, alpha)):
        o_ref[0, rows, :] = out


def _post(kernel_fn, x_parts, modsel, acts, consts, w_out, ln, w1, w2, n_lat, tm, tile0, n_tiles, alpha):
    b, d = modsel.shape[0], modsel.shape[-1]
    assert n_lat % tm == 0 and tm % ROW_TILE == 0
    nlt = n_lat // tm
    row = lambda bi, i: (bi, i + tile0, 0)
    in_specs, args = _split_rows_specs(x_parts, nlt, tm, tile0)
    in_specs.append(pl.BlockSpec((1, 1, 6, d), lambda bi, i: (bi, (i + tile0 >= nlt).astype(jnp.int32), 0, 0)))
    args.append(modsel)
    for a in acts:
        if isinstance(a, tuple):
            sp, ar = _split_rows_specs(a, nlt, tm, tile0)
            in_specs += sp
            args += ar
        else:
            in_specs.append(pl.BlockSpec((1, tm, a.shape[-1]), row))
            args.append(a)
    weights = [*consts, w_out, ln, w1, w2]
    in_specs += [_const_spec(w.shape) for w in weights]
    return pl.pallas_call(
        functools.partial(kernel_fn, alpha=alpha, nlt=nlt, tile0=tile0),
        grid=(b, n_tiles),
        in_specs=in_specs,
        out_specs=pl.BlockSpec((1, tm, d), lambda bi, i: (bi, i, 0)),
        out_shape=jax.ShapeDtypeStruct((b, n_tiles * tm, d), F32),
        compiler_params=_cparams(("parallel", "arbitrary")),
        name="post_" + kernel_fn.__name__ + ("_ctx" if tile0 else ""),
    )(*args, *weights)


_M_G, _M_QK, _M_V, _M_OG, _M_END = 0, 128, 1152, 2176, 3200


def _m_in_kernel(x_ref, xc_ref, xp_ref, xpc_ref, xn_ref, xnc_ref, mod_ref, w_ref, cw_ref, cb_ref, gb_ref,
                 qt_ref, k_ref, vt_ref, og_ref, gt1_ref, gt2_ref, gc_ref, *, nlt, seg_starts, seg_ends):
    i = pl.program_id(1)
    tm = x_ref.shape[1]
    shift = mod_ref[0, 0, 0:1, :]
    scale = mod_ref[0, 0, 1:2, :]
    h = (_select_rows(nlt, x_ref, xc_ref) * (1.0 + scale) + shift).astype(BF16)
    g = jnp.dot(h, w_ref[:, _M_G:_M_QK], preferred_element_type=F32) + gb_ref[...]
    lsg = jnp.minimum(g, 0.0) - jnp.log1p(jnp.exp(-jnp.abs(g)))
    r = lax.broadcasted_iota(jnp.int32, (tm, tm), 0)
    cidx = lax.broadcasted_iota(jnp.int32, (tm, tm), 1)
    same = (r // M_SCAN_CHUNK) == (cidx // M_SCAN_CHUNK)
    tril = jnp.where(jnp.logical_and(same, cidx <= r), 1.0, 0.0)
    triu = jnp.where(jnp.logical_and(same, cidx >= r), 1.0, 0.0)
    pre = jnp.dot(tril, lsg, precision=HIGHEST, preferred_element_type=F32)
    suf = jnp.dot(triu, lsg, precision=HIGHEST, preferred_element_type=F32)
    lane = lax.broadcasted_iota(jnp.int32, g.shape, 1)
    grp = lane // H_M
    gt1_ref[0] = jnp.where(grp == 1, pre, jnp.where(grp == 3, suf, g)).T
    bsum = jnp.where(grp == 1, pre, jnp.where(grp == 3, suf, 0.0))
    b_at_i = pltpu.roll(bsum, LANES - H_M, axis=1)
    r_gate = g - b_at_i
    pos = lax.broadcasted_iota(jnp.int32, g.shape, 0) % M_SCAN_CHUNK
    pmax = r_gate
    smax = r_gate
    step = 1
    while step < M_SCAN_CHUNK:
        pmax = jnp.where(pos >= step, jnp.maximum(pmax, pltpu.roll(pmax, step, axis=0)), pmax)
        smax = jnp.where(pos < M_SCAN_CHUNK - step, jnp.maximum(smax, pltpu.roll(smax, tm - step, axis=0)), smax)
        step *= 2
    gc_ref[0] = r_gate * LOG2E
    gt2_ref[0] = (b_at_i + jnp.where(grp == 0, pmax, smax)).T

    y = jnp.dot(h, w_ref[:, _M_QK:_M_END], preferred_element_type=F32)
    hp = (_select_rows(nlt, xp_ref, xpc_ref) * (1.0 + scale) + shift).astype(BF16)
    hn = (_select_rows(nlt, xn_ref, xnc_ref) * (1.0 + scale) + shift).astype(BF16)
    wqk = w_ref[:, _M_QK:_M_V]
    up = jnp.dot(hp, wqk, preferred_element_type=F32)[7:8, :]
    un = jnp.dot(hn, wqk, preferred_element_type=F32)[0:1, :]
    is_start = functools.reduce(jnp.logical_or, [i == s for s in seg_starts])
    is_end = functools.reduce(jnp.logical_or, [i == e for e in seg_ends])
    up = jnp.where(is_start, 0.0, up)
    un = jnp.where(is_end, 0.0, un)

    u = y[:, 0:_M_V - _M_QK]
    rows = lax.broadcasted_iota(jnp.int32, u.shape, 0)
    u_prev = jnp.where(rows == 0, up, pltpu.roll(u, 1, axis=0))
    u_next = jnp.where(rows == tm - 1, un, pltpu.roll(u, tm - 1, axis=0))
    qk = _silu(cw_ref[0:1, :] * u_prev + cw_ref[1:2, :] * u + cw_ref[2:3, :] * u_next + cb_ref[...])
    nq = H_M * M_DK
    for grp in range(nq // LANES):
        sl = slice(grp * LANES, (grp + 1) * LANES)
        qt_ref[0, sl, :] = qk[:, sl].T.astype(BF16)
    for hd in range(H_M):
        sl = slice(hd * LANES, (hd + 1) * LANES)
        vt_ref[0, sl, :] = y[:, _M_V - _M_QK + hd * LANES:_M_V - _M_QK + (hd + 1) * LANES].T.astype(BF16)
    k_ref[0] = (qk[:, nq:] * (M_DK ** -0.5)).astype(BF16)
    og_ref[0] = y[:, _M_OG - _M_QK:].astype(BF16)


def _m_in_proj(x_parts, modsel, w_ext, conv_w, conv_b, gate_b, nlt, t):
    b, d = modsel.shape[0], modsel.shape[-1]
    tm = ROW_TILE
    nt = t // tm
    r8 = tm // 8
    lat, ctx, ctx_tile0 = x_parts
    c0, n8_lat, n8_ctx = ctx_tile0 * r8, nlt * r8, (nt - nlt) * r8
    halo = lambda off, lo, n8: (lambda bi, i: (bi, lo + jnp.clip(i * r8 + off, 0, n8 - 1), 0))
    halo_ctx = lambda off: (lambda bi, i: (bi, c0 + jnp.clip((i - nlt) * r8 + off, 0, n8_ctx - 1), 0))
    x_specs, x_args = _split_rows_specs(x_parts, nlt, tm)
    x_specs += [pl.BlockSpec((1, 8, d), halo(-1, 0, n8_lat)), pl.BlockSpec((1, 8, d), halo_ctx(-1)),
                pl.BlockSpec((1, 8, d), halo(r8, 0, n8_lat)), pl.BlockSpec((1, 8, d), halo_ctx(r8))]
    x_args += [lat, ctx, lat, ctx]
    row = lambda bi, i: (bi, i, 0)
    col = lambda bi, i: (bi, 0, i)
    nk, n = H_M * M_DK, H_M * M_DV
    out_shapes = [jax.ShapeDtypeStruct((b, nk, t), BF16), jax.ShapeDtypeStruct((b, t, nk), BF16),
                  jax.ShapeDtypeStruct((b, n, t), BF16), jax.ShapeDtypeStruct((b, t, n), BF16)] + \
                 [jax.ShapeDtypeStruct((b, LANES, t), F32)] * 2 + [jax.ShapeDtypeStruct((b, t, LANES), F32)]
    out_specs = [pl.BlockSpec((1, nk, tm), col), pl.BlockSpec((1, tm, nk), row),
                 pl.BlockSpec((1, n, tm), col), pl.BlockSpec((1, tm, n), row)] + \
                [pl.BlockSpec((1, LANES, tm), col)] * 2 + [pl.BlockSpec((1, tm, LANES), row)]
    kern = functools.partial(_m_in_kernel, nlt=nlt, seg_starts=(0, nlt), seg_ends=(nlt - 1, nt - 1))
    return pl.pallas_call(
        kern,
        grid=(b, nt),
        in_specs=x_specs + [
                  pl.BlockSpec((1, 1, 6, d), lambda bi, i: (bi, (i >= nlt).astype(jnp.int32), 0, 0)),
                  _const_spec(w_ext.shape), _const_spec(conv_w.shape), _const_spec(conv_b.shape),
                  _const_spec(gate_b.shape)],
        out_specs=out_specs,
        out_shape=out_shapes,
        compiler_params=_cparams(("parallel", "arbitrary")),
        name="m_in_proj",
    )(*x_args, modsel, w_ext, conv_w, conv_b, gate_b)


def _mlstm_gates(gc, gt1, gt2, hd, backward, m_prev):
    L = M_SCAN_CHUNK
    gi = 2 * H_M if backward else 0
    i_row = gt1[gi + hd:gi + hd + 1, :]
    b_row = gt1[gi + H_M + hd:gi + H_M + hd + 1, :]
    mloc_row = gt2[gi + hd:gi + hd + 1, :]
    r_col = gc[:, gi + hd:gi + hd + 1]
    inter = b_row + m_prev
    m_t = jnp.maximum(inter, mloc_row)
    e = 0 if backward else L - 1
    b_end = b_row[:, e:e + 1]
    m_new = jnp.maximum(b_end + m_prev, mloc_row[:, e:e + 1])
    return i_row, b_row, r_col, inter, m_t, b_end, m_new


def _mlstm_kernel(*refs):
    ins, (hf_ref, hb_ref, cstate, mstate, sc_scr) = refs[:12], refs[12:]
    c = pl.program_id(1)
    L = M_SCAN_CHUNK

    @pl.when(c == 0)
    def _():
        cstate[...] = jnp.zeros_like(cstate)
        mstate[...] = jnp.zeros_like(mstate)

    chains = [(d, hd) for d in range(2) for hd in range(H_M)]

    def operands(d, hd):
        k_ref, qt_ref, vt_ref, gc_ref, gt1_ref, gt2_ref = ins[6 * d:6 * d + 6]
        pair = slice((hd // 2) * LANES, (hd // 2 + 1) * LANES)
        k_pair = k_ref[0, :, pair]
        lane = lax.broadcasted_iota(jnp.int32, k_pair.shape, 1)
        k_own = jnp.where((lane // M_DK) == (hd % 2), k_pair, jnp.zeros_like(k_pair))
        gates = _mlstm_gates(gc_ref[0], gt1_ref[0], gt2_ref[0], hd, d == 1, mstate[d * H_M + hd][0:1, 0:1])
        return k_own, qt_ref[0, pair, :], vt_ref[0, hd * LANES:(hd + 1) * LANES, :], gates

    ones_rows = jnp.ones((M_ONES_ROWS, L), BF16)
    s_idx = lax.broadcasted_iota(jnp.int32, (L, L), 0)
    t_idx = lax.broadcasted_iota(jnp.int32, (L, L), 1)

    for ch, (d, hd) in enumerate(chains):
        k_own, qt, _, (_, b_row, r_col, _, m_t, _, _) = operands(d, hd)
        visible = (s_idx >= t_idx) if d == 1 else (s_idx <= t_idx)
        dt = jnp.exp2(jnp.where(visible, r_col + (b_row - m_t) * LOG2E, NEG))
        sc_scr[ch] = (jnp.dot(k_own, qt, preferred_element_type=F32) * dt).astype(BF16)

    for ch, (d, hd) in enumerate(chains):
        _, qt, vt, (_, _, _, inter, m_t, _, _) = operands(d, hd)
        vext_t = jnp.concatenate([vt, ones_rows], axis=0)
        nd_t = jnp.dot(vext_t, sc_scr[ch], preferred_element_type=F32) \
            + jnp.dot(cstate[ch].astype(BF16), qt, preferred_element_type=F32) * jnp.exp(inter - m_t)
        den = jnp.broadcast_to(nd_t[LANES:LANES + 1, :], (LANES, L))
        h_t = nd_t[:LANES] / jnp.maximum(jnp.abs(den), jnp.exp(-m_t))
        (hb_ref if d else hf_ref)[0, :, hd * LANES:(hd + 1) * LANES] = h_t.T.astype(BF16)

    for ch, (d, hd) in enumerate(chains):
        k_own, _, vt, (i_row, b_row, _, _, _, b_end, m_new) = operands(d, hd)
        m_prev = mstate[ch][0:1, 0:1]
        vext_t = jnp.concatenate([vt, ones_rows], axis=0)
        w_row = jnp.exp(b_end - b_row + i_row - m_new)
        cstate[ch] = jnp.exp(b_end + m_prev - m_new) * cstate[ch] \
            + jnp.dot((vext_t.astype(F32) * w_row).astype(BF16), k_own, preferred_element_type=F32)
        mstate[ch] = jnp.broadcast_to(m_new, mstate.shape[1:])


def _mlstm(k, qt, vt, gc, gt1, gt2, n_lat):
    b, t, nk = k.shape
    n = vt.shape[1]
    L = M_SCAN_CHUNK
    nb = t // L
    fwd, bwd, fwd_t, bwd_t = _scan_maps(n_lat // L, nb - n_lat // L)

    def specs(rmap, tmap):
        return [pl.BlockSpec((1, L, nk), rmap), pl.BlockSpec((1, nk, L), tmap), pl.BlockSpec((1, n, L), tmap),
                pl.BlockSpec((1, L, LANES), rmap), pl.BlockSpec((1, 4 * H_M, L), tmap),
                pl.BlockSpec((1, 4 * H_M, L), tmap)]

    out = jax.ShapeDtypeStruct((b, t, H_M * M_DV), BF16)
    args = (k, qt, vt, gc, gt1, gt2)
    return pl.pallas_call(
        _mlstm_kernel,
        grid=(b, nb),
        in_specs=specs(fwd, fwd_t) + specs(bwd, bwd_t),
        out_specs=[pl.BlockSpec((1, L, H_M * M_DV), fwd), pl.BlockSpec((1, L, H_M * M_DV), bwd)],
        out_shape=[out, out],
        scratch_shapes=[pltpu.VMEM((2 * H_M, LANES + M_ONES_ROWS, LANES), F32),
                        pltpu.VMEM((2 * H_M, 8, LANES), F32),
                        pltpu.VMEM((2 * H_M, L, L), BF16)],
        compiler_params=_cparams(("parallel", "arbitrary")),
        name="mlstm_scan",
    )(*args, *args)


def _rope_tables(n_lat, ctx_len, d, lane0):
    rows = n_lat // GRID_W
    row = np.repeat(np.arange(rows, dtype=np.float64), GRID_W)
    col = np.tile(np.arange(GRID_W, dtype=np.float64), rows)
    da = d // 2
    inv = ROPE_BASE ** (-np.arange(0, da, 2, dtype=np.float64) / da)
    ar = row[:, None] * inv
    ac = col[:, None] * inv
    tab = np.zeros((2, n_lat + ctx_len, LANES), np.float32)
    tab[0] = 1.0
    tab[0, :n_lat, lane0:lane0 + d] = np.concatenate([np.cos(ar), np.cos(ar), np.cos(ac), np.cos(ac)], axis=-1)
    tab[1, :n_lat, lane0:lane0 + d] = np.concatenate([-np.sin(ar), np.sin(ar), -np.sin(ac), np.sin(ac)], axis=-1)
    return jnp.asarray(tab)


def _pad_heads(w, n_heads, width):
    k = w.shape[0]
    w = w.reshape(k, n_heads, width)
    return jnp.pad(w, ((0, 0), (0, 0), (0, LANES - width))).reshape(k, n_heads * LANES)


def _prep_ab_weights(w_in, w_uq, w_ukv):
    o1 = MLA_Q_LORA
    o2 = o1 + MLA_KV_LORA
    o3 = o2 + MLA_ROPE
    o4 = o3 + H_RET * RET_DK
    o5 = o4 + H_RET * RET_DK
    o6 = o5 + H_RET * RET_DV
    cq, ckv, kr, rq, rk, rv, rg = (w_in[:, :o1], w_in[:, o1:o2], w_in[:, o2:o3], w_in[:, o3:o4],
                                   w_in[:, o4:o5], w_in[:, o5:o6], w_in[:, o6:])
    kr_blk = jnp.pad(kr, ((0, 0), (MLA_NOPE, LANES - MLA_NOPE - MLA_ROPE)))
    w_ext = jnp.concatenate([cq, ckv, rq, rk, rv, rg, kr_blk], axis=1)

    wuq_ext = _pad_heads(w_uq, H_MLA, MLA_NOPE + MLA_ROPE)
    ukv = w_ukv.reshape(-1, H_MLA, MLA_NOPE + MLA_V)
    k_nope = _pad_heads(ukv[..., :MLA_NOPE].reshape(-1, H_MLA * MLA_NOPE), H_MLA, MLA_NOPE)
    v_w = ukv[..., MLA_NOPE:]
    zeros = jnp.zeros_like(v_w)
    even = (np.arange(H_MLA) % 2 == 0)[None, :, None]
    v_cols = jnp.concatenate([jnp.where(even, v_w, zeros), jnp.where(even, zeros, v_w)], axis=-1)
    wukv_ext = jnp.concatenate([k_nope, v_cols.reshape(-1, H_MLA * LANES)], axis=1)
    return w_ext.astype(BF16), wuq_ext.astype(BF16), wukv_ext.astype(BF16)


def _prep_m_weights(w_in, conv_w, conv_b, gate_b):
    o1 = 2 * H_M * M_DK
    o2 = o1 + H_M * M_DV
    o3 = o2 + H_M * M_DV
    g_pad = jnp.pad(w_in[:, o3:], ((0, 0), (0, LANES - 4 * H_M)))
    w_ext = jnp.concatenate([g_pad, w_in[:, :o3]], axis=1).astype(BF16)
    gb = jnp.pad(gate_b.reshape(1, 4 * H_M), ((0, 0), (0, LANES - 4 * H_M)))
    return w_ext, conv_w, conv_b[None, :], gb


def kernel(x, c, ctx, c_ctx, mod_w, mod_b, ln_g, ln_b, ffn_w_in, ffn_w_out, ab_w_in, mla_q_norm, mla_w_uq,
           mla_kv_norm, mla_w_ukv, ret_log_decay, ab_w_out, m_w_in, m_conv_w, m_conv_b, m_gate_b, m_norm_g, m_w_out):
    b, s, d = x.shape
    ctx_len = ctx.shape[1]
    depth = mod_w.shape[0]
    assert ctx_len % ROW_TILE == 0 and s % ROW_TILE == 0 and s % GRID_W == 0 and b + 1 <= 8
    nlt = s // ROW_TILE
    alpha = (2 * depth) ** 0.25

    c_rows = jnp.zeros((8, d), F32).at[:b].set(c).at[b].set(c_ctx)
    mods = _modulation(c_rows, mod_w, mod_b).reshape(depth, 8, 6, d)
    tab_mla = _rope_tables(s, ctx_len, MLA_ROPE, MLA_NOPE)
    tab_ret = _rope_tables(s, ctx_len, RET_DK, 0)

    t = s + ctx_len
    x_lat, x_ctx = x, ctx
    for l in range(depth):
        last = l == depth - 1
        j = l // 2
        modsel = jnp.stack([mods[l, :b], jnp.broadcast_to(mods[l, b], (b, 6, d))], axis=1)
        ln = jnp.stack([ln_g[l, 0], ln_b[l, 0], ln_g[l, 1], ln_b[l, 1]])
        w1 = ffn_w_in[l].astype(BF16)
        w2 = ffn_w_out[l].astype(BF16)
        x_parts = (x_lat, x_ctx, 0)
        if l % 2 == 0:
            w_ext, wuq_ext, wukv_ext = _prep_ab_weights(ab_w_in[j], mla_w_uq[j], mla_w_ukv[j])
            q, kt, v, rq, rkt, rv, rg = _ab_in_proj(x_parts, modsel, w_ext, mla_q_norm[j][None, :],
                                                    mla_kv_norm[j][None, :], wuq_ext, wukv_ext,
                                                    tab_mla, tab_ret, nlt, t)
            att_lat = _attention(q, kt, v, s, False)
            att_ctx = None if last else _attention(q, kt, v, s, True)
            ret_f, ret_b = _retention(rq, rkt, rv, ret_log_decay[j], s)
            post_fn, acts, consts, w_out = _ab_post_kernel, [ret_f, ret_b, rg], [], ab_w_out[j]
        else:
            w_ext, cw, cb, gb = _prep_m_weights(m_w_in[j], m_conv_w[j], m_conv_b[j], m_gate_b[j])
            qt, k, vt, og, gt1, gt2, gc = _m_in_proj(x_parts, modsel, w_ext, cw, cb, gb, nlt, t)
            hf, hb = _mlstm(k, qt, vt, gc, gt1, gt2, s)
            post_fn, acts, consts, w_out = _m_post_kernel, [hf, hb, og], [m_norm_g[j][None, :]], m_w_out[j]
            att_lat = att_ctx = None

        def post(tm, tile0, n_tiles, for_ctx):
            pair = (lambda a_lat, a_ctx: (a_lat, a_ctx, 0) if for_ctx else (a_lat, a_lat, 0))
            att = [] if att_lat is None else [pair(att_lat, att_ctx)]
            return _post(post_fn, pair(x_lat, x_ctx), modsel, att + acts, consts, w_out.astype(BF16),
                         ln, w1, w2, s, tm, tile0, n_tiles, alpha)

        new_lat = post(POST_TILE, 0, s // POST_TILE, False)
        if not last:
            x_ctx = post(ROW_TILE, nlt, ctx_len // ROW_TILE, True)
        x_lat = new_lat
    return x_lat
```
